```python
import math, functools
import jax, jax.numpy as jnp
from jax import lax
import numpy as np

D_MODEL = 2048
BATCH = 1
SEQ = 8192
DEPTH = 2

GRID_W = 64
CTX_LEN = 256
GROUP_W = D_MODEL // 4
MIX_W = 4 * GROUP_W
CHUNK = 64
SHORT_CONV = 5
FFN_CONV = 3
D_FF = ((8 * D_MODEL // 3 + 255) // 256) * 256
N_MOD = 6
EPS = 1e-6

DN_HEADS = 4
DN_DK = GROUP_W // DN_HEADS
DN_DV = GROUP_W // DN_HEADS
DN_QKV = 2 * DN_HEADS * DN_DK + DN_HEADS * DN_DV

SSD_HEAD_DIM = 64
SSD_HEADS = GROUP_W // SSD_HEAD_DIM
SSD_GROUPS = 2
SSD_STATE = 128
SSD_XBC = SSD_HEADS * SSD_HEAD_DIM + 2 * SSD_GROUPS * SSD_STATE

LRU_W = GROUP_W
LRU_BLOCKS = 8
LRU_BW = LRU_W // LRU_BLOCKS
LRU_C = 8.0

ML_HEADS = 4
ML_DV = GROUP_W // ML_HEADS
ML_DK = ML_DV // 2

IN_SPLITS = (
    ("dn_q", DN_HEADS * DN_DK), ("dn_k", DN_HEADS * DN_DK), ("dn_v", DN_HEADS * DN_DV),
    ("dn_z", DN_HEADS * DN_DV), ("dn_a", 2 * DN_HEADS), ("dn_b", 2 * DN_HEADS),
    ("ssd_x", SSD_HEADS * SSD_HEAD_DIM), ("ssd_z", SSD_HEADS * SSD_HEAD_DIM),
    ("ssd_B", SSD_GROUPS * SSD_STATE), ("ssd_C", SSD_GROUPS * SSD_STATE), ("ssd_dt", 2 * SSD_HEADS),
    ("lru_x", LRU_W), ("lru_g", LRU_W),
    ("ml_q", ML_HEADS * ML_DK), ("ml_k", ML_HEADS * ML_DK), ("ml_v", ML_HEADS * ML_DV),
    ("ml_o", ML_HEADS * ML_DV), ("ml_i", 2 * ML_HEADS), ("ml_f", 2 * ML_HEADS),
)
D_IN = sum(w for _, w in IN_SPLITS)

kernel_name = "hybrid_parallel_groups_prefix_ctx"

F32 = jnp.float32


def _rmsnorm(x, g):
    xf = x.astype(F32)
    y = xf * lax.rsqrt(jnp.mean(xf * xf, axis=-1, keepdims=True) + EPS)
    return (y * g.astype(F32)).astype(x.dtype)


def _l2norm(x):
    return x * lax.rsqrt(jnp.sum(x * x, axis=-1, keepdims=True) + EPS)


def _dwconv(x, w, b=None):
    k = w.shape[0]
    y = lax.conv_general_dilated(x, w[:, None, :].astype(x.dtype), window_strides=(1,),
                                 padding=((k // 2, k // 2),),
                                 dimension_numbers=("NWC", "WIO", "NWC"),
                                 feature_group_count=w.shape[1])
    return y if b is None else y + b.astype(x.dtype)


def _split_proj(p):
    out, off = {}, 0
    for name, width in IN_SPLITS:
        out[name] = p[..., off:off + width]
        off += width
    return out


def _to_col_major(t):
    b_, n, ch = t.shape
    rows = n // GRID_W
    return t.reshape(b_, rows, GRID_W, ch).transpose(0, 2, 1, 3).reshape(b_, n, ch)


def _from_col_major(t):
    b_, n, ch = t.shape
    rows = n // GRID_W
    return t.reshape(b_, GRID_W, rows, ch).transpose(0, 2, 1, 3).reshape(b_, n, ch)


def _to_chunks(t):
    b_, n, h = t.shape[:3]
    t = t.reshape(b_, n // CHUNK, CHUNK, h, *t.shape[3:])
    return jnp.moveaxis(t, (1, 3), (0, 2))


def _from_chunks(t):
    t = jnp.moveaxis(t, (0, 2), (1, 3))
    return t.reshape(t.shape[0], -1, *t.shape[3:])


def _seg_decay(cs):
    n = cs.shape[-1]
    tri = jnp.tril(jnp.ones((n, n), bool))
    diff = cs[..., :, None] - cs[..., None, :]
    return jnp.where(tri, jnp.exp(jnp.where(tri, diff, 0.0)), 0.0)


def _bidir(core, ctx_args, lat_args):
    flip = functools.partial(jnp.flip, axis=1)
    yc_f, s_f = core(ctx_args[0], None)
    yl_f, _ = core(lat_args[0], s_f)
    yc_b, s_b = core(tuple(map(flip, ctx_args[1])), None)
    yl_b, _ = core(tuple(map(flip, lat_args[1])), s_b)
    return yc_f + flip(yc_b), yl_f + flip(yl_b)


def _gated_delta_chunked(args, s0):
    q, k, v, g, beta = args
    b_, _, h, dk = q.shape
    dv = v.shape[-1]
    q, k, v, g, beta = map(_to_chunks, (q, k, v, g, beta))
    gc = jnp.cumsum(g, axis=-1)
    decay = _seg_decay(gc)
    strict = jnp.tril(jnp.ones((CHUNK, CHUNK), bool), -1)
    kb = k * beta[..., None]
    amat = jnp.eye(CHUNK, dtype=F32) + jnp.where(
        strict, jnp.einsum("nbhcd,nbhsd->nbhcs", kb, k) * decay, 0.0)
    rhs = jnp.concatenate([v * beta[..., None], kb * jnp.exp(gc)[..., None]], axis=-1)
    sol = lax.linalg.triangular_solve(amat, rhs, left_side=True, lower=True, unit_diagonal=True)
    u, w = sol[..., :dv], sol[..., dv:]
    attn = jnp.einsum("nbhcd,nbhsd->nbhcs", q, k) * decay
    if s0 is None:
        s0 = jnp.zeros((b_, h, dk, dv), F32)

    def step(s, xs):
        q_i, k_i, u_i, w_i, a_i, gc_i = xs
        v_new = u_i - w_i @ s
        o = (q_i * jnp.exp(gc_i)[..., None]) @ s + a_i @ v_new
        g_last = gc_i[..., -1:]
        s = s * jnp.exp(g_last)[..., None] + jnp.einsum(
            "bhcd,bhce->bhde", k_i * jnp.exp(g_last - gc_i)[..., None], v_new)
        return s, o

    s_fin, o = lax.scan(step, s0, (q, k, u, w, attn, gc))
    return _from_chunks(o), s_fin


def _gated_deltanet(pc, pl, conv_w, a_log, dt_bias, norm_g):
    hk = DN_HEADS * DN_DK

    def prep(p):
        qkv = jnp.concatenate([p["dn_q"], p["dn_k"], p["dn_v"]], axis=-1)
        qkv = jax.nn.silu(_dwconv(qkv, conv_w)).astype(F32)
        b_, t = qkv.shape[:2]
        q, k, v = jnp.split(qkv, [hk, 2 * hk], axis=-1)
        q = _l2norm(q.reshape(b_, t, DN_HEADS, DN_DK)) * DN_DK ** -0.5
        k = _l2norm(k.reshape(b_, t, DN_HEADS, DN_DK))
        v = v.reshape(b_, t, DN_HEADS, DN_DV)
        a = p["dn_a"].astype(F32).reshape(b_, t, 2, DN_HEADS)
        g = -jnp.exp(a_log.astype(F32)) * jax.nn.softplus(a + dt_bias.astype(F32))
        beta = jax.nn.sigmoid(p["dn_b"].astype(F32).reshape(b_, t, 2, DN_HEADS))
        return tuple((q, k, v, g[:, :, d], beta[:, :, d]) for d in range(2))

    def finish(o, p):
        b_, t = o.shape[:2]
        z = p["dn_z"].astype(F32).reshape(b_, t, DN_HEADS, DN_DV)
        return (_rmsnorm(o, norm_g) * jax.nn.silu(z)).reshape(b_, t, DN_HEADS * DN_DV)

    oc, ol = _bidir(_gated_delta_chunked, prep(pc), prep(pl))
    return finish(oc, pc), finish(ol, pl)


def _ssd_chunked(args, s0):
    xd, a, bm, cm = args
    b_, t, h, p = xd.shape
    n, nc = bm.shape[-1], t // CHUNK
    xd = xd.reshape(b_, nc, CHUNK, h, p)
    bm = bm.reshape(b_, nc, CHUNK, h, n)
    cm = cm.reshape(b_, nc, CHUNK, h, n)
    acs = jnp.cumsum(a.reshape(b_, nc, CHUNK, h), axis=2)
    lmat = _seg_decay(jnp.moveaxis(acs, 3, 2))
    scores = jnp.einsum("bclhn,bcshn->bchls", cm, bm) * lmat
    y_diag = jnp.einsum("bchls,bcshp->bclhp", scores, xd)
    w_state = jnp.exp(acs[:, :, -1:, :] - acs)
    local = jnp.einsum("bclhn,bclhp->bchpn", bm, xd * w_state[..., None])
    chunk_dec = jnp.exp(acs[:, :, -1, :])
    if s0 is None:
        s0 = jnp.zeros((b_, h, p, n), F32)

    def step(s, inp):
        dec, loc = inp
        return s * dec[..., None, None] + loc, s

    s_fin, s_in = lax.scan(step, s0, (jnp.moveaxis(chunk_dec, 1, 0), jnp.moveaxis(local, 1, 0)))
    s_in = jnp.moveaxis(s_in, 0, 1)
    y_off = jnp.einsum("bclhn,bchpn->bclhp", cm, s_in) * jnp.exp(acs)[..., None]
    return (y_diag + y_off).reshape(b_, t, h, p), s_fin


def _mamba2_ssd(pc, pl, conv_w, conv_b, a_log, dt_bias, d_skip, norm_g):
    hp, gn = SSD_HEADS * SSD_HEAD_DIM, SSD_GROUPS * SSD_STATE
    rep = SSD_HEADS // SSD_GROUPS

    def prep(p):
        xbc = jnp.concatenate([p["ssd_x"], p["ssd_B"], p["ssd_C"]], axis=-1)
        xbc = jax.nn.silu(_dwconv(xbc, conv_w, conv_b)).astype(F32)
        b_, t = xbc.shape[:2]
        xs, bm, cm = jnp.split(xbc, [hp, hp + gn], axis=-1)
        xs = xs.reshape(b_, t, SSD_HEADS, SSD_HEAD_DIM)
        bm = jnp.repeat(bm.reshape(b_, t, SSD_GROUPS, SSD_STATE), rep, axis=2)
        cm = jnp.repeat(cm.reshape(b_, t, SSD_GROUPS, SSD_STATE), rep, axis=2)
        dt = jax.nn.softplus(p["ssd_dt"].astype(F32).reshape(b_, t, 2, SSD_HEADS) + dt_bias.astype(F32))
        a = -jnp.exp(a_log.astype(F32)) * dt
        xd = xs[:, :, None] * dt[..., None]
        return tuple((xd[:, :, d], a[:, :, d], bm, cm) for d in range(2)), xs

    def finish(y, xs, p):
        b_, t = y.shape[:2]
        y = (y + d_skip.astype(F32)[:, None] * xs).reshape(b_, t, hp)
        y = y * jax.nn.silu(p["ssd_z"].astype(F32))
        y = _rmsnorm(y.reshape(b_, t, SSD_GROUPS, hp // SSD_GROUPS), norm_g.reshape(SSD_GROUPS, -1))
        return y.reshape(b_, t, hp)

    (args_c, xs_c), (args_l, xs_l) = prep(pc), prep(pl)
    yc, yl = _bidir(_ssd_chunked, args_c, args_l)
    return finish(yc, xs_c, pc), finish(yl, xs_l, pl)


def _linear_scan(args, h0):
    a, b = args
    if h0 is not None:
        b = b.at[:, 0].add(a[:, 0] * h0)

    def comb(l, r):
        return l[0] * r[0], r[0] * l[1] + r[1]

    _, h = lax.associative_scan(comb, (a, b), axis=1)
    return h, h[:, -1]


def _rglru(pc, pl, conv_w, conv_b, w_a, b_a, w_i, b_i, lam):
    def prep(p, col_major):
        xr = _to_col_major(p["lru_x"]) if col_major else p["lru_x"]
        xr = _dwconv(xr, conv_w, conv_b).astype(F32)
        b_, t = xr.shape[:2]
        xb = xr.reshape(b_, t, LRU_BLOCKS, LRU_BW)
        r = jax.nn.sigmoid(jnp.einsum("btnj,dnjk->btdnk", xb, w_a.astype(F32)).reshape(b_, t, 2, LRU_W)
                           + b_a.astype(F32))
        i = jax.nn.sigmoid(jnp.einsum("btnj,dnjk->btdnk", xb, w_i.astype(F32)).reshape(b_, t, 2, LRU_W)
                           + b_i.astype(F32))
        log_a = -LRU_C * r * jax.nn.softplus(-lam.astype(F32))
        b = jnp.sqrt(-jnp.expm1(2.0 * log_a)) * i * xr[:, :, None]
        a = jnp.exp(log_a)
        return tuple((a[:, :, d], b[:, :, d]) for d in range(2))

    hc, hl = _bidir(_linear_scan, prep(pc, False), prep(pl, True))
    hl = _from_col_major(hl)
    return (hc * jax.nn.gelu(pc["lru_g"].astype(F32)),
            hl * jax.nn.gelu(pl["lru_g"].astype(F32)))


def _mlstm_chunked(args, state):
    q, k, v, ig, lf = map(_to_chunks, args)
    nb, hh, dk, dv = q.shape[1], q.shape[2], q.shape[-1], v.shape[-1]
    if state is None:
        state = (jnp.zeros((nb, hh, dk, dv), F32), jnp.zeros((nb, hh, dk), F32),
                 jnp.zeros((nb, hh), F32))
    tri = jnp.tril(jnp.ones((CHUNK, CHUNK), bool))

    def step(carry, xs):
        c_s, n_s, m_s = carry
        qc, kc, vc, ic, fc = xs
        b = jnp.cumsum(fc, axis=-1)
        log_d = jnp.where(tri, b[..., :, None] - b[..., None, :] + ic[..., None, :], -jnp.inf)
        inter = b + m_s[..., None]
        m = jnp.maximum(inter, jnp.max(log_d, axis=-1))
        s = jnp.einsum("bhcd,bhsd->bhcs", qc, kc) * jnp.exp(log_d - m[..., None])
        w_inter = jnp.exp(inter - m)
        num = w_inter[..., None] * (qc @ c_s) + s @ vc
        den = w_inter * jnp.einsum("bhcd,bhd->bhc", qc, n_s) + jnp.sum(s, axis=-1)
        h = num / jnp.maximum(jnp.abs(den), jnp.exp(-m))[..., None]
        b_last = b[..., -1]
        log_g = b_last[..., None] - b + ic
        m_new = jnp.maximum(b_last + m_s, jnp.max(log_g, axis=-1))
        dec = jnp.exp(b_last + m_s - m_new)
        wk = jnp.exp(log_g - m_new[..., None])[..., None] * kc
        c_s = dec[..., None, None] * c_s + jnp.einsum("bhcd,bhce->bhde", wk, vc)
        n_s = dec[..., None] * n_s + jnp.sum(wk, axis=-2)
        return (c_s, n_s, m_new), h

    state, h = lax.scan(step, state, (q, k, v, ig, lf))
    return _from_chunks(h), state


def _mlstm(pc, pl, ig_b, fg_b, norm_g):
    def prep(p):
        b_, t = p["ml_q"].shape[:2]
        q = p["ml_q"].astype(F32).reshape(b_, t, ML_HEADS, ML_DK) * ML_DK ** -0.5
        k = p["ml_k"].astype(F32).reshape(b_, t, ML_HEADS, ML_DK)
        v = p["ml_v"].astype(F32).reshape(b_, t, ML_HEADS, ML_DV)
        ig = p["ml_i"].astype(F32).reshape(b_, t, 2, ML_HEADS) + ig_b.astype(F32)
        lf = jax.nn.log_sigmoid(p["ml_f"].astype(F32).reshape(b_, t, 2, ML_HEADS) + fg_b.astype(F32))
        return tuple((q, k, v, ig[:, :, d], lf[:, :, d]) for d in range(2))

    def finish(h, p):
        b_, t = h.shape[:2]
        h = _rmsnorm(h, norm_g.reshape(ML_HEADS, ML_DV)).reshape(b_, t, ML_HEADS * ML_DV)
        return h * jax.nn.sigmoid(p["ml_o"].astype(F32))

    hc, hl = _bidir(_mlstm_chunked, prep(pc), prep(pl))
    return finish(hc, pc), finish(hl, pl)


def _conv_ffn(h, w_up, conv_w, w_down):
    u, g = jnp.split(h @ w_up, 2, axis=-1)
    return (jax.nn.silu(_dwconv(g, conv_w)) * u) @ w_down


def setup_inputs(seed: int = 0) -> dict:
    key = jax.random.key(seed)
    ks = iter(jax.random.split(key, 40))
    L, D = DEPTH, D_MODEL

    def nrm(shape, scale=1.0):
        return jax.random.normal(next(ks), shape, F32) * scale

    def unif(shape, lo, hi):
        return jax.random.uniform(next(ks), shape, F32, lo, hi)

    def gain(shape):
        return 1.0 + nrm(shape, 0.05)

    def dt_bias(shape):
        dt = jnp.exp(unif(shape, math.log(1e-3), math.log(1e-1)))
        return dt + jnp.log(-jnp.expm1(-dt))

    def lru_lambda(shape):
        s = unif(shape, 0.9, 0.999) ** (1.0 / LRU_C)
        return jnp.log(s) - jnp.log1p(-s)

    return {
        "x": nrm((BATCH, SEQ, D)),
        "c": nrm((BATCH, D)),
        "ctx": nrm((BATCH, CTX_LEN, D)),
        "c_ctx": nrm((D,)),
        "ada_w": nrm((L, D, N_MOD * D), D ** -0.5),
        "ada_b": nrm((L, N_MOD * D), 0.01),
        "norm1_g": gain((L, D)),
        "norm2_g": gain((L, D)),
        "w_in": nrm((L, D, D_IN), D ** -0.5),
        "dn_conv_w": nrm((L, SHORT_CONV, DN_QKV), SHORT_CONV ** -0.5),
        "dn_a_log": jnp.log(unif((L, 2, DN_HEADS), 1.0, 16.0)),
        "dn_dt_bias": dt_bias((L, 2, DN_HEADS)),
        "dn_norm_g": gain((L, DN_DV)),
        "ssd_conv_w": nrm((L, SHORT_CONV, SSD_XBC), SHORT_CONV ** -0.5),
        "ssd_conv_b": nrm((L, SSD_XBC), 0.01),
        "ssd_a_log": jnp.log(unif((L, 2, SSD_HEADS), 1.0, 16.0)),
        "ssd_dt_bias": dt_bias((L, 2, SSD_HEADS)),
        "ssd_d": 1.0 + nrm((L, SSD_HEADS), 0.1),
        "ssd_norm_g": gain((L, SSD_HEADS * SSD_HEAD_DIM)),
        "lru_conv_w": nrm((L, SHORT_CONV, LRU_W), SHORT_CONV ** -0.5),
        "lru_conv_b": nrm((L, LRU_W), 0.01),
        "lru_w_a": nrm((L, 2, LRU_BLOCKS, LRU_BW, LRU_BW), LRU_BW ** -0.5),
        "lru_b_a": nrm((L, 2, LRU_W), 0.01),
        "lru_w_i": nrm((L, 2, LRU_BLOCKS, LRU_BW, LRU_BW), LRU_BW ** -0.5),
        "lru_b_i": nrm((L, 2, LRU_W), 0.01),
        "lru_lambda": lru_lambda((L, 2, LRU_W)),
        "ml_ig_b": nrm((L, 2, ML_HEADS), 0.1),
        "ml_fg_b": unif((L, 2, ML_HEADS), 3.0, 6.0),
        "ml_norm_g": gain((L, ML_HEADS * ML_DV)),
        "w_out": nrm((L, MIX_W, D), MIX_W ** -0.5),
        "ffn_w_up": nrm((L, D, 2 * D_FF), D ** -0.5),
        "ffn_conv_w": nrm((L, FFN_CONV, D_FF), FFN_CONV ** -0.5),
        "ffn_w_down": nrm((L, D_FF, D), D_FF ** -0.5),
        "final_norm_g": gain((D,)),
    }


def reference(x, c, ctx, c_ctx, ada_w, ada_b, norm1_g, norm2_g, w_in,
              dn_conv_w, dn_a_log, dn_dt_bias, dn_norm_g,
              ssd_conv_w, ssd_conv_b, ssd_a_log, ssd_dt_bias, ssd_d, ssd_norm_g,
              lru_conv_w, lru_conv_b, lru_w_a, lru_b_a, lru_w_i, lru_b_i, lru_lambda,
              ml_ig_b, ml_fg_b, ml_norm_g, w_out, ffn_w_up, ffn_conv_w, ffn_w_down,
              final_norm_g):
    lat, hctx = x, ctx
    nb = c.shape[0]
    for l in range(DEPTH):
        mod_l = (jax.nn.silu(c) @ ada_w[l] + ada_b[l]).reshape(nb, N_MOD, 1, D_MODEL)
        mod_c = (jax.nn.silu(c_ctx) @ ada_w[l] + ada_b[l]).reshape(N_MOD, D_MODEL)
        sh1_l, sc1_l, g1_l, sh2_l, sc2_l, g2_l = (mod_l[:, j] for j in range(N_MOD))
        sh1_c, sc1_c, g1_c, sh2_c, sc2_c, g2_c = (mod_c[j] for j in range(N_MOD))

        hl = _rmsnorm(lat, norm1_g[l]) * (1.0 + sc1_l) + sh1_l
        hc = _rmsnorm(hctx, norm1_g[l]) * (1.0 + sc1_c) + sh1_c
        pl = _split_proj(hl @ w_in[l])
        pc = _split_proj(hc @ w_in[l])
        a_c, a_l = _gated_deltanet(pc, pl, dn_conv_w[l], dn_a_log[l], dn_dt_bias[l], dn_norm_g[l])
        b_c, b_l = _mamba2_ssd(pc, pl, ssd_conv_w[l], ssd_conv_b[l], ssd_a_log[l], ssd_dt_bias[l],
                               ssd_d[l], ssd_norm_g[l])
        r_c, r_l = _rglru(pc, pl, lru_conv_w[l], lru_conv_b[l], lru_w_a[l], lru_b_a[l],
                          lru_w_i[l], lru_b_i[l], lru_lambda[l])
        m_c, m_l = _mlstm(pc, pl, ml_ig_b[l], ml_fg_b[l], ml_norm_g[l])
        mix_l = jnp.concatenate([a_l, b_l, r_l, m_l], axis=-1).astype(lat.dtype)
        lat = lat + g1_l * (mix_l @ w_out[l])

        hl = _rmsnorm(lat, norm2_g[l]) * (1.0 + sc2_l) + sh2_l
        lat = lat + g2_l * _conv_ffn(hl, ffn_w_up[l], ffn_conv_w[l], ffn_w_down[l])

        if l < DEPTH - 1:
            mix_c = jnp.concatenate([a_c, b_c, r_c, m_c], axis=-1).astype(hctx.dtype)
            hctx = hctx + g1_c * (mix_c @ w_out[l])
            hc = _rmsnorm(hctx, norm2_g[l]) * (1.0 + sc2_c) + sh2_c
            hctx = hctx + g2_c * _conv_ffn(hc, ffn_w_up[l], ffn_conv_w[l], ffn_w_down[l])

    return _rmsnorm(lat, final_norm_g)
```

```python
import functools

import jax
import jax.numpy as jnp
from jax import lax
from jax.experimental import pallas as pl
from jax.experimental.pallas import tpu as pltpu

F32 = jnp.float32
BF16 = jnp.bfloat16
HI = lax.Precision.HIGHEST

D_MODEL = 2048
DEPTH = 2
GRID_W = 64
GROUP_W = D_MODEL // 4
CHUNK = 64
SHORT_CONV = 5
FFN_CONV = 3
D_FF = ((8 * D_MODEL // 3 + 255) // 256) * 256
N_MOD = 6
EPS = 1e-6

DN_HEADS = 4
DN_DK = GROUP_W // DN_HEADS
DN_DV = GROUP_W // DN_HEADS
SSD_HEAD_DIM = 64
SSD_HEADS = GROUP_W // SSD_HEAD_DIM
SSD_GROUPS = 2
SSD_STATE = 128
LRU_W = GROUP_W
LRU_BLOCKS = 8
LRU_BW = LRU_W // LRU_BLOCKS
LRU_C = 8.0
ML_HEADS = 4
ML_DV = GROUP_W // ML_HEADS
ML_DK = ML_DV // 2

_SRC = {}
_off = 0
for _name, _w in (
    ("dn_q", 512), ("dn_k", 512), ("dn_v", 512), ("dn_z", 512), ("dn_a", 8), ("dn_b", 8),
    ("ssd_x", 512), ("ssd_z", 512), ("ssd_B", 256), ("ssd_C", 256), ("ssd_dt", 16),
    ("lru_x", 512), ("lru_g", 512),
    ("ml_q", 256), ("ml_k", 256), ("ml_v", 512), ("ml_o", 512), ("ml_i", 8), ("ml_f", 8),
):
    _SRC[_name] = (_off, _w)
    _off += _w
D_IN = _off

_DST_ORDER = ("dn_q", "dn_k", "dn_v", "dn_z", "ssd_x", "ssd_B", "ssd_C", "ml_q", "ml_k", "ml_v",
              "lru_x", "ssd_z", "lru_g", "ml_o", "dn_a", "dn_b", "ssd_dt", "ml_i", "ml_f")
N_PROJ = 6400
COL_DN_QKV = 0
COL_DN_Z = 1536
COL_SSD_XBC = 2048
COL_ML_QKV = 3072
COL_LRU_X = 4096
COL_SSD_Z = 4608
COL_LRU_G = 5120
COL_ML_O = 5632
COL_GATES = 6144
LANE_DN_A, LANE_DN_B, LANE_SSD_DT, LANE_ML_I, LANE_ML_F = 0, 8, 16, 32, 40

VMEM_LIMIT = 56 * 1024 * 1024


def _cparams(sem):
    return pltpu.CompilerParams(dimension_semantics=sem, vmem_limit_bytes=VMEM_LIMIT)


def _mm(a, b):
    return jnp.dot(a.astype(BF16), b.astype(BF16), preferred_element_type=F32)


def _dot(a, b):
    return jnp.dot(a, b, preferred_element_type=F32, precision=HI)


def _dot_nt(a, b):
    return lax.dot_general(a, b, (((1,), (1,)), ((), ())), preferred_element_type=F32, precision=HI)


def _dot_tn(a, b):
    return lax.dot_general(a, b, (((0,), (0,)), ((), ())), preferred_element_type=F32, precision=HI)


def _silu(x):
    return x * jax.nn.sigmoid(x)


def _masks(d):
    ri = lax.broadcasted_iota(jnp.int32, (CHUNK, CHUNK), 0)
    ci = lax.broadcasted_iota(jnp.int32, (CHUNK, CHUNK), 1)
    if d == 0:
        return ri >= ci, ri > ci
    return ri <= ci, ri < ci


def _seg_decay(col, row, incl):
    return jnp.where(incl, jnp.exp(jnp.where(incl, col - row, 0.0)), 0.0)


def _transpose_cols(a):
    return jnp.concatenate([a, jnp.zeros_like(a)], axis=0).T


def _ada_kernel(c_ref, w_ref, b_ref, o_ref):
    cc = c_ref[...]
    act = (cc * jax.nn.sigmoid(cc)).astype(BF16)
    o_ref[0] = jnp.dot(act, w_ref[0].astype(BF16), preferred_element_type=F32) + b_ref[0]


def _ada(cc, ada_w, ada_b):
    tn = 1024
    n = N_MOD * D_MODEL
    return pl.pallas_call(
        _ada_kernel,
        grid=(DEPTH, n // tn),
        in_specs=[
            pl.BlockSpec((8, D_MODEL), lambda l, j: (0, 0)),
            pl.BlockSpec((1, D_MODEL, tn), lambda l, j: (l, 0, j)),
            pl.BlockSpec((1, 1, tn), lambda l, j: (l, 0, j)),
        ],
        out_specs=pl.BlockSpec((1, 8, tn), lambda l, j: (l, 0, j)),
        out_shape=jax.ShapeDtypeStruct((DEPTH, 8, n), F32),
        compiler_params=_cparams(("arbitrary", "arbitrary")),
        name="ada",
    )(cc, ada_w, ada_b.reshape(DEPTH, 1, n))


def _norm_mod(xf, ng, sc, sh):
    y = xf * lax.rsqrt(jnp.mean(xf * xf, axis=-1, keepdims=True) + EPS) * ng
    return y * (1.0 + sc) + sh


def _inproj_kernel(x_ref, ng_ref, sc_ref, sh_ref, w_ref, o_ref, h_scr):
    @pl.when(pl.program_id(1) == 0)
    def _():
        h_scr[...] = _norm_mod(x_ref[...], ng_ref[...], sc_ref[...], sh_ref[...]).astype(BF16)

    o_ref[...] = jnp.dot(h_scr[...], w_ref[...], preferred_element_type=F32)


def _inproj(x, ng, sc, sh, w):
    t = x.shape[0]
    tm = min(t, 1024)
    tn = 1280
    row = lambda i, j: (0, 0)
    return pl.pallas_call(
        _inproj_kernel,
        grid=(t // tm, N_PROJ // tn),
        in_specs=[
            pl.BlockSpec((tm, D_MODEL), lambda i, j: (i, 0)),
            pl.BlockSpec((1, D_MODEL), row),
            pl.BlockSpec((1, D_MODEL), row),
            pl.BlockSpec((1, D_MODEL), row),
            pl.BlockSpec((D_MODEL, tn), lambda i, j: (0, j)),
        ],
        out_specs=pl.BlockSpec((tm, tn), lambda i, j: (i, j)),
        out_shape=jax.ShapeDtypeStruct((t, N_PROJ), F32),
        scratch_shapes=[pltpu.VMEM((tm, D_MODEL), BF16)],
        compiler_params=_cparams(("arbitrary", "arbitrary")),
        name="inproj",
    )(x, ng, sc, sh, w)


def _conv_chunk(x_scr, main_ref, prev_ref, next_ref, cw_ref, ci, nc):
    x_scr[0:8, :] = jnp.where(ci > 0, prev_ref[...], 0.0)
    x_scr[8:8 + CHUNK, :] = main_ref[...]
    x_scr[8 + CHUNK:16 + CHUNK, :] = jnp.where(ci < nc - 1, next_ref[...], 0.0)
    acc = cw_ref[0:1, :] * x_scr[pl.ds(6, CHUNK), :]
    for j in range(1, SHORT_CONV):
        acc = acc + cw_ref[j:j + 1, :] * x_scr[pl.ds(6 + j, CHUNK), :]
    return acc


def _chunk_specs(nc, width, colblk, rev):
    nb8 = nc * (CHUNK // 8)

    def cidx(i):
        return nc - 1 - i if rev else i

    return [
        pl.BlockSpec((CHUNK, width), lambda i: (cidx(i), colblk)),
        pl.BlockSpec((8, width), lambda i: (jnp.maximum(cidx(i) * 8 - 1, 0), colblk)),
        pl.BlockSpec((8, width), lambda i: (jnp.minimum(cidx(i) * 8 + 8, nb8 - 1), colblk)),
    ]


def _gate_spec(nc, rev):
    if rev:
        return pl.BlockSpec((CHUNK, 128), lambda i: (nc - 1 - i, COL_GATES // 128))
    return pl.BlockSpec((CHUNK, 128), lambda i: (i, COL_GATES // 128))


def _out_spec(nc, width, rev):
    if rev:
        return pl.BlockSpec((CHUNK, width), lambda i: (nc - 1 - i, 0))
    return pl.BlockSpec((CHUNK, width), lambda i: (i, 0))


def _const_spec(shape):
    nd = len(shape)
    return pl.BlockSpec(shape, lambda i: (0,) * nd)


def _cumsum_dir(x, d):
    incl, _ = _masks(d)
    return _dot(jnp.where(incl, 1.0, 0.0).astype(F32), x)


def _dn_kernel(nc, mf, pf, nf, mb, pb, nb, gf, gb, cw_ref, par_ref, s0_ref,
               of_ref, ob_ref, s_ref, x_scr):
    i = pl.program_id(0)

    @pl.when(i == 0)
    def _():
        s_ref[...] = s0_ref[...]

    eye = jnp.where(_masks(0)[0] & _masks(1)[0], 1.0, 0.0).astype(F32)
    for d, (m_ref, p_ref, n_ref, g_ref, o_ref) in enumerate(((mf, pf, nf, gf, of_ref),
                                                             (mb, pb, nb, gb, ob_ref))):
        ci = i if d == 0 else nc - 1 - i
        incl, strict = _masks(d)
        last = CHUNK - 1 if d == 0 else 0
        qkv = _silu(_conv_chunk(x_scr.at[d], m_ref, p_ref, n_ref, cw_ref, ci, nc))
        gates = g_ref[...]
        g_all = -jnp.exp(par_ref[0:1, :]) * jax.nn.softplus(gates + par_ref[1:2, :])
        beta_all = jax.nn.sigmoid(gates)
        gc = _cumsum_dir(g_all, d)
        gct = _transpose_cols(gc)
        for h in range(DN_HEADS):
            lane = LANE_DN_A + d * DN_HEADS + h
            blane = LANE_DN_B + d * DN_HEADS + h
            q = qkv[:, h * DN_DK:(h + 1) * DN_DK]
            k = qkv[:, 512 + h * DN_DK:512 + (h + 1) * DN_DK]
            v = qkv[:, 1024 + h * DN_DV:1024 + (h + 1) * DN_DV]
            q = q * lax.rsqrt(jnp.sum(q * q, axis=-1, keepdims=True) + EPS) * DN_DK ** -0.5
            k = k * lax.rsqrt(jnp.sum(k * k, axis=-1, keepdims=True) + EPS)
            gcc = gc[:, lane:lane + 1]
            gcr = gct[lane:lane + 1, 0:CHUNK]
            tot = gc[last:last + 1, lane:lane + 1]
            beta = beta_all[:, blane:blane + 1]
            decay = _seg_decay(gcc, gcr, incl)
            kb = k * beta
            nmat = jnp.where(strict, _dot_nt(kb, k) * decay, 0.0)
            tinv = eye - nmat
            pw = nmat
            for _ in range(5):
                pw = _dot(pw, pw)
                tinv = tinv + _dot(tinv, pw)
            rhs = jnp.concatenate([v * beta, kb * jnp.exp(gcc)], axis=-1)
            sol = _dot(tinv, rhs)
            u, w = sol[:, :DN_DV], sol[:, DN_DV:]
            attn = _dot_nt(q, k) * decay
            s = s_ref[d * DN_HEADS + h]
            v_new = u - _dot(w, s)
            o_ref[:, h * DN_DV:(h + 1) * DN_DV] = _dot(q * jnp.exp(gcc), s) + _dot(attn, v_new)
            s_ref[d * DN_HEADS + h] = s * jnp.exp(tot) + _dot_tn(k * jnp.exp(tot - gcc), v_new)


def _dn_scan(p, cw, par, s0):
    t = p.shape[0]
    nc = t // CHUNK
    wq = 3 * GROUP_W
    out = jax.ShapeDtypeStruct((t, GROUP_W), F32)
    return pl.pallas_call(
        functools.partial(_dn_kernel, nc),
        grid=(nc,),
        in_specs=(_chunk_specs(nc, wq, COL_DN_QKV // wq, False)
                  + _chunk_specs(nc, wq, COL_DN_QKV // wq, True)
                  + [_gate_spec(nc, False), _gate_spec(nc, True),
                     _const_spec((SHORT_CONV, wq)), _const_spec((8, 128)),
                     _const_spec((2 * DN_HEADS, DN_DK, DN_DV))]),
        out_specs=[_out_spec(nc, GROUP_W, False), _out_spec(nc, GROUP_W, True),
                   _const_spec((2 * DN_HEADS, DN_DK, DN_DV))],
        out_shape=[out, out, jax.ShapeDtypeStruct((2 * DN_HEADS, DN_DK, DN_DV), F32)],
        scratch_shapes=[pltpu.VMEM((2, CHUNK + 16, wq), F32)],
        compiler_params=_cparams(("arbitrary",)),
        name="dn_scan",
    )(p, p, p, p, p, p, p, p, cw, par, s0)


def _ssd_kernel(nc, mf, pf, nf, mb, pb, nb, gf, gb, cw_ref, cb_ref, par_ref, s0_ref,
                of_ref, ob_ref, xs_ref, s_ref, x_scr):
    i = pl.program_id(0)

    @pl.when(i == 0)
    def _():
        s_ref[...] = s0_ref[...]

    hp = SSD_HEADS * SSD_HEAD_DIM
    gw = hp // SSD_GROUPS
    rep = SSD_HEADS // SSD_GROUPS
    for d, (m_ref, p_ref, n_ref, g_ref, o_ref) in enumerate(((mf, pf, nf, gf, of_ref),
                                                             (mb, pb, nb, gb, ob_ref))):
        ci = i if d == 0 else nc - 1 - i
        incl, _ = _masks(d)
        last = CHUNK - 1 if d == 0 else 0
        xbc = _silu(_conv_chunk(x_scr.at[d], m_ref, p_ref, n_ref, cw_ref, ci, nc) + cb_ref[...])
        xs = xbc[:, :hp]
        if d == 0:
            xs_ref[...] = xs
        gates = g_ref[...]
        dt_all = jax.nn.softplus(gates + par_ref[0:1, :])
        a_all = -jnp.exp(par_ref[1:2, :]) * dt_all
        acs = _cumsum_dir(a_all, d)
        acst = _transpose_cols(acs)
        er = lax.broadcasted_iota(jnp.int32, (128, hp), 0)
        ec = lax.broadcasted_iota(jnp.int32, (128, hp), 1)
        expand = jnp.where(er == LANE_SSD_DT + d * SSD_HEADS + ec // SSD_HEAD_DIM, 1.0, 0.0).astype(F32)
        dt_x = _dot(dt_all, expand)
        acs_x = _dot(acs, expand)
        tot_x = acs_x[last:last + 1, :]
        xd = xs * dt_x
        xdw = xd * jnp.exp(tot_x - acs_x)
        for g in range(SSD_GROUPS):
            bm = xbc[:, hp + g * SSD_STATE:hp + (g + 1) * SSD_STATE]
            cm = xbc[:, hp + SSD_GROUPS * SSD_STATE + g * SSD_STATE:
                     hp + SSD_GROUPS * SSD_STATE + (g + 1) * SSD_STATE]
            cb = _dot_nt(cm, bm)
            s = s_ref[d * SSD_GROUPS + g]
            gs = slice(g * gw, (g + 1) * gw)
            o_ref[:, gs] = _dot(cm, s) * jnp.exp(acs_x[:, gs])
            for hh in range(rep):
                h = g * rep + hh
                lane = LANE_SSD_DT + d * SSD_HEADS + h
                hs = slice(h * SSD_HEAD_DIM, (h + 1) * SSD_HEAD_DIM)
                lmat = _seg_decay(acs[:, lane:lane + 1], acst[lane:lane + 1, 0:CHUNK], incl)
                o_ref[:, hs] += _dot(cb * lmat, xd[:, hs])
            s_ref[d * SSD_GROUPS + g] = s * jnp.exp(tot_x[:, gs]) + _dot_tn(bm, xdw[:, gs])


def _ssd_scan(p, cw, cb, par, s0):
    t = p.shape[0]
    nc = t // CHUNK
    wx = 2 * GROUP_W
    hp = SSD_HEADS * SSD_HEAD_DIM
    st = (2 * SSD_GROUPS, SSD_STATE, hp // SSD_GROUPS)
    out = jax.ShapeDtypeStruct((t, hp), F32)
    return pl.pallas_call(
        functools.partial(_ssd_kernel, nc),
        grid=(nc,),
        in_specs=(_chunk_specs(nc, wx, COL_SSD_XBC // wx, False)
                  + _chunk_specs(nc, wx, COL_SSD_XBC // wx, True)
                  + [_gate_spec(nc, False), _gate_spec(nc, True),
                     _const_spec((SHORT_CONV, wx)), _const_spec((1, wx)), _const_spec((8, 128)),
                     _const_spec(st)]),
        out_specs=[_out_spec(nc, hp, False), _out_spec(nc, hp, True), _out_spec(nc, hp, False),
                   _const_spec(st)],
        out_shape=[out, out, out, jax.ShapeDtypeStruct(st, F32)],
        scratch_shapes=[pltpu.VMEM((2, CHUNK + 16, wx), F32)],
        compiler_params=_cparams(("arbitrary",)),
        name="ssd_scan",
    )(p, p, p, p, p, p, p, p, cw, cb, par, s0)


def _ml_kernel(nc, mf, mb, gf, gb, par_ref, c0_ref, n0_ref, m0_ref,
               of_ref, ob_ref, c_ref, n_ref, m_ref):
    i = pl.program_id(0)

    @pl.when(i == 0)
    def _():
        c_ref[...] = c0_ref[...]
        n_ref[...] = n0_ref[...]
        m_ref[...] = m0_ref[...]

    for d, (x_ref, g_ref, o_ref) in enumerate(((mf, gf, of_ref), (mb, gb, ob_ref))):
        incl, _ = _masks(d)
        last = CHUNK - 1 if d == 0 else 0
        x = x_ref[...]
        gates = g_ref[...]
        ig_all = gates + par_ref[0:1, :]
        lf_all = jax.nn.log_sigmoid(gates + par_ref[1:2, :])
        b_all = _cumsum_dir(lf_all, d)
        bt = jnp.concatenate([b_all, ig_all], axis=0).T
        for h in range(ML_HEADS):
            sd = d * ML_HEADS + h
            il = LANE_ML_I + sd
            fl = LANE_ML_F + sd
            q = x[:, h * ML_DK:(h + 1) * ML_DK] * ML_DK ** -0.5
            k = x[:, 256 + h * ML_DK:256 + (h + 1) * ML_DK]
            v = x[:, 512 + h * ML_DV:512 + (h + 1) * ML_DV]
            bc = b_all[:, fl:fl + 1]
            igc = ig_all[:, il:il + 1]
            br = bt[fl:fl + 1, 0:CHUNK]
            igr = bt[il:il + 1, CHUNK:2 * CHUNK]
            b_last = b_all[last:last + 1, fl:fl + 1]
            c_s = c_ref[sd]
            n_s = n_ref[sd]
            m_s = m_ref[sd][:, 0:1]
            log_d = jnp.where(incl, bc - br + igr, -jnp.inf)
            inter = bc + m_s
            m = jnp.maximum(inter, jnp.max(log_d, axis=-1, keepdims=True))
            s = _dot_nt(q, k) * jnp.exp(log_d - m)
            w_inter = jnp.exp(inter - m)
            num = w_inter * _dot(q, c_s) + _dot(s, v)
            den = (w_inter * jnp.sum(q * n_s, axis=-1, keepdims=True)
                   + jnp.sum(s, axis=-1, keepdims=True))
            o_ref[:, h * ML_DV:(h + 1) * ML_DV] = num / jnp.maximum(jnp.abs(den), jnp.exp(-m))
            log_g = b_last - bc + igc
            m_new = jnp.maximum(b_last + m_s, jnp.max(log_g, axis=0, keepdims=True))
            dec = jnp.exp(b_last + m_s - m_new)
            wk = jnp.exp(log_g - m_new) * k
            c_ref[sd] = dec * c_s + _dot_tn(wk, v)
            n_ref[sd] = dec * n_s + jnp.sum(wk, axis=0, keepdims=True)
            m_ref[sd] = jnp.broadcast_to(m_new, (1, 128))


def _ml_scan(p, par, c0, n0, m0):
    t = p.shape[0]
    nc = t // CHUNK
    wx = 2 * GROUP_W
    cs, ns, ms = (2 * ML_HEADS, ML_DK, ML_DV), (2 * ML_HEADS, 1, ML_DK), (2 * ML_HEADS, 1, 128)
    out = jax.ShapeDtypeStruct((t, GROUP_W), F32)
    return pl.pallas_call(
        functools.partial(_ml_kernel, nc),
        grid=(nc,),
        in_specs=[pl.BlockSpec((CHUNK, wx), lambda i: (i, COL_ML_QKV // wx)),
                  pl.BlockSpec((CHUNK, wx), lambda i: (nc - 1 - i, COL_ML_QKV // wx)),
                  _gate_spec(nc, False), _gate_spec(nc, True), _const_spec((8, 128)),
                  _const_spec(cs), _const_spec(ns), _const_spec(ms)],
        out_specs=[_out_spec(nc, GROUP_W, False), _out_spec(nc, GROUP_W, True),
                   _const_spec(cs), _const_spec(ns), _const_spec(ms)],
        out_shape=[out, out, jax.ShapeDtypeStruct(cs, F32), jax.ShapeDtypeStruct(ns, F32),
                   jax.ShapeDtypeStruct(ms, F32)],
        compiler_params=_cparams(("arbitrary",)),
        name="ml_scan",
    )(p, p, p, p, par, c0, n0, m0)


def _lru_kernel(rows, wcols, x_ref, cw_ref, cb_ref, wa_ref, ba_ref, wi_ref, bi_ref, lam_ref,
                h0_ref, o_ref, hfin_ref, xp_scr, a_scr, b_scr):
    t = rows * wcols
    sub = lax.broadcasted_iota(jnp.int32, (wcols, 128), 0)

    def shift_down(a):
        return jnp.where(sub >= 1, pltpu.roll(a, 1, axis=0), 0.0)

    def shift_up(a):
        return jnp.where(sub < wcols - 1, pltpu.roll(a, wcols - 1, axis=0), 0.0)

    def slab(r):
        return pl.ds(pl.multiple_of(r * wcols, wcols), wcols)

    xp_scr[pl.ds(2 * wcols, t), :] = x_ref[...]
    xp_scr[pl.ds(0, wcols), :] = shift_down(x_ref[pl.ds((rows - 2) * wcols, wcols), :])
    xp_scr[pl.ds(wcols, wcols), :] = shift_down(x_ref[pl.ds((rows - 1) * wcols, wcols), :])
    xp_scr[pl.ds((rows + 2) * wcols, wcols), :] = shift_up(x_ref[pl.ds(0, wcols), :])
    xp_scr[pl.ds((rows + 3) * wcols, wcols), :] = shift_up(x_ref[pl.ds(wcols, wcols), :])

    rb = 256 if t % 256 == 0 else t
    for d in range(2):
        sp_lam = jax.nn.softplus(-lam_ref[d:d + 1, :])

        def gate_body(blk, carry):
            base = pl.multiple_of(blk * rb, rb)
            xr = cb_ref[...] + cw_ref[0:1, :] * xp_scr[pl.ds(base, rb), :]
            for j in range(1, SHORT_CONV):
                xr = xr + cw_ref[j:j + 1, :] * xp_scr[pl.ds(pl.multiple_of(base + j * wcols, 8), rb), :]
            xb = xr.astype(BF16)
            r = jax.nn.sigmoid(jnp.dot(xb, wa_ref[d], preferred_element_type=F32) + ba_ref[d:d + 1, :])
            ii = jax.nn.sigmoid(jnp.dot(xb, wi_ref[d], preferred_element_type=F32) + bi_ref[d:d + 1, :])
            log_a = -LRU_C * r * sp_lam
            a = jnp.exp(log_a)
            b = jnp.sqrt(-jnp.tanh(log_a) * (a * a + 1.0)) * ii * xr
            a_scr[pl.ds(base, rb), :] = a
            b_scr[pl.ds(base, rb), :] = b
            return carry

        lax.fori_loop(0, t // rb, gate_body, 0)

        def scan_body(step, carry):
            h, acc = carry
            r = step if d == 0 else rows - 1 - step
            a = a_scr[slab(r), :]
            h = a * h + b_scr[slab(r), :]
            acc = a * acc
            b_scr[slab(r), :] = h
            a_scr[slab(r), :] = acc
            return h, acc

        h_end, a_end = lax.fori_loop(
            0, rows, scan_body, (jnp.zeros((wcols, 128), F32), jnp.ones((wcols, 128), F32)))

        sh = 1
        while sh < wcols:
            if d == 0:
                valid = sub >= sh
                a_sh, h_sh = pltpu.roll(a_end, sh, axis=0), pltpu.roll(h_end, sh, axis=0)
            else:
                valid = sub < wcols - sh
                a_sh, h_sh = pltpu.roll(a_end, wcols - sh, axis=0), pltpu.roll(h_end, wcols - sh, axis=0)
            h_end = jnp.where(valid, a_end * h_sh + h_end, h_end)
            a_end = jnp.where(valid, a_end * a_sh, a_end)
            sh *= 2
        h0 = h0_ref[d:d + 1, :]
        h_full = h_end + a_end * h0
        if d == 0:
            carry_in = jnp.where(sub >= 1, pltpu.roll(h_full, 1, axis=0), h0)
            hfin_ref[0:1, :] = h_full[wcols - 1:wcols, :]
        else:
            carry_in = jnp.where(sub < wcols - 1, pltpu.roll(h_full, wcols - 1, axis=0), h0)
            hfin_ref[1:2, :] = h_full[0:1, :]

        def fix_body(r, carry):
            hv = b_scr[slab(r), :] + a_scr[slab(r), :] * carry_in
            if d == 0:
                o_ref[slab(r), :] = hv
            else:
                o_ref[slab(r), :] += hv
            return carry

        lax.fori_loop(0, rows, fix_body, 0)


def _lru_scan(x, colblk, wcols, cw, cb, wa, ba, wi, bi, lam, h0):
    t = x.shape[0]
    rows = t // wcols
    nt = LRU_W // 128
    vec = lambda r: pl.BlockSpec((r, 128), lambda j: (0, j))
    wspec = pl.BlockSpec((2, 128, 128), lambda j: (0, j, j))
    return pl.pallas_call(
        functools.partial(_lru_kernel, rows, wcols),
        grid=(nt,),
        in_specs=[pl.BlockSpec((t, 128), lambda j: (0, colblk + j)),
                  vec(SHORT_CONV), vec(1), wspec, vec(2), wspec, vec(2), vec(2), vec(2)],
        out_specs=[pl.BlockSpec((t, 128), lambda j: (0, j)), vec(2)],
        out_shape=[jax.ShapeDtypeStruct((t, LRU_W), F32), jax.ShapeDtypeStruct((2, LRU_W), F32)],
        scratch_shapes=[pltpu.VMEM((t + 4 * wcols, 128), F32), pltpu.VMEM((t, 128), F32),
                        pltpu.VMEM((t, 128), F32)],
        compiler_params=_cparams(("arbitrary",)),
        name="lru_scan",
    )(x, cw, cb, wa, ba, wi, bi, lam, h0)


def _rms_groups(x, g, width):
    parts = []
    for s in range(0, x.shape[-1], width):
        xs = x[:, s:s + width]
        parts.append(xs * lax.rsqrt(jnp.mean(xs * xs, axis=-1, keepdims=True) + EPS) * g[:, s:s + width])
    return parts


def _outproj_kernel(x_ref, g1_ref, dnf, dnb, dnz, dng, sdf, sdb, sdx, sdz, sdd, sdg,
                    lrh, lrg, mlf, mlb, mlo, mlg, w_ref, o_ref):
    acc = jnp.zeros(o_ref.shape, F32)
    z = dnz[...]
    a_parts = _rms_groups(dnf[...] + dnb[...], dng[...], DN_DV)
    for h, part in enumerate(a_parts):
        zz = z[:, h * DN_DV:(h + 1) * DN_DV]
        acc += _mm(part * _silu(zz), w_ref[h * DN_DV:(h + 1) * DN_DV, :])
    y = (sdf[...] + sdb[...] + sdd[...] * sdx[...]) * _silu(sdz[...])
    gw = GROUP_W // SSD_GROUPS
    for g, part in enumerate(_rms_groups(y, sdg[...], gw)):
        acc += _mm(part, w_ref[GROUP_W + g * gw:GROUP_W + (g + 1) * gw, :])
    acc += _mm(lrh[...] * jax.nn.gelu(lrg[...]), w_ref[2 * GROUP_W:3 * GROUP_W, :])
    o = mlo[...]
    for h, part in enumerate(_rms_groups(mlf[...] + mlb[...], mlg[...], ML_DV)):
        oo = o[:, h * ML_DV:(h + 1) * ML_DV]
        acc += _mm(part * jax.nn.sigmoid(oo), w_ref[3 * GROUP_W + h * ML_DV:3 * GROUP_W + (h + 1) * ML_DV, :])
    o_ref[...] = x_ref[...] + g1_ref[...] * acc


def _outproj(x, g1, p, dn_f, dn_b, dn_g, sd_f, sd_b, sd_x, sd_d, sd_g, lr_h, ml_f, ml_b, ml_g, w):
    t = x.shape[0]
    tm = min(t, 256)
    tok = pl.BlockSpec((tm, GROUP_W), lambda i: (i, 0))
    pcol = lambda col: pl.BlockSpec((tm, GROUP_W), lambda i: (i, col // GROUP_W))
    vec = lambda n: pl.BlockSpec((1, n), lambda i: (0, 0))
    return pl.pallas_call(
        _outproj_kernel,
        grid=(t // tm,),
        in_specs=[pl.BlockSpec((tm, D_MODEL), lambda i: (i, 0)), vec(D_MODEL),
                  tok, tok, pcol(COL_DN_Z), vec(GROUP_W),
                  tok, tok, tok, pcol(COL_SSD_Z), vec(GROUP_W), vec(GROUP_W),
                  tok, pcol(COL_LRU_G),
                  tok, tok, pcol(COL_ML_O), vec(GROUP_W),
                  pl.BlockSpec((D_MODEL, D_MODEL), lambda i: (0, 0))],
        out_specs=pl.BlockSpec((tm, D_MODEL), lambda i: (i, 0)),
        out_shape=jax.ShapeDtypeStruct((t, D_MODEL), F32),
        compiler_params=_cparams(("arbitrary",)),
        name="outproj",
    )(x, g1, dn_f, dn_b, p, dn_g, sd_f, sd_b, sd_x, p, sd_d, sd_g, lr_h, p, ml_f, ml_b, p, ml_g, w)


FFN_HALO = 16


def _ffn_kernel(nt, nk, final, x_ref, xp_ref, xn_ref, ng_ref, sc_ref, sh_ref, g2_ref, fg_ref,
                wu_ref, wg_ref, cw_ref, wd_ref, o_ref, h_scr, g_scr, acc_scr):
    i = pl.program_id(0)
    k = pl.program_id(1)
    tm = x_ref.shape[0]

    @pl.when(k == 0)
    def _():
        ng, sc, sh = ng_ref[...], sc_ref[...], sh_ref[...]
        hp = jnp.where(i > 0, _norm_mod(xp_ref[...], ng, sc, sh), 0.0)
        hn = jnp.where(i < nt - 1, _norm_mod(xn_ref[...], ng, sc, sh), 0.0)
        h_scr[0:FFN_HALO, :] = hp.astype(BF16)
        h_scr[FFN_HALO:FFN_HALO + tm, :] = _norm_mod(x_ref[...], ng, sc, sh).astype(BF16)
        h_scr[FFN_HALO + tm:2 * FFN_HALO + tm, :] = hn.astype(BF16)
        acc_scr[...] = jnp.zeros_like(acc_scr)

    u = jnp.dot(h_scr[FFN_HALO:FFN_HALO + tm, :], wu_ref[...], preferred_element_type=F32)
    g_scr[...] = jnp.dot(h_scr[...], wg_ref[...], preferred_element_type=F32)
    conv = (cw_ref[0:1, :] * g_scr[pl.ds(FFN_HALO - 1, tm), :]
            + cw_ref[1:2, :] * g_scr[pl.ds(FFN_HALO, tm), :]
            + cw_ref[2:3, :] * g_scr[pl.ds(FFN_HALO + 1, tm), :])
    acc_scr[...] += jnp.dot((_silu(conv) * u).astype(BF16), wd_ref[...], preferred_element_type=F32)

    @pl.when(k == nk - 1)
    def _():
        y = x_ref[...] + g2_ref[...] * acc_scr[...]
        if final:
            y = y * lax.rsqrt(jnp.mean(y * y, axis=-1, keepdims=True) + EPS) * fg_ref[...]
        o_ref[...] = y


def _ffn(x, ng, sc, sh, g2, fg, w_up, cw, w_down, final):
    t = x.shape[0]
    tm = min(t, 512)
    bk = 512
    nt, nk = t // tm, D_FF // bk
    nhb = t // FFN_HALO
    per = tm // FFN_HALO
    vec = pl.BlockSpec((1, D_MODEL), lambda i, k: (0, 0))
    return pl.pallas_call(
        functools.partial(_ffn_kernel, nt, nk, final),
        grid=(nt, nk),
        in_specs=[pl.BlockSpec((tm, D_MODEL), lambda i, k: (i, 0)),
                  pl.BlockSpec((FFN_HALO, D_MODEL), lambda i, k: (jnp.maximum(i * per - 1, 0), 0)),
                  pl.BlockSpec((FFN_HALO, D_MODEL), lambda i, k: (jnp.minimum((i + 1) * per, nhb - 1), 0)),
                  vec, vec, vec, vec, vec,
                  pl.BlockSpec((D_MODEL, bk), lambda i, k: (0, k)),
                  pl.BlockSpec((D_MODEL, bk), lambda i, k: (0, nk + k)),
                  pl.BlockSpec((FFN_CONV, bk), lambda i, k: (0, k)),
                  pl.BlockSpec((bk, D_MODEL), lambda i, k: (k, 0))],
        out_specs=pl.BlockSpec((tm, D_MODEL), lambda i, k: (i, 0)),
        out_shape=jax.ShapeDtypeStruct((t, D_MODEL), F32),
        scratch_shapes=[pltpu.VMEM((tm + 2 * FFN_HALO, D_MODEL), BF16),
                        pltpu.VMEM((tm + 2 * FFN_HALO, bk), F32),
                        pltpu.VMEM((tm, D_MODEL), F32)],
        compiler_params=_cparams(("arbitrary", "arbitrary")),
        name="ffn",
    )(x, x, x, ng, sc, sh, g2, fg, w_up, w_up, cw, w_down)


def _perm_w_in(w):
    cols = [w[:, _SRC[n][0]:_SRC[n][0] + _SRC[n][1]] for n in _DST_ORDER]
    used = sum(_SRC[n][1] for n in _DST_ORDER)
    cols.append(jnp.zeros((w.shape[0], N_PROJ - used), w.dtype))
    return jnp.concatenate(cols, axis=1).astype(BF16)


def _lane_row(pairs):
    tile = jnp.zeros((8, 128), F32)
    for row, lane, vals in pairs:
        vals = vals.reshape(-1).astype(F32)
        tile = tile.at[row, lane:lane + vals.shape[0]].set(vals)
    return tile


def _block_diag(w):
    nb, bw = w.shape[1], w.shape[2]
    eye = jnp.eye(nb, dtype=w.dtype)
    full = jnp.einsum("dnjk,nm->dnjmk", w, eye)
    return full.reshape(2, nb * bw, nb * bw).astype(BF16)


def _ctx_to_cols(a, wcols):
    t, ch = a.shape
    return a.reshape(wcols, t // wcols, ch).transpose(1, 0, 2).reshape(t, ch)


def _ctx_from_cols(a, wcols):
    t, ch = a.shape
    return a.reshape(t // wcols, wcols, ch).transpose(1, 0, 2).reshape(t, ch)


CTX_COLS = 8


def kernel(x, c, ctx, c_ctx, ada_w, ada_b, norm1_g, norm2_g, w_in, dn_conv_w, dn_a_log, dn_dt_bias, dn_norm_g, ssd_conv_w, ssd_conv_b, ssd_a_log, ssd_dt_bias, ssd_d, ssd_norm_g, lru_conv_w, lru_conv_b, lru_w_a, lru_b_a, lru_w_i, lru_b_i, lru_lambda, ml_ig_b, ml_fg_b, ml_norm_g, w_out, ffn_w_up, ffn_conv_w, ffn_w_down, final_norm_g):
    assert x.shape[0] == 1 and c.shape[0] == 1
    lat, hctx = x[0], ctx[0]
    cc = jnp.zeros((8, D_MODEL), F32).at[0].set(c[0]).at[1].set(c_ctx)
    mods = _ada(cc, ada_w, ada_b)
    row = lambda v: v.reshape(1, -1).astype(F32)

    for l in range(DEPTH):
        mod_l = mods[l, 0].reshape(N_MOD, 1, D_MODEL)
        mod_c = mods[l, 1].reshape(N_MOD, 1, D_MODEL)
        w_in_p = _perm_w_in(w_in[l])
        w_out_b = w_out[l].astype(BF16)
        w_up_b = ffn_w_up[l].astype(BF16)
        w_down_b = ffn_w_down[l].astype(BF16)
        ng1, ng2 = row(norm1_g[l]), row(norm2_g[l])

        dn_par = _lane_row([(0, LANE_DN_A, dn_a_log[l]), (1, LANE_DN_A, dn_dt_bias[l])])
        ssd_par = _lane_row([(0, LANE_SSD_DT, ssd_dt_bias[l]), (1, LANE_SSD_DT, ssd_a_log[l])])
        ml_par = _lane_row([(0, LANE_ML_I, ml_ig_b[l]), (1, LANE_ML_F, ml_fg_b[l])])
        lru_args = (lru_conv_w[l], row(lru_conv_b[l]), _block_diag(lru_w_a[l]), lru_b_a[l],
                    _block_diag(lru_w_i[l]), lru_b_i[l], lru_lambda[l])
        dn_g = row(jnp.tile(dn_norm_g[l], DN_HEADS))
        sd_d = row(jnp.repeat(ssd_d[l], SSD_HEAD_DIM))
        sd_g, ml_g = row(ssd_norm_g[l]), row(ml_norm_g[l])

        pc = _inproj(hctx, ng1, mod_c[1], mod_c[0], w_in_p)
        dn_cf, dn_cb, dn_s = _dn_scan(pc, dn_conv_w[l], dn_par,
                                      jnp.zeros((2 * DN_HEADS, DN_DK, DN_DV), F32))
        sd_cf, sd_cb, sd_cx, sd_s = _ssd_scan(
            pc, ssd_conv_w[l], row(ssd_conv_b[l]), ssd_par,
            jnp.zeros((2 * SSD_GROUPS, SSD_STATE, GROUP_W // SSD_GROUPS), F32))
        ml_cf, ml_cb, ml_c, ml_n, ml_m = _ml_scan(
            pc, ml_par, jnp.zeros((2 * ML_HEADS, ML_DK, ML_DV), F32),
            jnp.zeros((2 * ML_HEADS, 1, ML_DK), F32), jnp.zeros((2 * ML_HEADS, 1, 128), F32))
        xc_cols = _ctx_to_cols(pc[:, COL_LRU_X:COL_LRU_X + LRU_W], CTX_COLS)
        lr_c, lr_s = _lru_scan(xc_cols, 0, CTX_COLS, *lru_args, jnp.zeros((2, LRU_W), F32))

        pl_ = _inproj(lat, ng1, mod_l[1], mod_l[0], w_in_p)
        dn_lf, dn_lb, _ = _dn_scan(pl_, dn_conv_w[l], dn_par, dn_s)
        sd_lf, sd_lb, sd_lx, _ = _ssd_scan(pl_, ssd_conv_w[l], row(ssd_conv_b[l]), ssd_par, sd_s)
        ml_lf, ml_lb, _, _, _ = _ml_scan(pl_, ml_par, ml_c, ml_n, ml_m)
        lr_l, _ = _lru_scan(pl_, COL_LRU_X // 128, GRID_W, *lru_args, lr_s)

        lat = _outproj(lat, mod_l[2], pl_, dn_lf, dn_lb, dn_g, sd_lf, sd_lb, sd_lx, sd_d, sd_g,
                       lr_l, ml_lf, ml_lb, ml_g, w_out_b)
        last = l == DEPTH - 1
        lat = _ffn(lat, ng2, mod_l[4], mod_l[3], mod_l[5], row(final_norm_g), w_up_b,
                   ffn_conv_w[l], w_down_b, last)

        if not last:
            lr_cn = _ctx_from_cols(lr_c, CTX_COLS)
            hctx = _outproj(hctx, mod_c[2], pc, dn_cf, dn_cb, dn_g, sd_cf, sd_cb, sd_cx, sd_d, sd_g,
                            lr_cn, ml_cf, ml_cb, ml_g, w_out_b)
            hctx = _ffn(hctx, ng2, mod_c[4], mod_c[3], mod_c[5], row(final_norm_g), w_up_b,
                        ffn_conv_w[l], w_down_b, False)

    return lat[None]
```

```python
import functools

import jax
import jax.numpy as jnp
from jax import lax
from jax.experimental import pallas as pl
from jax.experimental.pallas import tpu as pltpu

F32 = jnp.float32
BF16 = jnp.bfloat16
HI = lax.Precision.HIGHEST

D_MODEL = 2048
DEPTH = 2
GRID_W = 64
GROUP_W = D_MODEL // 4
CHUNK = 64
SHORT_CONV = 5
FFN_CONV = 3
D_FF = ((8 * D_MODEL // 3 + 255) // 256) * 256
N_MOD = 6
EPS = 1e-6

DN_HEADS = 4
DN_DK = GROUP_W // DN_HEADS
DN_DV = GROUP_W // DN_HEADS
SSD_HEAD_DIM = 64
SSD_HEADS = GROUP_W // SSD_HEAD_DIM
SSD_GROUPS = 2
SSD_STATE = 128
LRU_W = GROUP_W
LRU_BLOCKS = 8
LRU_BW = LRU_W // LRU_BLOCKS
LRU_C = 8.0
ML_HEADS = 4
ML_DV = GROUP_W // ML_HEADS
ML_DK = ML_DV // 2

_SRC = {}
_off = 0
for _name, _w in (
    ("dn_q", 512), ("dn_k", 512), ("dn_v", 512), ("dn_z", 512), ("dn_a", 8), ("dn_b", 8),
    ("ssd_x", 512), ("ssd_z", 512), ("ssd_B", 256), ("ssd_C", 256), ("ssd_dt", 16),
    ("lru_x", 512), ("lru_g", 512),
    ("ml_q", 256), ("ml_k", 256), ("ml_v", 512), ("ml_o", 512), ("ml_i", 8), ("ml_f", 8),
):
    _SRC[_name] = (_off, _w)
    _off += _w
D_IN = _off

_DST_ORDER = ("dn_q", "dn_k", "dn_v", "dn_z", "ssd_x", "ssd_B", "ssd_C", "ml_q", "ml_k", "ml_v",
              "lru_x", "ssd_z", "lru_g", "ml_o", "dn_a", "dn_b", "ssd_dt", "ml_i", "ml_f")
N_PROJ = 6400
COL_DN_QKV = 0
COL_DN_Z = 1536
COL_SSD_XBC = 2048
COL_ML_QKV = 3072
COL_LRU_X = 4096
COL_SSD_Z = 4608
COL_LRU_G = 5120
COL_ML_O = 5632
COL_GATES = 6144
LANE_DN_A, LANE_DN_B, LANE_SSD_DT, LANE_ML_I, LANE_ML_F = 0, 8, 16, 32, 40

VMEM_LIMIT = 56 * 1024 * 1024


def _cparams(sem):
    return pltpu.CompilerParams(dimension_semantics=sem, vmem_limit_bytes=VMEM_LIMIT)


_NN = (((1,), (0,)), ((), ()))
_NT = (((1,), (1,)), ((), ()))
_TN = (((0,), (0,)), ((), ()))


def _dg(a, b, dims):
    return lax.dot_general(a, b, dims, preferred_element_type=F32)


def _mm(a, b, dims=_NN):
    return _dg(a.astype(BF16), b.astype(BF16), dims)


def _split2(a):
    hi = a.astype(BF16)
    return hi, (a - hi.astype(F32)).astype(BF16)


def _split3(a):
    hi = a.astype(BF16)
    r = a - hi.astype(F32)
    mid = r.astype(BF16)
    return hi, mid, (r - mid.astype(F32)).astype(BF16)


def _mm3(a, b, dims=_NN):
    ah, al = _split2(a)
    bh, bl = _split2(b)
    return _dg(ah, bh, dims) + (_dg(ah, bl, dims) + _dg(al, bh, dims))


def _mm_sel(sel, x):
    sb = sel.astype(BF16)
    x0, x1, x2 = _split3(x)
    return _dg(sb, x0, _NN) + (_dg(sb, x1, _NN) + _dg(sb, x2, _NN))


def _mm_spread(x, sel):
    sb = sel.astype(BF16)
    x0, x1, x2 = _split3(x)
    return _dg(x0, sb, _NN) + (_dg(x1, sb, _NN) + _dg(x2, sb, _NN))


def _silu(x):
    return x * jax.nn.sigmoid(x)


def _masks(d):
    ri = lax.broadcasted_iota(jnp.int32, (CHUNK, CHUNK), 0)
    ci = lax.broadcasted_iota(jnp.int32, (CHUNK, CHUNK), 1)
    if d == 0:
        return ri >= ci, ri > ci
    return ri <= ci, ri < ci


def _seg_decay(col, row, incl):
    return jnp.where(incl, jnp.exp(jnp.where(incl, col - row, 0.0)), 0.0)


def _transpose_cols(a):
    return jnp.concatenate([a, jnp.zeros_like(a)], axis=0).T


def _ada_kernel(c_ref, w_ref, b_ref, o_ref):
    cc = c_ref[...]
    act = (cc * jax.nn.sigmoid(cc)).astype(BF16)
    o_ref[0] = jnp.dot(act, w_ref[0].astype(BF16), preferred_element_type=F32) + b_ref[0]


def _ada(cc, ada_w, ada_b):
    tn = 1024
    n = N_MOD * D_MODEL
    return pl.pallas_call(
        _ada_kernel,
        grid=(DEPTH, n // tn),
        in_specs=[
            pl.BlockSpec((8, D_MODEL), lambda l, j: (0, 0)),
            pl.BlockSpec((1, D_MODEL, tn), lambda l, j: (l, 0, j)),
            pl.BlockSpec((1, 1, tn), lambda l, j: (l, 0, j)),
        ],
        out_specs=pl.BlockSpec((1, 8, tn), lambda l, j: (l, 0, j)),
        out_shape=jax.ShapeDtypeStruct((DEPTH, 8, n), F32),
        compiler_params=_cparams(("arbitrary", "arbitrary")),
        name="ada",
    )(cc, ada_w, ada_b.reshape(DEPTH, 1, n))


def _norm_mod(xf, ng, sc, sh):
    y = xf * lax.rsqrt(jnp.mean(xf * xf, axis=-1, keepdims=True) + EPS) * ng
    return y * (1.0 + sc) + sh


def _inproj_kernel(x_ref, ng_ref, sc_ref, sh_ref, w_ref, o_ref, h_scr):
    @pl.when(pl.program_id(1) == 0)
    def _():
        h_scr[...] = _norm_mod(x_ref[...], ng_ref[...], sc_ref[...], sh_ref[...]).astype(BF16)

    o_ref[...] = jnp.dot(h_scr[...], w_ref[...], preferred_element_type=F32)


def _inproj(x, ng, sc, sh, w):
    t = x.shape[0]
    tm = min(t, 1024)
    tn = 1280
    row = lambda i, j: (0, 0)
    return pl.pallas_call(
        _inproj_kernel,
        grid=(t // tm, N_PROJ // tn),
        in_specs=[
            pl.BlockSpec((tm, D_MODEL), lambda i, j: (i, 0)),
            pl.BlockSpec((1, D_MODEL), row),
            pl.BlockSpec((1, D_MODEL), row),
            pl.BlockSpec((1, D_MODEL), row),
            pl.BlockSpec((D_MODEL, tn), lambda i, j: (0, j)),
        ],
        out_specs=pl.BlockSpec((tm, tn), lambda i, j: (i, j)),
        out_shape=jax.ShapeDtypeStruct((t, N_PROJ), F32),
        scratch_shapes=[pltpu.VMEM((tm, D_MODEL), BF16)],
        compiler_params=_cparams(("arbitrary", "arbitrary")),
        name="inproj",
    )(x, ng, sc, sh, w)


def _conv_chunk(x_scr, main_ref, prev_ref, next_ref, cw_ref, ci, nc):
    x_scr[0:8, :] = jnp.where(ci > 0, prev_ref[...], 0.0)
    x_scr[8:8 + CHUNK, :] = main_ref[...]
    x_scr[8 + CHUNK:16 + CHUNK, :] = jnp.where(ci < nc - 1, next_ref[...], 0.0)
    acc = cw_ref[0:1, :] * x_scr[pl.ds(6, CHUNK), :]
    for j in range(1, SHORT_CONV):
        acc = acc + cw_ref[j:j + 1, :] * x_scr[pl.ds(6 + j, CHUNK), :]
    return acc


def _chunk_specs(nc, width, colblk, rev):
    nb8 = nc * (CHUNK // 8)

    def cidx(i):
        return nc - 1 - i if rev else i

    return [
        pl.BlockSpec((CHUNK, width), lambda i: (cidx(i), colblk)),
        pl.BlockSpec((8, width), lambda i: (jnp.maximum(cidx(i) * 8 - 1, 0), colblk)),
        pl.BlockSpec((8, width), lambda i: (jnp.minimum(cidx(i) * 8 + 8, nb8 - 1), colblk)),
    ]


def _gate_spec(nc, rev):
    if rev:
        return pl.BlockSpec((CHUNK, 128), lambda i: (nc - 1 - i, COL_GATES // 128))
    return pl.BlockSpec((CHUNK, 128), lambda i: (i, COL_GATES // 128))


def _out_spec(nc, width, rev):
    if rev:
        return pl.BlockSpec((CHUNK, width), lambda i: (nc - 1 - i, 0))
    return pl.BlockSpec((CHUNK, width), lambda i: (i, 0))


def _const_spec(shape):
    nd = len(shape)
    return pl.BlockSpec(shape, lambda i: (0,) * nd)


def _cumsum_dir(x, d):
    incl, _ = _masks(d)
    return _mm_sel(jnp.where(incl, 1.0, 0.0), x)


def _dn_kernel(nc, mf, pf, nf, mb, pb, nb, gf, gb, cw_ref, par_ref, s0_ref,
               of_ref, ob_ref, s_ref, x_scr):
    i = pl.program_id(0)

    @pl.when(i == 0)
    def _():
        s_ref[...] = s0_ref[...]

    eye = jnp.where(_masks(0)[0] & _masks(1)[0], 1.0, 0.0).astype(F32)
    hd = []
    for d, (m_ref, p_ref, n_ref, g_ref, o_ref) in enumerate(((mf, pf, nf, gf, of_ref),
                                                             (mb, pb, nb, gb, ob_ref))):
        ci = i if d == 0 else nc - 1 - i
        incl, strict = _masks(d)
        last = CHUNK - 1 if d == 0 else 0
        qkv = _silu(_conv_chunk(x_scr.at[d], m_ref, p_ref, n_ref, cw_ref, ci, nc))
        gates = g_ref[...]
        g_all = -jnp.exp(par_ref[0:1, :]) * jax.nn.softplus(gates + par_ref[1:2, :])
        beta_all = jax.nn.sigmoid(gates)
        gc = _cumsum_dir(g_all, d)
        gct = _transpose_cols(gc)
        for h in range(DN_HEADS):
            lane = LANE_DN_A + d * DN_HEADS + h
            blane = LANE_DN_B + d * DN_HEADS + h
            q = qkv[:, h * DN_DK:(h + 1) * DN_DK]
            k = qkv[:, 512 + h * DN_DK:512 + (h + 1) * DN_DK]
            v = qkv[:, 1024 + h * DN_DV:1024 + (h + 1) * DN_DV]
            q = q * lax.rsqrt(jnp.sum(q * q, axis=-1, keepdims=True) + EPS) * DN_DK ** -0.5
            k = k * lax.rsqrt(jnp.sum(k * k, axis=-1, keepdims=True) + EPS)
            gcc = gc[:, lane:lane + 1]
            tot = gc[last:last + 1, lane:lane + 1]
            beta = beta_all[:, blane:blane + 1]
            hd.append(dict(
                q=q, k=k, kb=k * beta, vb=v * beta, gcc=gcc, tot=tot, strict=strict,
                decay=_seg_decay(gcc, gct[lane:lane + 1, 0:CHUNK], incl),
                o_ref=o_ref, osl=slice(h * DN_DV, (h + 1) * DN_DV), sidx=d * DN_HEADS + h))

    nmat = [jnp.where(x["strict"], _mm(x["kb"], x["k"], _NT) * x["decay"], 0.0) for x in hd]
    attn = [_mm(x["q"], x["k"], _NT) * x["decay"] for x in hd]
    ri = lax.broadcasted_iota(jnp.int32, (CHUNK, CHUNK), 0)
    ci = lax.broadcasted_iota(jnp.int32, (CHUNK, CHUNK), 1)
    base = 8
    pw = [jnp.where(ri // base == ci // base, n, 0.0) for n in nmat]
    tinv = [eye - p for p in pw]
    for _ in range(2):
        pw = [_mm3(p, p) for p in pw]
        tinv = [t + _mm3(t, p) for t, p in zip(tinv, pw)]
    size = base
    while size < CHUNK:
        sib = (ri // (2 * size) == ci // (2 * size)) & (ri // size != ci // size)
        tc = [_mm3(t, jnp.where(sib, n, 0.0)) for t, n in zip(tinv, nmat)]
        tinv = [t - _mm3(c, t) for t, c in zip(tinv, tc)]
        size *= 2
    sol = [_mm3(t, jnp.concatenate([x["vb"], x["kb"] * jnp.exp(x["gcc"])], axis=-1))
           for t, x in zip(tinv, hd)]
    s_old = [s_ref[x["sidx"]] for x in hd]
    v_new = [so[:, :DN_DV] - _mm(so[:, DN_DV:], s) for so, s in zip(sol, s_old)]
    for x, s, a, vn in zip(hd, s_old, attn, v_new):
        x["o_ref"][:, x["osl"]] = _mm(x["q"] * jnp.exp(x["gcc"]), s) + _mm(a, vn)
    for x, s, vn in zip(hd, s_old, v_new):
        s_ref[x["sidx"]] = (s * jnp.exp(x["tot"])
                            + _mm(x["k"] * jnp.exp(x["tot"] - x["gcc"]), vn, _TN))


def _dn_scan(p, cw, par, s0):
    t = p.shape[0]
    nc = t // CHUNK
    wq = 3 * GROUP_W
    out = jax.ShapeDtypeStruct((t, GROUP_W), F32)
    return pl.pallas_call(
        functools.partial(_dn_kernel, nc),
        grid=(nc,),
        in_specs=(_chunk_specs(nc, wq, COL_DN_QKV // wq, False)
                  + _chunk_specs(nc, wq, COL_DN_QKV // wq, True)
                  + [_gate_spec(nc, False), _gate_spec(nc, True),
                     _const_spec((SHORT_CONV, wq)), _const_spec((8, 128)),
                     _const_spec((2 * DN_HEADS, DN_DK, DN_DV))]),
        out_specs=[_out_spec(nc, GROUP_W, False), _out_spec(nc, GROUP_W, True),
                   _const_spec((2 * DN_HEADS, DN_DK, DN_DV))],
        out_shape=[out, out, jax.ShapeDtypeStruct((2 * DN_HEADS, DN_DK, DN_DV), F32)],
        scratch_shapes=[pltpu.VMEM((2, CHUNK + 16, wq), F32)],
        compiler_params=_cparams(("arbitrary",)),
        name="dn_scan",
    )(p, p, p, p, p, p, p, p, cw, par, s0)


def _ssd_kernel(nc, mf, pf, nf, mb, pb, nb, gf, gb, cw_ref, cb_ref, par_ref, s0_ref,
                of_ref, ob_ref, xs_ref, s_ref, x_scr):
    i = pl.program_id(0)

    @pl.when(i == 0)
    def _():
        s_ref[...] = s0_ref[...]

    hp = SSD_HEADS * SSD_HEAD_DIM
    gw = hp // SSD_GROUPS
    rep = SSD_HEADS // SSD_GROUPS
    s_old = [s_ref[j] for j in range(2 * SSD_GROUPS)]
    stores = []
    for d, (m_ref, p_ref, n_ref, g_ref, o_ref) in enumerate(((mf, pf, nf, gf, of_ref),
                                                             (mb, pb, nb, gb, ob_ref))):
        ci = i if d == 0 else nc - 1 - i
        incl, _ = _masks(d)
        last = CHUNK - 1 if d == 0 else 0
        xbc = _silu(_conv_chunk(x_scr.at[d], m_ref, p_ref, n_ref, cw_ref, ci, nc) + cb_ref[...])
        xs = xbc[:, :hp]
        if d == 0:
            xs_ref[...] = xs
        gates = g_ref[...]
        dt_all = jax.nn.softplus(gates + par_ref[0:1, :])
        a_all = -jnp.exp(par_ref[1:2, :]) * dt_all
        acs = _cumsum_dir(a_all, d)
        acst = _transpose_cols(acs)
        er = lax.broadcasted_iota(jnp.int32, (128, hp), 0)
        ec = lax.broadcasted_iota(jnp.int32, (128, hp), 1)
        expand = jnp.where(er == LANE_SSD_DT + d * SSD_HEADS + ec // SSD_HEAD_DIM, 1.0, 0.0).astype(F32)
        both_x = _mm_spread(jnp.concatenate([dt_all, acs], axis=0), expand)
        dt_x, acs_x = both_x[:CHUNK], both_x[CHUNK:]
        tot_x = acs_x[last:last + 1, :]
        xd = xs * dt_x
        xdw = xd * jnp.exp(tot_x - acs_x)
        for g in range(SSD_GROUPS):
            bm = xbc[:, hp + g * SSD_STATE:hp + (g + 1) * SSD_STATE]
            cm = xbc[:, hp + SSD_GROUPS * SSD_STATE + g * SSD_STATE:
                     hp + SSD_GROUPS * SSD_STATE + (g + 1) * SSD_STATE]
            cb = _mm(cm, bm, _NT)
            s = s_old[d * SSD_GROUPS + g]
            gs = slice(g * gw, (g + 1) * gw)
            y_diag = []
            for hh in range(rep):
                h = g * rep + hh
                lane = LANE_SSD_DT + d * SSD_HEADS + h
                hs = slice(h * SSD_HEAD_DIM, (h + 1) * SSD_HEAD_DIM)
                lmat = _seg_decay(acs[:, lane:lane + 1], acst[lane:lane + 1, 0:CHUNK], incl)
                y_diag.append(_mm(cb * lmat, xd[:, hs]))
            stores.append((o_ref, gs, _mm(cm, s) * jnp.exp(acs_x[:, gs]) + jnp.concatenate(y_diag, axis=-1),
                           d * SSD_GROUPS + g, s * jnp.exp(tot_x[:, gs]) + _mm(bm, xdw[:, gs], _TN)))
    for o_ref, gs, y, sidx, s_new in stores:
        o_ref[:, gs] = y
        s_ref[sidx] = s_new


def _ssd_scan(p, cw, cb, par, s0):
    t = p.shape[0]
    nc = t // CHUNK
    wx = 2 * GROUP_W
    hp = SSD_HEADS * SSD_HEAD_DIM
    st = (2 * SSD_GROUPS, SSD_STATE, hp // SSD_GROUPS)
    out = jax.ShapeDtypeStruct((t, hp), F32)
    return pl.pallas_call(
        functools.partial(_ssd_kernel, nc),
        grid=(nc,),
        in_specs=(_chunk_specs(nc, wx, COL_SSD_XBC // wx, False)
                  + _chunk_specs(nc, wx, COL_SSD_XBC // wx, True)
                  + [_gate_spec(nc, False), _gate_spec(nc, True),
                     _const_spec((SHORT_CONV, wx)), _const_spec((1, wx)), _const_spec((8, 128)),
                     _const_spec(st)]),
        out_specs=[_out_spec(nc, hp, False), _out_spec(nc, hp, True), _out_spec(nc, hp, False),
                   _const_spec(st)],
        out_shape=[out, out, out, jax.ShapeDtypeStruct(st, F32)],
        scratch_shapes=[pltpu.VMEM((2, CHUNK + 16, wx), F32)],
        compiler_params=_cparams(("arbitrary",)),
        name="ssd_scan",
    )(p, p, p, p, p, p, p, p, cw, cb, par, s0)


def _cummax_dir(x, d):
    sub = lax.broadcasted_iota(jnp.int32, x.shape, 0)
    sh = 1
    while sh < CHUNK:
        if d == 0:
            x = jnp.where(sub >= sh, jnp.maximum(x, pltpu.roll(x, sh, axis=0)), x)
        else:
            x = jnp.where(sub < CHUNK - sh, jnp.maximum(x, pltpu.roll(x, CHUNK - sh, axis=0)), x)
        sh *= 2
    return x


def _ml_kernel(nc, mf, mb, gf, gb, par_ref, c0_ref, m0_ref, of_ref, ob_ref, c_ref, m_ref):
    i = pl.program_id(0)

    @pl.when(i == 0)
    def _():
        c_ref[...] = c0_ref[...]
        m_ref[...] = m0_ref[...]

    lane = lax.broadcasted_iota(jnp.int32, (1, 128), 1)
    m_row = m_ref[...]
    ones = jnp.ones((CHUNK, ML_DV), F32)
    hd, m_rows = [], []
    for d, (x_ref, g_ref, o_ref) in enumerate(((mf, gf, of_ref), (mb, gb, ob_ref))):
        incl, _ = _masks(d)
        last = CHUNK - 1 if d == 0 else 0
        lane0 = LANE_ML_F + d * ML_HEADS
        valid = (lane >= lane0) & (lane < lane0 + ML_HEADS)
        x = x_ref[...]
        gates = g_ref[...]
        ig_all = pltpu.roll(gates + par_ref[0:1, :], LANE_ML_F - LANE_ML_I, axis=1)
        b_all = _cumsum_dir(jax.nn.log_sigmoid(gates + par_ref[1:2, :]), d)
        r_all = ig_all - b_all
        m_all = b_all + jnp.maximum(m_row, _cummax_dir(r_all, d))
        b_last = b_all[last:last + 1, :]
        log_g = b_last - b_all + ig_all
        m_new = jnp.maximum(b_last + m_row, jnp.max(log_g, axis=0, keepdims=True))
        m_rows.append(jnp.where(valid, m_new, 0.0))
        dec_row = jnp.exp(b_last + m_row - m_new)
        packed = jnp.concatenate(
            [b_all - m_all, jnp.exp(b_all + m_row - m_all), jnp.exp(-m_all),
             jnp.broadcast_to(dec_row, (8, 128))], axis=0)
        sr = lax.broadcasted_iota(jnp.int32, (128, ML_HEADS * ML_DV), 0)
        sc = lax.broadcasted_iota(jnp.int32, (128, ML_HEADS * ML_DV), 1)
        wide = _mm_spread(jnp.where(valid, packed, 0.0), jnp.where(sr == lane0 + sc // ML_DV, 1.0, 0.0))
        sr = lax.broadcasted_iota(jnp.int32, (128, ML_HEADS * ML_DK), 0)
        sc = lax.broadcasted_iota(jnp.int32, (128, ML_HEADS * ML_DK), 1)
        wkc = _mm_spread(jnp.where(valid, jnp.exp(log_g - m_new), 0.0),
                         jnp.where(sr == lane0 + sc // ML_DK, 1.0, 0.0))
        rt = _transpose_cols(r_all)
        for h in range(ML_HEADS):
            sd = d * ML_HEADS + h
            ws = slice(h * ML_DV, (h + 1) * ML_DV)
            hd.append(dict(
                q=x[:, h * ML_DK:(h + 1) * ML_DK] * ML_DK ** -0.5,
                k=x[:, 256 + h * ML_DK:256 + (h + 1) * ML_DK],
                v_aug=jnp.concatenate([x[:, 512 + h * ML_DV:512 + (h + 1) * ML_DV], ones], axis=-1),
                dmat=jnp.where(incl, jnp.exp(wide[0:CHUNK, h * ML_DV:h * ML_DV + CHUNK]
                                             + rt[lane0 + h:lane0 + h + 1, 0:CHUNK]), 0.0),
                w_inter=wide[CHUNK:2 * CHUNK, ws], enm=wide[2 * CHUNK:3 * CHUNK, ws],
                dec=wide[3 * CHUNK:3 * CHUNK + 1, ws], wkc=wkc[:, h * ML_DK:(h + 1) * ML_DK],
                c_aug=c_ref[sd], sd=sd, o_ref=o_ref, osl=ws))

    s = [_mm(x["q"], x["k"], _NT) * x["dmat"] for x in hd]
    inter = [_mm(x["q"], x["c_aug"]) for x in hd]
    intra = [_mm(s_h, x["v_aug"]) for s_h, x in zip(s, hd)]
    upd = [_mm(x["wkc"] * x["k"], x["v_aug"], _TN) for x in hd]
    for x, ie, ia, up in zip(hd, inter, intra, upd):
        num = x["w_inter"] * ie[:, :ML_DV] + ia[:, :ML_DV]
        den = x["w_inter"] * ie[:, ML_DV:] + ia[:, ML_DV:]
        x["o_ref"][:, x["osl"]] = num / jnp.maximum(jnp.abs(den), x["enm"])
        c_ref[x["sd"]] = jnp.concatenate([x["dec"], x["dec"]], axis=-1) * x["c_aug"] + up
    m_ref[...] = m_rows[0] + m_rows[1]


def _ml_scan(p, par, c0, m0):
    t = p.shape[0]
    nc = t // CHUNK
    wx = 2 * GROUP_W
    cs, ms = (2 * ML_HEADS, ML_DK, 2 * ML_DV), (1, 128)
    out = jax.ShapeDtypeStruct((t, GROUP_W), F32)
    return pl.pallas_call(
        functools.partial(_ml_kernel, nc),
        grid=(nc,),
        in_specs=[pl.BlockSpec((CHUNK, wx), lambda i: (i, COL_ML_QKV // wx)),
                  pl.BlockSpec((CHUNK, wx), lambda i: (nc - 1 - i, COL_ML_QKV // wx)),
                  _gate_spec(nc, False), _gate_spec(nc, True), _const_spec((8, 128)),
                  _const_spec(cs), _const_spec(ms)],
        out_specs=[_out_spec(nc, GROUP_W, False), _out_spec(nc, GROUP_W, True),
                   _const_spec(cs), _const_spec(ms)],
        out_shape=[out, out, jax.ShapeDtypeStruct(cs, F32), jax.ShapeDtypeStruct(ms, F32)],
        compiler_params=_cparams(("arbitrary",)),
        name="ml_scan",
    )(p, p, p, p, par, c0, m0)


def _lru_kernel(rows, wcols, x_ref, cw_ref, cb_ref, wa_ref, ba_ref, wi_ref, bi_ref, lam_ref,
                h0_ref, o_ref, hfin_ref, xp_scr, a_scr, b_scr):
    t = rows * wcols
    sub = lax.broadcasted_iota(jnp.int32, (wcols, 128), 0)

    def shift_down(a):
        return jnp.where(sub >= 1, pltpu.roll(a, 1, axis=0), 0.0)

    def shift_up(a):
        return jnp.where(sub < wcols - 1, pltpu.roll(a, wcols - 1, axis=0), 0.0)

    def slab(r):
        return pl.ds(pl.multiple_of(r * wcols, wcols), wcols)

    xp_scr[pl.ds(2 * wcols, t), :] = x_ref[...]
    xp_scr[pl.ds(0, wcols), :] = shift_down(x_ref[pl.ds((rows - 2) * wcols, wcols), :])
    xp_scr[pl.ds(wcols, wcols), :] = shift_down(x_ref[pl.ds((rows - 1) * wcols, wcols), :])
    xp_scr[pl.ds((rows + 2) * wcols, wcols), :] = shift_up(x_ref[pl.ds(0, wcols), :])
    xp_scr[pl.ds((rows + 3) * wcols, wcols), :] = shift_up(x_ref[pl.ds(wcols, wcols), :])

    rb = 256 if t % 256 == 0 else t
    for d in range(2):
        sp_lam = jax.nn.softplus(-lam_ref[d:d + 1, :])

        def gate_body(blk, carry):
            base = pl.multiple_of(blk * rb, rb)
            xr = cb_ref[...] + cw_ref[0:1, :] * xp_scr[pl.ds(base, rb), :]
            for j in range(1, SHORT_CONV):
                xr = xr + cw_ref[j:j + 1, :] * xp_scr[pl.ds(pl.multiple_of(base + j * wcols, 8), rb), :]
            xb = xr.astype(BF16)
            r = jax.nn.sigmoid(jnp.dot(xb, wa_ref[d], preferred_element_type=F32) + ba_ref[d:d + 1, :])
            ii = jax.nn.sigmoid(jnp.dot(xb, wi_ref[d], preferred_element_type=F32) + bi_ref[d:d + 1, :])
            log_a = -LRU_C * r * sp_lam
            a = jnp.exp(log_a)
            b = jnp.sqrt(-jnp.tanh(log_a) * (a * a + 1.0)) * ii * xr
            a_scr[pl.ds(base, rb), :] = a
            b_scr[pl.ds(base, rb), :] = b
            return carry

        lax.fori_loop(0, t // rb, gate_body, 0)

        def scan_body(step, carry):
            h, acc = carry
            r = step if d == 0 else rows - 1 - step
            a = a_scr[slab(r), :]
            h = a * h + b_scr[slab(r), :]
            acc = a * acc
            b_scr[slab(r), :] = h
            a_scr[slab(r), :] = acc
            return h, acc

        h_end, a_end = lax.fori_loop(
            0, rows, scan_body, (jnp.zeros((wcols, 128), F32), jnp.ones((wcols, 128), F32)))

        sh = 1
        while sh < wcols:
            if d == 0:
                valid = sub >= sh
                a_sh, h_sh = pltpu.roll(a_end, sh, axis=0), pltpu.roll(h_end, sh, axis=0)
            else:
                valid = sub < wcols - sh
                a_sh, h_sh = pltpu.roll(a_end, wcols - sh, axis=0), pltpu.roll(h_end, wcols - sh, axis=0)
            h_end = jnp.where(valid, a_end * h_sh + h_end, h_end)
            a_end = jnp.where(valid, a_end * a_sh, a_end)
            sh *= 2
        h0 = h0_ref[d:d + 1, :]
        h_full = h_end + a_end * h0
        if d == 0:
            carry_in = jnp.where(sub >= 1, pltpu.roll(h_full, 1, axis=0), h0)
            hfin_ref[0:1, :] = h_full[wcols - 1:wcols, :]
        else:
            carry_in = jnp.where(sub < wcols - 1, pltpu.roll(h_full, wcols - 1, axis=0), h0)
            hfin_ref[1:2, :] = h_full[0:1, :]

        def fix_body(r, carry):
            hv = b_scr[slab(r), :] + a_scr[slab(r), :] * carry_in
            if d == 0:
                o_ref[slab(r), :] = hv
            else:
                o_ref[slab(r), :] += hv
            return carry

        lax.fori_loop(0, rows, fix_body, 0)


def _lru_scan(x, colblk, wcols, cw, cb, wa, ba, wi, bi, lam, h0):
    t = x.shape[0]
    rows = t // wcols
    nt = LRU_W // 128
    vec = lambda r: pl.BlockSpec((r, 128), lambda j: (0, j))
    wspec = pl.BlockSpec((2, 128, 128), lambda j: (0, j, j))
    return pl.pallas_call(
        functools.partial(_lru_kernel, rows, wcols),
        grid=(nt,),
        in_specs=[pl.BlockSpec((t, 128), lambda j: (0, colblk + j)),
                  vec(SHORT_CONV), vec(1), wspec, vec(2), wspec, vec(2), vec(2), vec(2)],
        out_specs=[pl.BlockSpec((t, 128), lambda j: (0, j)), vec(2)],
        out_shape=[jax.ShapeDtypeStruct((t, LRU_W), F32), jax.ShapeDtypeStruct((2, LRU_W), F32)],
        scratch_shapes=[pltpu.VMEM((t + 4 * wcols, 128), F32), pltpu.VMEM((t, 128), F32),
                        pltpu.VMEM((t, 128), F32)],
        compiler_params=_cparams(("arbitrary",)),
        name="lru_scan",
    )(x, cw, cb, wa, ba, wi, bi, lam, h0)


def _rms_groups(x, g, width):
    parts = []
    for s in range(0, x.shape[-1], width):
        xs = x[:, s:s + width]
        parts.append(xs * lax.rsqrt(jnp.mean(xs * xs, axis=-1, keepdims=True) + EPS) * g[:, s:s + width])
    return parts


def _outproj_kernel(x_ref, g1_ref, dnf, dnb, dnz, dng, sdf, sdb, sdx, sdz, sdd, sdg,
                    lrh, lrg, mlf, mlb, mlo, mlg, w_ref, o_ref):
    acc = jnp.zeros(o_ref.shape, F32)
    z = dnz[...]
    a_parts = _rms_groups(dnf[...] + dnb[...], dng[...], DN_DV)
    for h, part in enumerate(a_parts):
        zz = z[:, h * DN_DV:(h + 1) * DN_DV]
        acc += _mm(part * _silu(zz), w_ref[h * DN_DV:(h + 1) * DN_DV, :])
    y = (sdf[...] + sdb[...] + sdd[...] * sdx[...]) * _silu(sdz[...])
    gw = GROUP_W // SSD_GROUPS
    for g, part in enumerate(_rms_groups(y, sdg[...], gw)):
        acc += _mm(part, w_ref[GROUP_W + g * gw:GROUP_W + (g + 1) * gw, :])
    acc += _mm(lrh[...] * jax.nn.gelu(lrg[...]), w_ref[2 * GROUP_W:3 * GROUP_W, :])
    o = mlo[...]
    for h, part in enumerate(_rms_groups(mlf[...] + mlb[...], mlg[...], ML_DV)):
        oo = o[:, h * ML_DV:(h + 1) * ML_DV]
        acc += _mm(part * jax.nn.sigmoid(oo), w_ref[3 * GROUP_W + h * ML_DV:3 * GROUP_W + (h + 1) * ML_DV, :])
    o_ref[...] = x_ref[...] + g1_ref[...] * acc


def _outproj(x, g1, p, dn_f, dn_b, dn_g, sd_f, sd_b, sd_x, sd_d, sd_g, lr_h, ml_f, ml_b, ml_g, w):
    t = x.shape[0]
    tm = min(t, 256)
    tok = pl.BlockSpec((tm, GROUP_W), lambda i: (i, 0))
    pcol = lambda col: pl.BlockSpec((tm, GROUP_W), lambda i: (i, col // GROUP_W))
    vec = lambda n: pl.BlockSpec((1, n), lambda i: (0, 0))
    return pl.pallas_call(
        _outproj_kernel,
        grid=(t // tm,),
        in_specs=[pl.BlockSpec((tm, D_MODEL), lambda i: (i, 0)), vec(D_MODEL),
                  tok, tok, pcol(COL_DN_Z), vec(GROUP_W),
                  tok, tok, tok, pcol(COL_SSD_Z), vec(GROUP_W), vec(GROUP_W),
                  tok, pcol(COL_LRU_G),
                  tok, tok, pcol(COL_ML_O), vec(GROUP_W),
                  pl.BlockSpec((D_MODEL, D_MODEL), lambda i: (0, 0))],
        out_specs=pl.BlockSpec((tm, D_MODEL), lambda i: (i, 0)),
        out_shape=jax.ShapeDtypeStruct((t, D_MODEL), F32),
        compiler_params=_cparams(("arbitrary",)),
        name="outproj",
    )(x, g1, dn_f, dn_b, p, dn_g, sd_f, sd_b, sd_x, p, sd_d, sd_g, lr_h, p, ml_f, ml_b, p, ml_g, w)


FFN_HALO = 16


def _ffn_kernel(nt, nk, final, x_ref, xp_ref, xn_ref, ng_ref, sc_ref, sh_ref, g2_ref, fg_ref,
                wu_ref, wg_ref, cw_ref, wd_ref, o_ref, h_scr, g_scr, acc_scr):
    i = pl.program_id(0)
    k = pl.program_id(1)
    tm = x_ref.shape[0]

    @pl.when(k == 0)
    def _():
        ng, sc, sh = ng_ref[...], sc_ref[...], sh_ref[...]
        hp = jnp.where(i > 0, _norm_mod(xp_ref[...], ng, sc, sh), 0.0)
        hn = jnp.where(i < nt - 1, _norm_mod(xn_ref[...], ng, sc, sh), 0.0)
        h_scr[0:FFN_HALO, :] = hp.astype(BF16)
        h_scr[FFN_HALO:FFN_HALO + tm, :] = _norm_mod(x_ref[...], ng, sc, sh).astype(BF16)
        h_scr[FFN_HALO + tm:2 * FFN_HALO + tm, :] = hn.astype(BF16)
        acc_scr[...] = jnp.zeros_like(acc_scr)

    u = jnp.dot(h_scr[FFN_HALO:FFN_HALO + tm, :], wu_ref[...], preferred_element_type=F32)
    g_scr[...] = jnp.dot(h_scr[...], wg_ref[...], preferred_element_type=F32)
    conv = (cw_ref[0:1, :] * g_scr[pl.ds(FFN_HALO - 1, tm), :]
            + cw_ref[1:2, :] * g_scr[pl.ds(FFN_HALO, tm), :]
            + cw_ref[2:3, :] * g_scr[pl.ds(FFN_HALO + 1, tm), :])
    acc_scr[...] += jnp.dot((_silu(conv) * u).astype(BF16), wd_ref[...], preferred_element_type=F32)

    @pl.when(k == nk - 1)
    def _():
        y = x_ref[...] + g2_ref[...] * acc_scr[...]
        if final:
            y = y * lax.rsqrt(jnp.mean(y * y, axis=-1, keepdims=True) + EPS) * fg_ref[...]
        o_ref[...] = y


def _ffn(x, ng, sc, sh, g2, fg, w_up, cw, w_down, final):
    t = x.shape[0]
    tm = min(t, 512)
    bk = 512
    nt, nk = t // tm, D_FF // bk
    nhb = t // FFN_HALO
    per = tm // FFN_HALO
    vec = pl.BlockSpec((1, D_MODEL), lambda i, k: (0, 0))
    return pl.pallas_call(
        functools.partial(_ffn_kernel, nt, nk, final),
        grid=(nt, nk),
        in_specs=[pl.BlockSpec((tm, D_MODEL), lambda i, k: (i, 0)),
                  pl.BlockSpec((FFN_HALO, D_MODEL), lambda i, k: (jnp.maximum(i * per - 1, 0), 0)),
                  pl.BlockSpec((FFN_HALO, D_MODEL), lambda i, k: (jnp.minimum((i + 1) * per, nhb - 1), 0)),
                  vec, vec, vec, vec, vec,
                  pl.BlockSpec((D_MODEL, bk), lambda i, k: (0, k)),
                  pl.BlockSpec((D_MODEL, bk), lambda i, k: (0, nk + k)),
                  pl.BlockSpec((FFN_CONV, bk), lambda i, k: (0, k)),
                  pl.BlockSpec((bk, D_MODEL), lambda i, k: (k, 0))],
        out_specs=pl.BlockSpec((tm, D_MODEL), lambda i, k: (i, 0)),
        out_shape=jax.ShapeDtypeStruct((t, D_MODEL), F32),
        scratch_shapes=[pltpu.VMEM((tm + 2 * FFN_HALO, D_MODEL), BF16),
                        pltpu.VMEM((tm + 2 * FFN_HALO, bk), F32),
                        pltpu.VMEM((tm, D_MODEL), F32)],
        compiler_params=_cparams(("arbitrary", "arbitrary")),
        name="ffn",
    )(x, x, x, ng, sc, sh, g2, fg, w_up, w_up, cw, w_down)


def _perm_w_in(w):
    cols = [w[:, _SRC[n][0]:_SRC[n][0] + _SRC[n][1]] for n in _DST_ORDER]
    used = sum(_SRC[n][1] for n in _DST_ORDER)
    cols.append(jnp.zeros((w.shape[0], N_PROJ - used), w.dtype))
    return jnp.concatenate(cols, axis=1).astype(BF16)


def _lane_row(pairs):
    tile = jnp.zeros((8, 128), F32)
    for row, lane, vals in pairs:
        vals = vals.reshape(-1).astype(F32)
        tile = tile.at[row, lane:lane + vals.shape[0]].set(vals)
    return tile


def _block_diag(w):
    nb, bw = w.shape[1], w.shape[2]
    eye = jnp.eye(nb, dtype=w.dtype)
    full = jnp.einsum("dnjk,nm->dnjmk", w, eye)
    return full.reshape(2, nb * bw, nb * bw).astype(BF16)


def _ctx_to_cols(a, wcols):
    t, ch = a.shape
    return a.reshape(wcols, t // wcols, ch).transpose(1, 0, 2).reshape(t, ch)


def _ctx_from_cols(a, wcols):
    t, ch = a.shape
    return a.reshape(t // wcols, wcols, ch).transpose(1, 0, 2).reshape(t, ch)


CTX_COLS = 8


def kernel(x, c, ctx, c_ctx, ada_w, ada_b, norm1_g, norm2_g, w_in, dn_conv_w, dn_a_log, dn_dt_bias, dn_norm_g, ssd_conv_w, ssd_conv_b, ssd_a_log, ssd_dt_bias, ssd_d, ssd_norm_g, lru_conv_w, lru_conv_b, lru_w_a, lru_b_a, lru_w_i, lru_b_i, lru_lambda, ml_ig_b, ml_fg_b, ml_norm_g, w_out, ffn_w_up, ffn_conv_w, ffn_w_down, final_norm_g):
    assert x.shape[0] == 1 and c.shape[0] == 1
    lat, hctx = x[0], ctx[0]
    cc = jnp.zeros((8, D_MODEL), F32).at[0].set(c[0]).at[1].set(c_ctx)
    mods = _ada(cc, ada_w, ada_b)
    row = lambda v: v.reshape(1, -1).astype(F32)

    for l in range(DEPTH):
        mod_l = mods[l, 0].reshape(N_MOD, 1, D_MODEL)
        mod_c = mods[l, 1].reshape(N_MOD, 1, D_MODEL)
        w_in_p = _perm_w_in(w_in[l])
        w_out_b = w_out[l].astype(BF16)
        w_up_b = ffn_w_up[l].astype(BF16)
        w_down_b = ffn_w_down[l].astype(BF16)
        ng1, ng2 = row(norm1_g[l]), row(norm2_g[l])

        dn_par = _lane_row([(0, LANE_DN_A, dn_a_log[l]), (1, LANE_DN_A, dn_dt_bias[l])])
        ssd_par = _lane_row([(0, LANE_SSD_DT, ssd_dt_bias[l]), (1, LANE_SSD_DT, ssd_a_log[l])])
        ml_par = _lane_row([(0, LANE_ML_I, ml_ig_b[l]), (1, LANE_ML_F, ml_fg_b[l])])
        lru_args = (lru_conv_w[l], row(lru_conv_b[l]), _block_diag(lru_w_a[l]), lru_b_a[l],
                    _block_diag(lru_w_i[l]), lru_b_i[l], lru_lambda[l])
        dn_g = row(jnp.tile(dn_norm_g[l], DN_HEADS))
        sd_d = row(jnp.repeat(ssd_d[l], SSD_HEAD_DIM))
        sd_g, ml_g = row(ssd_norm_g[l]), row(ml_norm_g[l])

        pc = _inproj(hctx, ng1, mod_c[1], mod_c[0], w_in_p)
        dn_cf, dn_cb, dn_s = _dn_scan(pc, dn_conv_w[l], dn_par,
                                      jnp.zeros((2 * DN_HEADS, DN_DK, DN_DV), F32))
        sd_cf, sd_cb, sd_cx, sd_s = _ssd_scan(
            pc, ssd_conv_w[l], row(ssd_conv_b[l]), ssd_par,
            jnp.zeros((2 * SSD_GROUPS, SSD_STATE, GROUP_W // SSD_GROUPS), F32))
        ml_cf, ml_cb, ml_c, ml_m = _ml_scan(
            pc, ml_par, jnp.zeros((2 * ML_HEADS, ML_DK, 2 * ML_DV), F32), jnp.zeros((1, 128), F32))
        xc_cols = _ctx_to_cols(pc[:, COL_LRU_X:COL_LRU_X + LRU_W], CTX_COLS)
        lr_c, lr_s = _lru_scan(xc_cols, 0, CTX_COLS, *lru_args, jnp.zeros((2, LRU_W), F32))

        pl_ = _inproj(lat, ng1, mod_l[1], mod_l[0], w_in_p)
        dn_lf, dn_lb, _ = _dn_scan(pl_, dn_conv_w[l], dn_par, dn_s)
        sd_lf, sd_lb, sd_lx, _ = _ssd_scan(pl_, ssd_conv_w[l], row(ssd_conv_b[l]), ssd_par, sd_s)
        ml_lf, ml_lb, _, _ = _ml_scan(pl_, ml_par, ml_c, ml_m)
        lr_l, _ = _lru_scan(pl_, COL_LRU_X // 128, GRID_W, *lru_args, lr_s)

        lat = _outproj(lat, mod_l[2], pl_, dn_lf, dn_lb, dn_g, sd_lf, sd_lb, sd_lx, sd_d, sd_g,
                       lr_l, ml_lf, ml_lb, ml_g, w_out_b)
        last = l == DEPTH - 1
        lat = _ffn(lat, ng2, mod_l[4], mod_l[3], mod_l[5], row(final_norm_g), w_up_b,
                   ffn_conv_w[l], w_down_b, last)

        if not last:
            lr_cn = _ctx_from_cols(lr_c, CTX_COLS)
            hctx = _outproj(hctx, mod_c[2], pc, dn_cf, dn_cb, dn_g, sd_cf, sd_cb, sd_cx, sd_d, sd_g,
                            lr_cn, ml_cf, ml_cb, ml_g, w_out_b)
            hctx = _ffn(hctx, ng2, mod_c[4], mod_c[3], mod_c[5], row(final_norm_g), w_up_b,
                        ffn_conv_w[l], w_down_b, False)

    return lat[None]
```

```python
import functools

import jax
import jax.numpy as jnp
from jax import lax
from jax.experimental import pallas as pl
from jax.experimental.pallas import tpu as pltpu

F32 = jnp.float32
BF16 = jnp.bfloat16
HI = lax.Precision.HIGHEST

D_MODEL = 2048
DEPTH = 2
GRID_W = 64
GROUP_W = D_MODEL // 4
CHUNK = 64
SHORT_CONV = 5
FFN_CONV = 3
D_FF = ((8 * D_MODEL // 3 + 255) // 256) * 256
N_MOD = 6
EPS = 1e-6

DN_HEADS = 4
DN_DK = GROUP_W // DN_HEADS
DN_DV = GROUP_W // DN_HEADS
SSD_HEAD_DIM = 64
SSD_HEADS = GROUP_W // SSD_HEAD_DIM
SSD_GROUPS = 2
SSD_STATE = 128
LRU_W = GROUP_W
LRU_BLOCKS = 8
LRU_BW = LRU_W // LRU_BLOCKS
LRU_C = 8.0
ML_HEADS = 4
ML_DV = GROUP_W // ML_HEADS
ML_DK = ML_DV // 2

_SRC = {}
_off = 0
for _name, _w in (
    ("dn_q", 512), ("dn_k", 512), ("dn_v", 512), ("dn_z", 512), ("dn_a", 8), ("dn_b", 8),
    ("ssd_x", 512), ("ssd_z", 512), ("ssd_B", 256), ("ssd_C", 256), ("ssd_dt", 16),
    ("lru_x", 512), ("lru_g", 512),
    ("ml_q", 256), ("ml_k", 256), ("ml_v", 512), ("ml_o", 512), ("ml_i", 8), ("ml_f", 8),
):
    _SRC[_name] = (_off, _w)
    _off += _w
D_IN = _off

_DST_ORDER = ("ssd_x", "ssd_B", "ssd_C", "dn_q", "dn_k", "dn_v", "ml_q", "ml_k", "ml_v",
              "lru_x", "dn_z", "ssd_z", "lru_g", "ml_o", "dn_a", "dn_b", "ssd_dt", "ml_i", "ml_f")
N_PROJ = 6400
COL_SSD_XBC = 0
COL_DN_Q = 1024
COL_DN_K = 1536
COL_DN_V = 2048
N_CONV = 2560
COL_ML_QK = 2560
COL_ML_V = 3072
COL_LRU_X = 3584
COL_DN_Z = 4096
COL_SSD_Z = 4608
COL_LRU_G = 5120
COL_ML_O = 5632
COL_GATES = 6144
LANE_DN_A, LANE_DN_B, LANE_SSD_DT, LANE_ML_I, LANE_ML_F = 0, 8, 16, 32, 40

VMEM_LIMIT = 56 * 1024 * 1024


def _cparams(sem):
    return pltpu.CompilerParams(dimension_semantics=sem, vmem_limit_bytes=VMEM_LIMIT)


_NN = (((1,), (0,)), ((), ()))
_NT = (((1,), (1,)), ((), ()))
_TN = (((0,), (0,)), ((), ()))


def _dg(a, b, dims):
    return lax.dot_general(a, b, dims, preferred_element_type=F32)


def _mm(a, b, dims=_NN):
    return _dg(a.astype(BF16), b.astype(BF16), dims)


def _split2(a):
    hi = a.astype(BF16)
    return hi, (a - hi.astype(F32)).astype(BF16)


def _split3(a):
    hi = a.astype(BF16)
    r = a - hi.astype(F32)
    mid = r.astype(BF16)
    return hi, mid, (r - mid.astype(F32)).astype(BF16)


def _mm3(a, b, dims=_NN):
    ah, al = _split2(a)
    bh, bl = _split2(b)
    return _dg(ah, bh, dims) + (_dg(ah, bl, dims) + _dg(al, bh, dims))


_mm_inv = _mm


def _mm_sel(sel, x):
    sb = sel.astype(BF16)
    x0, x1, x2 = _split3(x)
    return _dg(sb, x0, _NN) + (_dg(sb, x1, _NN) + _dg(sb, x2, _NN))


def _mm_spread(x, sel):
    sb = sel.astype(BF16)
    x0, x1, x2 = _split3(x)
    return _dg(x0, sb, _NN) + (_dg(x1, sb, _NN) + _dg(x2, sb, _NN))


def _silu(x):
    return x * jax.nn.sigmoid(x)


def _masks(d):
    ri = lax.broadcasted_iota(jnp.int32, (CHUNK, CHUNK), 0)
    ci = lax.broadcasted_iota(jnp.int32, (CHUNK, CHUNK), 1)
    if d == 0:
        return ri >= ci, ri > ci
    return ri <= ci, ri < ci


def _seg_decay(col, row, incl):
    return jnp.where(incl, jnp.exp(jnp.where(incl, col - row, 0.0)), 0.0)


def _ada_kernel(c_ref, w_ref, b_ref, o_ref):
    cc = c_ref[...]
    act = (cc * jax.nn.sigmoid(cc)).astype(BF16)
    o_ref[0] = jnp.dot(act, w_ref[0].astype(BF16), preferred_element_type=F32) + b_ref[0]


def _ada(cc, ada_w, ada_b):
    tn = 1024
    n = N_MOD * D_MODEL
    return pl.pallas_call(
        _ada_kernel,
        grid=(DEPTH, n // tn),
        in_specs=[
            pl.BlockSpec((8, D_MODEL), lambda l, j: (0, 0)),
            pl.BlockSpec((1, D_MODEL, tn), lambda l, j: (l, 0, j)),
            pl.BlockSpec((1, 1, tn), lambda l, j: (l, 0, j)),
        ],
        out_specs=pl.BlockSpec((1, 8, tn), lambda l, j: (l, 0, j)),
        out_shape=jax.ShapeDtypeStruct((DEPTH, 8, n), F32),
        compiler_params=_cparams(("arbitrary", "arbitrary")),
        name="ada",
    )(cc, ada_w, ada_b.reshape(DEPTH, 1, n))


def _norm_mod(xf, ng, sc, sh):
    y = xf * lax.rsqrt(jnp.mean(xf * xf, axis=-1, keepdims=True) + EPS) * ng
    return y * (1.0 + sc) + sh


HALO = 16
PROJ_TN = 1280
PROJ_SUB = 256


def _fill_halo_tile(h_scr, i, nt, x_ref, xp_ref, xn_ref, ng, sc, sh):
    tm = x_ref.shape[0]
    hp = jnp.where(i > 0, _norm_mod(xp_ref[...], ng, sc, sh), 0.0)
    hn = jnp.where(i < nt - 1, _norm_mod(xn_ref[...], ng, sc, sh), 0.0)
    h_scr[0:HALO, :] = hp.astype(BF16)
    h_scr[HALO:HALO + tm, :] = _norm_mod(x_ref[...], ng, sc, sh).astype(BF16)
    h_scr[HALO + tm:2 * HALO + tm, :] = hn.astype(BF16)


def _inproj_kernel(nt, x_ref, xp_ref, xn_ref, ng_ref, sc_ref, sh_ref, w_ref, cw_ref, cb_ref,
                   o_ref, h_scr):
    i = pl.program_id(0)
    j = pl.program_id(1)
    tm = x_ref.shape[0]

    @pl.when(j == 0)
    def _():
        _fill_halo_tile(h_scr, i, nt, x_ref, xp_ref, xn_ref, ng_ref[...], sc_ref[...], sh_ref[...])

    for jj in range(N_CONV // PROJ_TN):
        @pl.when(j == jj)
        def _():
            for c in range(PROJ_TN // PROJ_SUB):
                cs = slice(c * PROJ_SUB, (c + 1) * PROJ_SUB)
                y = jnp.dot(h_scr[...], w_ref[:, cs], preferred_element_type=F32)
                rows = tm + 2 * HALO
                acc = cb_ref[:, cs] + cw_ref[2:3, cs] * y[HALO:HALO + tm]
                for tap in (0, 1, 3, 4):
                    shifted = pltpu.roll(y, (2 - tap) % rows, axis=0)
                    acc = acc + cw_ref[tap:tap + 1, cs] * shifted[HALO:HALO + tm]
                act = _silu(acc)
                for half in range(PROJ_SUB // 128):
                    col = jj * PROJ_TN + c * PROJ_SUB + half * 128
                    a = act[:, half * 128:(half + 1) * 128]
                    if COL_DN_Q <= col < COL_DN_V:
                        a = a * lax.rsqrt(jnp.sum(a * a, axis=-1, keepdims=True) + EPS)
                        if col < COL_DN_K:
                            a = a * DN_DK ** -0.5
                    o_ref[:, c * PROJ_SUB + half * 128:c * PROJ_SUB + (half + 1) * 128] = a

    @pl.when(j >= N_CONV // PROJ_TN)
    def _():
        o_ref[...] = jnp.dot(h_scr[HALO:HALO + tm, :], w_ref[...], preferred_element_type=F32)


def _halo_specs(t, tm, nidx):
    per, nhb = tm // HALO, t // HALO
    if nidx == 1:
        return [pl.BlockSpec((tm, D_MODEL), lambda i: (i, 0)),
                pl.BlockSpec((HALO, D_MODEL), lambda i: (jnp.maximum(i * per - 1, 0), 0)),
                pl.BlockSpec((HALO, D_MODEL), lambda i: (jnp.minimum((i + 1) * per, nhb - 1), 0))]
    return [pl.BlockSpec((tm, D_MODEL), lambda i, k: (i, 0)),
            pl.BlockSpec((HALO, D_MODEL), lambda i, k: (jnp.maximum(i * per - 1, 0), 0)),
            pl.BlockSpec((HALO, D_MODEL), lambda i, k: (jnp.minimum((i + 1) * per, nhb - 1), 0))]


def _inproj(x, ng, sc, sh, w, cw, cb):
    t = x.shape[0]
    tm = min(t, 1024)
    tn = PROJ_TN
    nconv = N_CONV // tn
    row = lambda i, j: (0, 0)
    return pl.pallas_call(
        functools.partial(_inproj_kernel, t // tm),
        grid=(t // tm, N_PROJ // tn),
        in_specs=_halo_specs(t, tm, 2) + [
            pl.BlockSpec((1, D_MODEL), row),
            pl.BlockSpec((1, D_MODEL), row),
            pl.BlockSpec((1, D_MODEL), row),
            pl.BlockSpec((D_MODEL, tn), lambda i, j: (0, j)),
            pl.BlockSpec((SHORT_CONV, tn), lambda i, j: (0, jnp.minimum(j, nconv - 1))),
            pl.BlockSpec((1, tn), lambda i, j: (0, jnp.minimum(j, nconv - 1))),
        ],
        out_specs=pl.BlockSpec((tm, tn), lambda i, j: (i, j)),
        out_shape=jax.ShapeDtypeStruct((t, N_PROJ), F32),
        scratch_shapes=[pltpu.VMEM((tm + 2 * HALO, D_MODEL), BF16)],
        compiler_params=_cparams(("arbitrary", "arbitrary")),
        name="inproj",
    )(x, x, x, ng, sc, sh, w, cw, cb)


SUBS = 4
ROWS = SUBS * CHUNK


def _sub_order(d):
    return list(range(SUBS)) if d == 0 else list(range(SUBS - 1, -1, -1))


def _col_spec(nb, width, col, rev):
    blk = col // width
    assert blk * width == col
    if rev:
        return pl.BlockSpec((ROWS, width), lambda i: (nb - 1 - i, blk))
    return pl.BlockSpec((ROWS, width), lambda i: (i, blk))


def _gate_spec(nb, rev):
    return _col_spec(nb, 128, COL_GATES, rev)


def _out_spec(nb, width, rev):
    if rev:
        return pl.BlockSpec((ROWS, width), lambda i: (nb - 1 - i, 0))
    return pl.BlockSpec((ROWS, width), lambda i: (i, 0))


def _const_spec(shape):
    nd = len(shape)
    return pl.BlockSpec(shape, lambda i: (0,) * nd)


def _cumsum_dir(x, d):
    ri = lax.broadcasted_iota(jnp.int32, (ROWS, ROWS), 0)
    ci = lax.broadcasted_iota(jnp.int32, (ROWS, ROWS), 1)
    incl = (ri >= ci) if d == 0 else (ri <= ci)
    return _mm_sel(jnp.where((ri // CHUNK == ci // CHUNK) & incl, 1.0, 0.0), x)


def _dn_kernel(qf, kf, vf, qb, kb_, vb_, gf, gb, par_ref, s0_ref, of_ref, ob_ref, s_ref):
    i = pl.program_id(0)

    @pl.when(i == 0)
    def _():
        s_ref[...] = s0_ref[...]

    eye = jnp.where(_masks(0)[0] & _masks(1)[0], 1.0, 0.0).astype(F32)
    hd = []
    for d, (q_ref, k_ref, v_ref, g_ref, o_ref) in enumerate(((qf, kf, vf, gf, of_ref),
                                                             (qb, kb_, vb_, gb, ob_ref))):
        incl, strict = _masks(d)
        last = CHUNK - 1 if d == 0 else 0
        gates = g_ref[...]
        g_all = -jnp.exp(par_ref[0:1, :]) * jax.nn.softplus(gates + par_ref[1:2, :])
        beta_all = jax.nn.sigmoid(gates)
        gc = _cumsum_dir(g_all, d)
        gct = gc.T
        for pos, sub in enumerate(_sub_order(d)):
            rs = slice(sub * CHUNK, (sub + 1) * CHUNK)
            for h in range(DN_HEADS):
                lane = LANE_DN_A + d * DN_HEADS + h
                blane = LANE_DN_B + d * DN_HEADS + h
                k = k_ref[rs, h * DN_DK:(h + 1) * DN_DK]
                gcc = gc[rs, lane:lane + 1]
                beta = beta_all[rs, blane:blane + 1]
                hd.append(dict(
                    q=q_ref[rs, h * DN_DK:(h + 1) * DN_DK], k=k, kb=k * beta,
                    vb=v_ref[rs, h * DN_DV:(h + 1) * DN_DV] * beta, gcc=gcc,
                    tot=gc[sub * CHUNK + last:sub * CHUNK + last + 1, lane:lane + 1],
                    strict=strict, decay=_seg_decay(gcc, gct[lane:lane + 1, rs], incl),
                    pos=pos, rs=rs, o_ref=o_ref, osl=slice(h * DN_DV, (h + 1) * DN_DV),
                    sidx=d * DN_HEADS + h))

    nmat = [jnp.where(x["strict"], _mm(x["kb"], x["k"], _NT) * x["decay"], 0.0) for x in hd]
    attn = [_mm(x["q"], x["k"], _NT) * x["decay"] for x in hd]
    ri = lax.broadcasted_iota(jnp.int32, (CHUNK, CHUNK), 0)
    ci = lax.broadcasted_iota(jnp.int32, (CHUNK, CHUNK), 1)
    base = 8
    pw = [jnp.where(ri // base == ci // base, n, 0.0) for n in nmat]
    tinv = [eye - p for p in pw]
    for _ in range(2):
        pw = [_mm_inv(p, p) for p in pw]
        tinv = [t + _mm_inv(t, p) for t, p in zip(tinv, pw)]
    size = base
    while size < CHUNK:
        sib = (ri // (2 * size) == ci // (2 * size)) & (ri // size != ci // size)
        tc = [_mm_inv(t, jnp.where(sib, n, 0.0)) for t, n in zip(tinv, nmat)]
        tinv = [t - _mm_inv(c, t) for t, c in zip(tinv, tc)]
        size *= 2
    sol = [_mm_inv(t, jnp.concatenate([x["vb"], x["kb"] * jnp.exp(x["gcc"])], axis=-1))
           for t, x in zip(tinv, hd)]
    state = {j: s_ref[j] for j in range(2 * DN_HEADS)}
    for pos in range(SUBS):
        cur = [(x, so, a) for x, so, a in zip(hd, sol, attn) if x["pos"] == pos]
        v_new = [so[:, :DN_DV] - _mm(so[:, DN_DV:], state[x["sidx"]]) for x, so, _ in cur]
        for (x, _, a), vn in zip(cur, v_new):
            x["o_ref"][x["rs"], x["osl"]] = (_mm(x["q"] * jnp.exp(x["gcc"]), state[x["sidx"]])
                                             + _mm(a, vn))
        for (x, _, _), vn in zip(cur, v_new):
            state[x["sidx"]] = (state[x["sidx"]] * jnp.exp(x["tot"])
                                + _mm(x["k"] * jnp.exp(x["tot"] - x["gcc"]), vn, _TN))
    for j in range(2 * DN_HEADS):
        s_ref[j] = state[j]


def _dn_scan(p, par, s0):
    t = p.shape[0]
    nc = t // ROWS
    out = jax.ShapeDtypeStruct((t, GROUP_W), F32)
    qkv = lambda rev: [_col_spec(nc, GROUP_W, col, rev) for col in (COL_DN_Q, COL_DN_K, COL_DN_V)]
    return pl.pallas_call(
        _dn_kernel,
        grid=(nc,),
        in_specs=(qkv(False) + qkv(True)
                  + [_gate_spec(nc, False), _gate_spec(nc, True), _const_spec((8, 128)),
                     _const_spec((2 * DN_HEADS, DN_DK, DN_DV))]),
        out_specs=[_out_spec(nc, GROUP_W, False), _out_spec(nc, GROUP_W, True),
                   _const_spec((2 * DN_HEADS, DN_DK, DN_DV))],
        out_shape=[out, out, jax.ShapeDtypeStruct((2 * DN_HEADS, DN_DK, DN_DV), F32)],
        compiler_params=_cparams(("arbitrary",)),
        name="dn_scan",
    )(p, p, p, p, p, p, p, p, par, s0)


def _ssd_kernel(xf, xb, gf, gb, par_ref, s0_ref, of_ref, ob_ref, s_ref):
    i = pl.program_id(0)

    @pl.when(i == 0)
    def _():
        s_ref[...] = s0_ref[...]

    hp = SSD_HEADS * SSD_HEAD_DIM
    gw = hp // SSD_GROUPS
    rep = SSD_HEADS // SSD_GROUPS
    items = []
    for d, (x_ref, g_ref, o_ref) in enumerate(((xf, gf, of_ref), (xb, gb, ob_ref))):
        incl, _ = _masks(d)
        last = CHUNK - 1 if d == 0 else 0
        gates = g_ref[...]
        dt_all = jax.nn.softplus(gates + par_ref[0:1, :])
        a_all = -jnp.exp(par_ref[1:2, :]) * dt_all
        acs = _cumsum_dir(a_all, d)
        acst = acs.T
        er = lax.broadcasted_iota(jnp.int32, (128, hp), 0)
        ec = lax.broadcasted_iota(jnp.int32, (128, hp), 1)
        expand = jnp.where(er == LANE_SSD_DT + d * SSD_HEADS + ec // SSD_HEAD_DIM, 1.0, 0.0)
        both_x = _mm_spread(jnp.concatenate([dt_all, acs], axis=0), expand)
        dt_x, acs_x = both_x[:ROWS], both_x[ROWS:]
        xd_all = x_ref[:, :hp] * dt_x
        for pos, sub in enumerate(_sub_order(d)):
            rs = slice(sub * CHUNK, (sub + 1) * CHUNK)
            tot_x = acs_x[sub * CHUNK + last:sub * CHUNK + last + 1, :]
            xd = xd_all[rs]
            xdw = xd * jnp.exp(tot_x - acs_x[rs])
            for g in range(SSD_GROUPS):
                bm = x_ref[rs, hp + g * SSD_STATE:hp + (g + 1) * SSD_STATE]
                cm = x_ref[rs, hp + SSD_GROUPS * SSD_STATE + g * SSD_STATE:
                           hp + SSD_GROUPS * SSD_STATE + (g + 1) * SSD_STATE]
                cb = _mm(cm, bm, _NT)
                gs = slice(g * gw, (g + 1) * gw)
                y_diag = []
                for hh in range(rep):
                    h = g * rep + hh
                    lane = LANE_SSD_DT + d * SSD_HEADS + h
                    hs = slice(h * SSD_HEAD_DIM, (h + 1) * SSD_HEAD_DIM)
                    lmat = _seg_decay(acs[rs, lane:lane + 1], acst[lane:lane + 1, rs], incl)
                    y_diag.append(_mm(cb * lmat, xd[:, hs]))
                items.append(dict(
                    pos=pos, rs=rs, gs=gs, o_ref=o_ref, sidx=d * SSD_GROUPS + g, cm=cm,
                    y_diag=jnp.concatenate(y_diag, axis=-1), off_scale=jnp.exp(acs_x[rs, gs]),
                    dec=jnp.exp(tot_x[:, gs]), local=_mm(bm, xdw[:, gs], _TN)))
    state = {j: s_ref[j] for j in range(2 * SSD_GROUPS)}
    for pos in range(SUBS):
        for x in [x for x in items if x["pos"] == pos]:
            s = state[x["sidx"]]
            x["o_ref"][x["rs"], x["gs"]] = _mm(x["cm"], s) * x["off_scale"] + x["y_diag"]
            state[x["sidx"]] = s * x["dec"] + x["local"]
    for j in range(2 * SSD_GROUPS):
        s_ref[j] = state[j]


def _ssd_scan(p, par, s0):
    t = p.shape[0]
    nc = t // ROWS
    wx = 2 * GROUP_W
    hp = SSD_HEADS * SSD_HEAD_DIM
    st = (2 * SSD_GROUPS, SSD_STATE, hp // SSD_GROUPS)
    out = jax.ShapeDtypeStruct((t, hp), F32)
    return pl.pallas_call(
        _ssd_kernel,
        grid=(nc,),
        in_specs=[_col_spec(nc, wx, COL_SSD_XBC, False), _col_spec(nc, wx, COL_SSD_XBC, True),
                  _gate_spec(nc, False), _gate_spec(nc, True), _const_spec((8, 128)),
                  _const_spec(st)],
        out_specs=[_out_spec(nc, hp, False), _out_spec(nc, hp, True), _const_spec(st)],
        out_shape=[out, out, jax.ShapeDtypeStruct(st, F32)],
        compiler_params=_cparams(("arbitrary",)),
        name="ssd_scan",
    )(p, p, p, p, par, s0)


def _cummax_dir(x, d):
    rows = x.shape[0]
    sub = lax.broadcasted_iota(jnp.int32, x.shape, 0) % CHUNK
    sh = 1
    while sh < CHUNK:
        if d == 0:
            x = jnp.where(sub >= sh, jnp.maximum(x, pltpu.roll(x, sh, axis=0)), x)
        else:
            x = jnp.where(sub < CHUNK - sh, jnp.maximum(x, pltpu.roll(x, rows - sh, axis=0)), x)
        sh *= 2
    return x


ML_PACK = 3 * CHUNK + 8


def _ml_kernel(qkf, vf, qkb, vb, gf, gb, par_ref, c0_ref, m0_ref, of_ref, ob_ref, c_ref, m_ref):
    i = pl.program_id(0)

    @pl.when(i == 0)
    def _():
        c_ref[...] = c0_ref[...]
        m_ref[...] = m0_ref[...]

    lane = lax.broadcasted_iota(jnp.int32, (1, 128), 1)
    ones = jnp.ones((CHUNK, ML_DV), F32)
    hd, m_rows = [], []
    for d, (qk_ref, v_ref, g_ref, o_ref) in enumerate(((qkf, vf, gf, of_ref), (qkb, vb, gb, ob_ref))):
        incl, _ = _masks(d)
        last = CHUNK - 1 if d == 0 else 0
        lane0 = LANE_ML_F + d * ML_HEADS
        valid = (lane >= lane0) & (lane < lane0 + ML_HEADS)
        gates = g_ref[...]
        ig_all = pltpu.roll(gates + par_ref[0:1, :], LANE_ML_F - LANE_ML_I, axis=1)
        b_all = _cumsum_dir(jax.nn.log_sigmoid(gates + par_ref[1:2, :]), d)
        r_all = ig_all - b_all
        cmax = _cummax_dir(r_all, d)
        rt = r_all.T
        m_cur = m_ref[...]
        packed, wk_src = [], {}
        for sub in _sub_order(d):
            rs = slice(sub * CHUNK, (sub + 1) * CHUNK)
            b = b_all[rs]
            m_all = b + jnp.maximum(m_cur, cmax[rs])
            b_last = b_all[sub * CHUNK + last:sub * CHUNK + last + 1, :]
            log_g = b_last - b + ig_all[rs]
            m_new = jnp.maximum(b_last + m_cur, jnp.max(log_g, axis=0, keepdims=True))
            packed += [b - m_all, jnp.exp(b + m_cur - m_all), jnp.exp(-m_all),
                       jnp.broadcast_to(jnp.exp(b_last + m_cur - m_new), (8, 128))]
            wk_src[sub] = jnp.exp(log_g - m_new)
            m_cur = m_new
        m_rows.append(jnp.where(valid, m_cur, 0.0))
        sr = lax.broadcasted_iota(jnp.int32, (128, ML_HEADS * ML_DV), 0)
        sc = lax.broadcasted_iota(jnp.int32, (128, ML_HEADS * ML_DV), 1)
        wide = _mm_spread(jnp.where(valid, jnp.concatenate(packed, axis=0), 0.0),
                          jnp.where(sr == lane0 + sc // ML_DV, 1.0, 0.0))
        sr = lax.broadcasted_iota(jnp.int32, (128, ML_HEADS * ML_DK), 0)
        sc = lax.broadcasted_iota(jnp.int32, (128, ML_HEADS * ML_DK), 1)
        wkc = _mm_spread(
            jnp.where(valid, jnp.concatenate([wk_src[s] for s in range(SUBS)], axis=0), 0.0),
            jnp.where(sr == lane0 + sc // ML_DK, 1.0, 0.0))
        for pos, sub in enumerate(_sub_order(d)):
            rs = slice(sub * CHUNK, (sub + 1) * CHUNK)
            r0 = pos * ML_PACK
            for h in range(ML_HEADS):
                ws = slice(h * ML_DV, (h + 1) * ML_DV)
                k = qk_ref[rs, ML_HEADS * ML_DK + h * ML_DK:ML_HEADS * ML_DK + (h + 1) * ML_DK]
                hd.append(dict(
                    q=qk_ref[rs, h * ML_DK:(h + 1) * ML_DK] * ML_DK ** -0.5, k=k,
                    wk=wkc[rs, h * ML_DK:(h + 1) * ML_DK] * k,
                    v_aug=jnp.concatenate([v_ref[rs, h * ML_DV:(h + 1) * ML_DV], ones], axis=-1),
                    dmat=jnp.where(incl, jnp.exp(wide[r0:r0 + CHUNK, h * ML_DV:h * ML_DV + CHUNK]
                                                 + rt[lane0 + h:lane0 + h + 1, rs]), 0.0),
                    w_inter=wide[r0 + CHUNK:r0 + 2 * CHUNK, ws],
                    enm=wide[r0 + 2 * CHUNK:r0 + 3 * CHUNK, ws],
                    dec=wide[r0 + 3 * CHUNK:r0 + 3 * CHUNK + 1, ws],
                    pos=pos, rs=rs, sd=d * ML_HEADS + h, o_ref=o_ref, osl=ws))

    s = [_mm(x["q"], x["k"], _NT) * x["dmat"] for x in hd]
    intra = [_mm(s_h, x["v_aug"]) for s_h, x in zip(s, hd)]
    upd = [_mm(x["wk"], x["v_aug"], _TN) for x in hd]
    state = {j: c_ref[j] for j in range(2 * ML_HEADS)}
    for pos in range(SUBS):
        cur = [(x, ia, up) for x, ia, up in zip(hd, intra, upd) if x["pos"] == pos]
        inter = [_mm(x["q"], state[x["sd"]]) for x, _, _ in cur]
        for (x, ia, up), ie in zip(cur, inter):
            num = x["w_inter"] * ie[:, :ML_DV] + ia[:, :ML_DV]
            den = x["w_inter"] * ie[:, ML_DV:] + ia[:, ML_DV:]
            x["o_ref"][x["rs"], x["osl"]] = num / jnp.maximum(jnp.abs(den), x["enm"])
            state[x["sd"]] = jnp.concatenate([x["dec"], x["dec"]], axis=-1) * state[x["sd"]] + up
    for j in range(2 * ML_HEADS):
        c_ref[j] = state[j]
    m_ref[...] = m_rows[0] + m_rows[1]


def _ml_scan(p, par, c0, m0):
    t = p.shape[0]
    nc = t // ROWS
    cs, ms = (2 * ML_HEADS, ML_DK, 2 * ML_DV), (1, 128)
    out = jax.ShapeDtypeStruct((t, GROUP_W), F32)
    qkv = lambda rev: [_col_spec(nc, GROUP_W, COL_ML_QK, rev), _col_spec(nc, GROUP_W, COL_ML_V, rev)]
    return pl.pallas_call(
        _ml_kernel,
        grid=(nc,),
        in_specs=(qkv(False) + qkv(True)
                  + [_gate_spec(nc, False), _gate_spec(nc, True), _const_spec((8, 128)),
                     _const_spec(cs), _const_spec(ms)]),
        out_specs=[_out_spec(nc, GROUP_W, False), _out_spec(nc, GROUP_W, True),
                   _const_spec(cs), _const_spec(ms)],
        out_shape=[out, out, jax.ShapeDtypeStruct(cs, F32), jax.ShapeDtypeStruct(ms, F32)],
        compiler_params=_cparams(("arbitrary",)),
        name="ml_scan",
    )(p, p, p, p, p, p, par, c0, m0)


def _lru_kernel(rows, wcols, x_ref, cw_ref, cb_ref, wa_ref, ba_ref, wi_ref, bi_ref, lam_ref,
                h0_ref, o_ref, hfin_ref, xp_scr, a_scr, b_scr):
    t = rows * wcols
    sub = lax.broadcasted_iota(jnp.int32, (wcols, 128), 0)

    def shift_down(a):
        return jnp.where(sub >= 1, pltpu.roll(a, 1, axis=0), 0.0)

    def shift_up(a):
        return jnp.where(sub < wcols - 1, pltpu.roll(a, wcols - 1, axis=0), 0.0)

    def slab(r):
        return pl.ds(pl.multiple_of(r * wcols, wcols), wcols)

    xp_scr[pl.ds(2 * wcols, t), :] = x_ref[...]
    xp_scr[pl.ds(0, wcols), :] = shift_down(x_ref[pl.ds((rows - 2) * wcols, wcols), :])
    xp_scr[pl.ds(wcols, wcols), :] = shift_down(x_ref[pl.ds((rows - 1) * wcols, wcols), :])
    xp_scr[pl.ds((rows + 2) * wcols, wcols), :] = shift_up(x_ref[pl.ds(0, wcols), :])
    xp_scr[pl.ds((rows + 3) * wcols, wcols), :] = shift_up(x_ref[pl.ds(wcols, wcols), :])

    rb = 256 if t % 256 == 0 else t
    for d in range(2):
        sp_lam = jax.nn.softplus(-lam_ref[d:d + 1, :])

        def gate_body(blk, carry):
            base = pl.multiple_of(blk * rb, rb)
            xr = cb_ref[...] + cw_ref[0:1, :] * xp_scr[pl.ds(base, rb), :]
            for j in range(1, SHORT_CONV):
                xr = xr + cw_ref[j:j + 1, :] * xp_scr[pl.ds(pl.multiple_of(base + j * wcols, 8), rb), :]
            xb = xr.astype(BF16)
            r = jax.nn.sigmoid(jnp.dot(xb, wa_ref[d], preferred_element_type=F32) + ba_ref[d:d + 1, :])
            ii = jax.nn.sigmoid(jnp.dot(xb, wi_ref[d], preferred_element_type=F32) + bi_ref[d:d + 1, :])
            log_a = -LRU_C * r * sp_lam
            a = jnp.exp(log_a)
            b = jnp.sqrt(-jnp.tanh(log_a) * (a * a + 1.0)) * ii * xr
            a_scr[pl.ds(base, rb), :] = a
            b_scr[pl.ds(base, rb), :] = b
            return carry

        lax.fori_loop(0, t // rb, gate_body, 0)

        def scan_body(step, carry):
            h, acc = carry
            r = step if d == 0 else rows - 1 - step
            a = a_scr[slab(r), :]
            h = a * h + b_scr[slab(r), :]
            acc = a * acc
            b_scr[slab(r), :] = h
            a_scr[slab(r), :] = acc
            return h, acc

        h_end, a_end = lax.fori_loop(
            0, rows, scan_body, (jnp.zeros((wcols, 128), F32), jnp.ones((wcols, 128), F32)))

        sh = 1
        while sh < wcols:
            if d == 0:
                valid = sub >= sh
                a_sh, h_sh = pltpu.roll(a_end, sh, axis=0), pltpu.roll(h_end, sh, axis=0)
            else:
                valid = sub < wcols - sh
                a_sh, h_sh = pltpu.roll(a_end, wcols - sh, axis=0), pltpu.roll(h_end, wcols - sh, axis=0)
            h_end = jnp.where(valid, a_end * h_sh + h_end, h_end)
            a_end = jnp.where(valid, a_end * a_sh, a_end)
            sh *= 2
        h0 = h0_ref[d:d + 1, :]
        h_full = h_end + a_end * h0
        if d == 0:
            carry_in = jnp.where(sub >= 1, pltpu.roll(h_full, 1, axis=0), h0)
            hfin_ref[0:1, :] = h_full[wcols - 1:wcols, :]
        else:
            carry_in = jnp.where(sub < wcols - 1, pltpu.roll(h_full, wcols - 1, axis=0), h0)
            hfin_ref[1:2, :] = h_full[0:1, :]

        def fix_body(r, carry):
            hv = b_scr[slab(r), :] + a_scr[slab(r), :] * carry_in
            if d == 0:
                o_ref[slab(r), :] = hv
            else:
                o_ref[slab(r), :] += hv
            return carry

        lax.fori_loop(0, rows, fix_body, 0)


def _lru_scan(x, colblk, wcols, cw, cb, wa, ba, wi, bi, lam, h0):
    t = x.shape[0]
    rows = t // wcols
    nt = LRU_W // 128
    vec = lambda r: pl.BlockSpec((r, 128), lambda j: (0, j))
    wspec = pl.BlockSpec((2, 128, 128), lambda j: (0, j, j))
    return pl.pallas_call(
        functools.partial(_lru_kernel, rows, wcols),
        grid=(nt,),
        in_specs=[pl.BlockSpec((t, 128), lambda j: (0, colblk + j)),
                  vec(SHORT_CONV), vec(1), wspec, vec(2), wspec, vec(2), vec(2), vec(2)],
        out_specs=[pl.BlockSpec((t, 128), lambda j: (0, j)), vec(2)],
        out_shape=[jax.ShapeDtypeStruct((t, LRU_W), F32), jax.ShapeDtypeStruct((2, LRU_W), F32)],
        scratch_shapes=[pltpu.VMEM((t + 4 * wcols, 128), F32), pltpu.VMEM((t, 128), F32),
                        pltpu.VMEM((t, 128), F32)],
        compiler_params=_cparams(("arbitrary",)),
        name="lru_scan",
    )(x, cw, cb, wa, ba, wi, bi, lam, h0)


def _rms_groups(x, g, width):
    parts = []
    for s in range(0, x.shape[-1], width):
        xs = x[:, s:s + width]
        parts.append(xs * lax.rsqrt(jnp.mean(xs * xs, axis=-1, keepdims=True) + EPS) * g[:, s:s + width])
    return parts


def _outproj_kernel(x_ref, g1_ref, dnf, dnb, dnz, dng, sdf, sdb, sdx, sdz, sdd, sdg,
                    lrh, lrg, mlf, mlb, mlo, mlg, w_ref, o_ref):
    acc = jnp.zeros(o_ref.shape, F32)
    z = dnz[...]
    a_parts = _rms_groups(dnf[...] + dnb[...], dng[...], DN_DV)
    for h, part in enumerate(a_parts):
        zz = z[:, h * DN_DV:(h + 1) * DN_DV]
        acc += _mm(part * _silu(zz), w_ref[h * DN_DV:(h + 1) * DN_DV, :])
    y = (sdf[...] + sdb[...] + sdd[...] * sdx[...]) * _silu(sdz[...])
    gw = GROUP_W // SSD_GROUPS
    for g, part in enumerate(_rms_groups(y, sdg[...], gw)):
        acc += _mm(part, w_ref[GROUP_W + g * gw:GROUP_W + (g + 1) * gw, :])
    acc += _mm(lrh[...] * jax.nn.gelu(lrg[...]), w_ref[2 * GROUP_W:3 * GROUP_W, :])
    o = mlo[...]
    for h, part in enumerate(_rms_groups(mlf[...] + mlb[...], mlg[...], ML_DV)):
        oo = o[:, h * ML_DV:(h + 1) * ML_DV]
        acc += _mm(part * jax.nn.sigmoid(oo), w_ref[3 * GROUP_W + h * ML_DV:3 * GROUP_W + (h + 1) * ML_DV, :])
    o_ref[...] = x_ref[...] + g1_ref[...] * acc


def _outproj(x, g1, p, dn_f, dn_b, dn_g, sd_f, sd_b, sd_d, sd_g, lr_h, ml_f, ml_b, ml_g, w):
    t = x.shape[0]
    tm = min(t, 256)
    tok = pl.BlockSpec((tm, GROUP_W), lambda i: (i, 0))
    pcol = lambda col: pl.BlockSpec((tm, GROUP_W), lambda i: (i, col // GROUP_W))
    vec = lambda n: pl.BlockSpec((1, n), lambda i: (0, 0))
    return pl.pallas_call(
        _outproj_kernel,
        grid=(t // tm,),
        in_specs=[pl.BlockSpec((tm, D_MODEL), lambda i: (i, 0)), vec(D_MODEL),
                  tok, tok, pcol(COL_DN_Z), vec(GROUP_W),
                  tok, tok, pcol(COL_SSD_XBC), pcol(COL_SSD_Z), vec(GROUP_W), vec(GROUP_W),
                  tok, pcol(COL_LRU_G),
                  tok, tok, pcol(COL_ML_O), vec(GROUP_W),
                  pl.BlockSpec((D_MODEL, D_MODEL), lambda i: (0, 0))],
        out_specs=pl.BlockSpec((tm, D_MODEL), lambda i: (i, 0)),
        out_shape=jax.ShapeDtypeStruct((t, D_MODEL), F32),
        compiler_params=_cparams(("arbitrary",)),
        name="outproj",
    )(x, g1, dn_f, dn_b, p, dn_g, sd_f, sd_b, p, p, sd_d, sd_g, lr_h, p, ml_f, ml_b, p, ml_g, w)


def _ffn_kernel(nt, nk, final, x_ref, xp_ref, xn_ref, ng_ref, sc_ref, sh_ref, g2_ref, fg_ref,
                wu_ref, wg_ref, cw_ref, wd_ref, o_ref, h_scr, g_scr, acc_scr):
    i = pl.program_id(0)
    k = pl.program_id(1)
    tm = x_ref.shape[0]

    @pl.when(k == 0)
    def _():
        _fill_halo_tile(h_scr, i, nt, x_ref, xp_ref, xn_ref, ng_ref[...], sc_ref[...], sh_ref[...])
        acc_scr[...] = jnp.zeros_like(acc_scr)

    u = jnp.dot(h_scr[HALO:HALO + tm, :], wu_ref[...], preferred_element_type=F32)
    g_scr[...] = jnp.dot(h_scr[...], wg_ref[...], preferred_element_type=F32)
    conv = (cw_ref[0:1, :] * g_scr[pl.ds(HALO - 1, tm), :]
            + cw_ref[1:2, :] * g_scr[pl.ds(HALO, tm), :]
            + cw_ref[2:3, :] * g_scr[pl.ds(HALO + 1, tm), :])
    acc_scr[...] += jnp.dot((_silu(conv) * u).astype(BF16), wd_ref[...], preferred_element_type=F32)

    @pl.when(k == nk - 1)
    def _():
        y = x_ref[...] + g2_ref[...] * acc_scr[...]
        if final:
            y = y * lax.rsqrt(jnp.mean(y * y, axis=-1, keepdims=True) + EPS) * fg_ref[...]
        o_ref[...] = y


def _ffn(x, ng, sc, sh, g2, fg, w_up, cw, w_down, final):
    t = x.shape[0]
    tm = min(t, 512)
    bk = 512
    nt, nk = t // tm, D_FF // bk
    vec = pl.BlockSpec((1, D_MODEL), lambda i, k: (0, 0))
    return pl.pallas_call(
        functools.partial(_ffn_kernel, nt, nk, final),
        grid=(nt, nk),
        in_specs=_halo_specs(t, tm, 2) + [
                  vec, vec, vec, vec, vec,
                  pl.BlockSpec((D_MODEL, bk), lambda i, k: (0, k)),
                  pl.BlockSpec((D_MODEL, bk), lambda i, k: (0, nk + k)),
                  pl.BlockSpec((FFN_CONV, bk), lambda i, k: (0, k)),
                  pl.BlockSpec((bk, D_MODEL), lambda i, k: (k, 0))],
        out_specs=pl.BlockSpec((tm, D_MODEL), lambda i, k: (i, 0)),
        out_shape=jax.ShapeDtypeStruct((t, D_MODEL), F32),
        scratch_shapes=[pltpu.VMEM((tm + 2 * HALO, D_MODEL), BF16),
                        pltpu.VMEM((tm + 2 * HALO, bk), F32),
                        pltpu.VMEM((tm, D_MODEL), F32)],
        compiler_params=_cparams(("arbitrary", "arbitrary")),
        name="ffn",
    )(x, x, x, ng, sc, sh, g2, fg, w_up, w_up, cw, w_down)


def _perm_w_in(w):
    cols = [w[:, _SRC[n][0]:_SRC[n][0] + _SRC[n][1]] for n in _DST_ORDER]
    used = sum(_SRC[n][1] for n in _DST_ORDER)
    cols.append(jnp.zeros((w.shape[0], N_PROJ - used), w.dtype))
    return jnp.concatenate(cols, axis=1).astype(BF16)


def _lane_row(pairs):
    tile = jnp.zeros((8, 128), F32)
    for row, lane, vals in pairs:
        vals = vals.reshape(-1).astype(F32)
        tile = tile.at[row, lane:lane + vals.shape[0]].set(vals)
    return tile


def _block_diag(w):
    nb, bw = w.shape[1], w.shape[2]
    eye = jnp.eye(nb, dtype=w.dtype)
    full = jnp.einsum("dnjk,nm->dnjmk", w, eye)
    return full.reshape(2, nb * bw, nb * bw).astype(BF16)


def _ctx_to_cols(a, wcols):
    t, ch = a.shape
    return a.reshape(wcols, t // wcols, ch).transpose(1, 0, 2).reshape(t, ch)


def _ctx_from_cols(a, wcols):
    t, ch = a.shape
    return a.reshape(t // wcols, wcols, ch).transpose(1, 0, 2).reshape(t, ch)


CTX_COLS = 8


def kernel(x, c, ctx, c_ctx, ada_w, ada_b, norm1_g, norm2_g, w_in, dn_conv_w, dn_a_log, dn_dt_bias, dn_norm_g, ssd_conv_w, ssd_conv_b, ssd_a_log, ssd_dt_bias, ssd_d, ssd_norm_g, lru_conv_w, lru_conv_b, lru_w_a, lru_b_a, lru_w_i, lru_b_i, lru_lambda, ml_ig_b, ml_fg_b, ml_norm_g, w_out, ffn_w_up, ffn_conv_w, ffn_w_down, final_norm_g):
    assert x.shape[0] == 1 and c.shape[0] == 1
    lat, hctx = x[0], ctx[0]
    cc = jnp.zeros((8, D_MODEL), F32).at[0].set(c[0]).at[1].set(c_ctx)
    mods = _ada(cc, ada_w, ada_b)
    row = lambda v: v.reshape(1, -1).astype(F32)

    for l in range(DEPTH):
        mod_l = mods[l, 0].reshape(N_MOD, 1, D_MODEL)
        mod_c = mods[l, 1].reshape(N_MOD, 1, D_MODEL)
        w_in_p = _perm_w_in(w_in[l])
        w_out_b = w_out[l].astype(BF16)
        w_up_b = ffn_w_up[l].astype(BF16)
        w_down_b = ffn_w_down[l].astype(BF16)
        ng1, ng2 = row(norm1_g[l]), row(norm2_g[l])

        dn_par = _lane_row([(0, LANE_DN_A, dn_a_log[l]), (1, LANE_DN_A, dn_dt_bias[l])])
        ssd_par = _lane_row([(0, LANE_SSD_DT, ssd_dt_bias[l]), (1, LANE_SSD_DT, ssd_a_log[l])])
        ml_par = _lane_row([(0, LANE_ML_I, ml_ig_b[l]), (1, LANE_ML_F, ml_fg_b[l])])
        lru_args = (lru_conv_w[l], row(lru_conv_b[l]), _block_diag(lru_w_a[l]), lru_b_a[l],
                    _block_diag(lru_w_i[l]), lru_b_i[l], lru_lambda[l])
        dn_g = row(jnp.tile(dn_norm_g[l], DN_HEADS))
        sd_d = row(jnp.repeat(ssd_d[l], SSD_HEAD_DIM))
        sd_g, ml_g = row(ssd_norm_g[l]), row(ml_norm_g[l])

        conv_w = jnp.concatenate([ssd_conv_w[l], dn_conv_w[l]], axis=1)
        conv_b = jnp.concatenate([row(ssd_conv_b[l]), jnp.zeros((1, 3 * GROUP_W), F32)], axis=1)

        pc = _inproj(hctx, ng1, mod_c[1], mod_c[0], w_in_p, conv_w, conv_b)
        dn_cf, dn_cb, dn_s = _dn_scan(pc, dn_par, jnp.zeros((2 * DN_HEADS, DN_DK, DN_DV), F32))
        sd_cf, sd_cb, sd_s = _ssd_scan(
            pc, ssd_par, jnp.zeros((2 * SSD_GROUPS, SSD_STATE, GROUP_W // SSD_GROUPS), F32))
        ml_cf, ml_cb, ml_c, ml_m = _ml_scan(
            pc, ml_par, jnp.zeros((2 * ML_HEADS, ML_DK, 2 * ML_DV), F32), jnp.zeros((1, 128), F32))
        xc_cols = _ctx_to_cols(pc[:, COL_LRU_X:COL_LRU_X + LRU_W], CTX_COLS)
        lr_c, lr_s = _lru_scan(xc_cols, 0, CTX_COLS, *lru_args, jnp.zeros((2, LRU_W), F32))

        pl_ = _inproj(lat, ng1, mod_l[1], mod_l[0], w_in_p, conv_w, conv_b)
        dn_lf, dn_lb, _ = _dn_scan(pl_, dn_par, dn_s)
        sd_lf, sd_lb, _ = _ssd_scan(pl_, ssd_par, sd_s)
        ml_lf, ml_lb, _, _ = _ml_scan(pl_, ml_par, ml_c, ml_m)
        lr_l, _ = _lru_scan(pl_, COL_LRU_X // 128, GRID_W, *lru_args, lr_s)

        lat = _outproj(lat, mod_l[2], pl_, dn_lf, dn_lb, dn_g, sd_lf, sd_lb, sd_d, sd_g,
                       lr_l, ml_lf, ml_lb, ml_g, w_out_b)
        last = l == DEPTH - 1
        lat = _ffn(lat, ng2, mod_l[4], mod_l[3], mod_l[5], row(final_norm_g), w_up_b,
                   ffn_conv_w[l], w_down_b, last)

        if not last:
            lr_cn = _ctx_from_cols(lr_c, CTX_COLS)
            hctx = _outproj(hctx, mod_c[2], pc, dn_cf, dn_cb, dn_g, sd_cf, sd_cb, sd_d, sd_g,
                            lr_cn, ml_cf, ml_cb, ml_g, w_out_b)
            hctx = _ffn(hctx, ng2, mod_c[4], mod_c[3], mod_c[5], row(final_norm_g), w_up_b,
                        ffn_conv_w[l], w_down_b, False)

    return lat[None]
```

```python
import functools

import jax
import jax.numpy as jnp
from jax import lax
from jax.experimental import pallas as pl
from jax.experimental.pallas import tpu as pltpu

F32 = jnp.float32
BF16 = jnp.bfloat16
HI = lax.Precision.HIGHEST

D_MODEL = 2048
DEPTH = 2
GRID_W = 64
GROUP_W = D_MODEL // 4
CHUNK = 64
SHORT_CONV = 5
FFN_CONV = 3
D_FF = ((8 * D_MODEL // 3 + 255) // 256) * 256
N_MOD = 6
EPS = 1e-6

DN_HEADS = 4
DN_DK = GROUP_W // DN_HEADS
DN_DV = GROUP_W // DN_HEADS
SSD_HEAD_DIM = 64
SSD_HEADS = GROUP_W // SSD_HEAD_DIM
SSD_GROUPS = 2
SSD_STATE = 128
LRU_W = GROUP_W
LRU_BLOCKS = 8
LRU_BW = LRU_W // LRU_BLOCKS
LRU_C = 8.0
ML_HEADS = 4
ML_DV = GROUP_W // ML_HEADS
ML_DK = ML_DV // 2

_SRC = {}
_off = 0
for _name, _w in (
    ("dn_q", 512), ("dn_k", 512), ("dn_v", 512), ("dn_z", 512), ("dn_a", 8), ("dn_b", 8),
    ("ssd_x", 512), ("ssd_z", 512), ("ssd_B", 256), ("ssd_C", 256), ("ssd_dt", 16),
    ("lru_x", 512), ("lru_g", 512),
    ("ml_q", 256), ("ml_k", 256), ("ml_v", 512), ("ml_o", 512), ("ml_i", 8), ("ml_f", 8),
):
    _SRC[_name] = (_off, _w)
    _off += _w
D_IN = _off

_DST_ORDER = ("ssd_x", "ssd_B", "ssd_C", "dn_q", "dn_k", "dn_v", "ml_q", "ml_k", "ml_v",
              "lru_x", "dn_z", "ssd_z", "lru_g", "ml_o", "dn_a", "dn_b", "ssd_dt", "ml_i", "ml_f")
N_PROJ = 6400
COL_SSD_XBC = 0
COL_DN_Q = 1024
COL_DN_K = 1536
COL_DN_V = 2048
N_CONV = 2560
COL_ML_QK = 2560
COL_ML_V = 3072
COL_LRU_X = 3584
COL_DN_Z = 4096
COL_SSD_Z = 4608
COL_LRU_G = 5120
COL_ML_O = 5632
COL_GATES = 6144
LANE_DN_A, LANE_DN_B, LANE_SSD_DT, LANE_ML_I, LANE_ML_F = 0, 8, 16, 32, 40

VMEM_LIMIT = 56 * 1024 * 1024


def _cparams(sem):
    return pltpu.CompilerParams(dimension_semantics=sem, vmem_limit_bytes=VMEM_LIMIT)


_NN = (((1,), (0,)), ((), ()))
_NT = (((1,), (1,)), ((), ()))
_TN = (((0,), (0,)), ((), ()))


def _dg(a, b, dims):
    return lax.dot_general(a, b, dims, preferred_element_type=F32)


def _mm(a, b, dims=_NN):
    return _dg(a.astype(BF16), b.astype(BF16), dims)


def _split2(a):
    hi = a.astype(BF16)
    return hi, (a - hi.astype(F32)).astype(BF16)


def _split3(a):
    hi = a.astype(BF16)
    r = a - hi.astype(F32)
    mid = r.astype(BF16)
    return hi, mid, (r - mid.astype(F32)).astype(BF16)


def _mm3(a, b, dims=_NN):
    ah, al = _split2(a)
    bh, bl = _split2(b)
    return _dg(ah, bh, dims) + (_dg(ah, bl, dims) + _dg(al, bh, dims))


_mm_inv = _mm


def _mm_sel(sel, x):
    sb = sel.astype(BF16)
    x0, x1, x2 = _split3(x)
    return _dg(sb, x0, _NN) + (_dg(sb, x1, _NN) + _dg(sb, x2, _NN))


def _mm_spread(x, sel):
    sb = sel.astype(BF16)
    x0, x1, x2 = _split3(x)
    return _dg(x0, sb, _NN) + (_dg(x1, sb, _NN) + _dg(x2, sb, _NN))


def _silu(x):
    return x * jax.nn.sigmoid(x)


def _masks(d):
    ri = lax.broadcasted_iota(jnp.int32, (CHUNK, CHUNK), 0)
    ci = lax.broadcasted_iota(jnp.int32, (CHUNK, CHUNK), 1)
    if d == 0:
        return ri >= ci, ri > ci
    return ri <= ci, ri < ci


def _seg_decay(col, row, incl):
    return jnp.where(incl, jnp.exp(jnp.where(incl, col - row, 0.0)), 0.0)


ADA_TK = 256


def _ada_kernel(c_ref, w_ref, b_ref, o_ref):
    k = pl.program_id(1)

    @pl.when(k == 0)
    def _():
        o_ref[...] = jnp.broadcast_to(b_ref[...], o_ref.shape)

    cc = c_ref[...]
    act = (cc * jax.nn.sigmoid(cc)).astype(BF16)
    o_ref[...] += jnp.dot(act, w_ref[...].astype(BF16), preferred_element_type=F32)


def _ada(cc, ada_w, ada_b):
    n = N_MOD * D_MODEL
    return pl.pallas_call(
        _ada_kernel,
        grid=(DEPTH, D_MODEL // ADA_TK),
        in_specs=[
            pl.BlockSpec((8, ADA_TK), lambda l, k: (0, k)),
            pl.BlockSpec((None, ADA_TK, n), lambda l, k: (l, k, 0)),
            pl.BlockSpec((None, 1, n), lambda l, k: (l, 0, 0)),
        ],
        out_specs=pl.BlockSpec((None, 8, n), lambda l, k: (l, 0, 0)),
        out_shape=jax.ShapeDtypeStruct((DEPTH, 8, n), F32),
        compiler_params=_cparams(("arbitrary", "arbitrary")),
        name="ada",
    )(cc, ada_w, ada_b.reshape(DEPTH, 1, n))


def _norm_mod(xf, ng, sc, sh):
    y = xf * lax.rsqrt(jnp.mean(xf * xf, axis=-1, keepdims=True) + EPS) * ng
    return y * (1.0 + sc) + sh


HALO = 16
PROJ_TN = 1280
PROJ_SUB = 256


def _fill_halo_tile(h_scr, i, nt, x_ref, xp_ref, xn_ref, ng, sc, sh):
    tm = x_ref.shape[0]
    hp = jnp.where(i > 0, _norm_mod(xp_ref[...], ng, sc, sh), 0.0)
    hn = jnp.where(i < nt - 1, _norm_mod(xn_ref[...], ng, sc, sh), 0.0)
    h_scr[0:HALO, :] = hp.astype(BF16)
    h_scr[HALO:HALO + tm, :] = _norm_mod(x_ref[...], ng, sc, sh).astype(BF16)
    h_scr[HALO + tm:2 * HALO + tm, :] = hn.astype(BF16)


def _inproj_kernel(nt, x_ref, xp_ref, xn_ref, ng_ref, sc_ref, sh_ref, w_ref, cw_ref, cb_ref,
                   o_ref, h_scr):
    i = pl.program_id(0)
    j = pl.program_id(1)
    tm = x_ref.shape[0]

    @pl.when(j == 0)
    def _():
        _fill_halo_tile(h_scr, i, nt, x_ref, xp_ref, xn_ref, ng_ref[...], sc_ref[...], sh_ref[...])

    for jj in range(N_CONV // PROJ_TN):
        @pl.when(j == jj)
        def _():
            rows = tm + 2 * HALO
            for c in range(PROJ_TN // PROJ_SUB):
                cs = slice(c * PROJ_SUB, (c + 1) * PROJ_SUB)
                y = jnp.dot(h_scr[...], w_ref[:, cs], preferred_element_type=F32)
                acc = cb_ref[:, cs] + cw_ref[2:3, cs] * y[HALO:HALO + tm]
                for tap in (0, 1, 3, 4):
                    shifted = pltpu.roll(y, (2 - tap) % rows, axis=0)
                    acc = acc + cw_ref[tap:tap + 1, cs] * shifted[HALO:HALO + tm]
                act = _silu(acc)
                for half in range(PROJ_SUB // 128):
                    col = jj * PROJ_TN + c * PROJ_SUB + half * 128
                    a = act[:, half * 128:(half + 1) * 128]
                    if COL_DN_Q <= col < COL_DN_V:
                        a = a * lax.rsqrt(jnp.sum(a * a, axis=-1, keepdims=True) + EPS)
                        if col < COL_DN_K:
                            a = a * DN_DK ** -0.5
                    o_ref[:, c * PROJ_SUB + half * 128:c * PROJ_SUB + (half + 1) * 128] = a

    @pl.when(j >= N_CONV // PROJ_TN)
    def _():
        o_ref[...] = jnp.dot(h_scr[HALO:HALO + tm, :], w_ref[...], preferred_element_type=F32)


def _halo_specs(t, tm, nidx):
    per, nhb = tm // HALO, t // HALO
    if nidx == 1:
        return [pl.BlockSpec((tm, D_MODEL), lambda i: (i, 0)),
                pl.BlockSpec((HALO, D_MODEL), lambda i: (jnp.maximum(i * per - 1, 0), 0)),
                pl.BlockSpec((HALO, D_MODEL), lambda i: (jnp.minimum((i + 1) * per, nhb - 1), 0))]
    return [pl.BlockSpec((tm, D_MODEL), lambda i, k: (i, 0)),
            pl.BlockSpec((HALO, D_MODEL), lambda i, k: (jnp.maximum(i * per - 1, 0), 0)),
            pl.BlockSpec((HALO, D_MODEL), lambda i, k: (jnp.minimum((i + 1) * per, nhb - 1), 0))]


def _inproj(x, ng, sc, sh, w, l, cw, cb):
    t = x.shape[0]
    tm = min(t, 1024)
    tn = PROJ_TN
    nconv = N_CONV // tn
    row = lambda i, j: (0, 0)
    return pl.pallas_call(
        functools.partial(_inproj_kernel, t // tm),
        grid=(t // tm, N_PROJ // tn),
        in_specs=_halo_specs(t, tm, 2) + [
            pl.BlockSpec((1, D_MODEL), row),
            pl.BlockSpec((1, D_MODEL), row),
            pl.BlockSpec((1, D_MODEL), row),
            pl.BlockSpec((None, D_MODEL, tn), lambda i, j: (l, 0, j)),
            pl.BlockSpec((SHORT_CONV, tn), lambda i, j: (0, jnp.minimum(j, nconv - 1))),
            pl.BlockSpec((1, tn), lambda i, j: (0, jnp.minimum(j, nconv - 1))),
        ],
        out_specs=pl.BlockSpec((tm, tn), lambda i, j: (i, j)),
        out_shape=jax.ShapeDtypeStruct((t, N_PROJ), F32),
        scratch_shapes=[pltpu.VMEM((tm + 2 * HALO, D_MODEL), BF16)],
        compiler_params=_cparams(("arbitrary", "arbitrary")),
        name="inproj",
    )(x, x, x, ng, sc, sh, w, cw, cb)


SUBS = 4
ROWS = SUBS * CHUNK


def _sub_order(d):
    return list(range(SUBS)) if d == 0 else list(range(SUBS - 1, -1, -1))


def _col_spec(nb, width, col, rev):
    blk = col // width
    assert blk * width == col
    if rev:
        return pl.BlockSpec((ROWS, width), lambda i: (nb - 1 - i, blk))
    return pl.BlockSpec((ROWS, width), lambda i: (i, blk))


def _gate_spec(nb, rev):
    return _col_spec(nb, 128, COL_GATES, rev)


def _out_spec(nb, width, rev):
    if rev:
        return pl.BlockSpec((ROWS, width), lambda i: (nb - 1 - i, 0))
    return pl.BlockSpec((ROWS, width), lambda i: (i, 0))


def _const_spec(shape):
    nd = len(shape)
    return pl.BlockSpec(shape, lambda i: (0,) * nd)


def _cumsum_dir(x, d):
    ri = lax.broadcasted_iota(jnp.int32, (ROWS, ROWS), 0)
    ci = lax.broadcasted_iota(jnp.int32, (ROWS, ROWS), 1)
    incl = (ri >= ci) if d == 0 else (ri <= ci)
    return _mm_sel(jnp.where((ri // CHUNK == ci // CHUNK) & incl, 1.0, 0.0), x)


def _dn_kernel(qf, kf, vf, qb, kb_, vb_, gf, gb, par_ref, s0_ref, of_ref, ob_ref, s_ref):
    i = pl.program_id(0)

    @pl.when(i == 0)
    def _():
        s_ref[...] = s0_ref[...]

    eye = jnp.where(_masks(0)[0] & _masks(1)[0], 1.0, 0.0).astype(F32)
    hd = []
    for d, (q_ref, k_ref, v_ref, g_ref, o_ref) in enumerate(((qf, kf, vf, gf, of_ref),
                                                             (qb, kb_, vb_, gb, ob_ref))):
        incl, strict = _masks(d)
        last = CHUNK - 1 if d == 0 else 0
        gates = g_ref[...]
        g_all = -jnp.exp(par_ref[0:1, :]) * jax.nn.softplus(gates + par_ref[1:2, :])
        beta_all = jax.nn.sigmoid(gates)
        gc = _cumsum_dir(g_all, d)
        gct = gc.T
        for pos, sub in enumerate(_sub_order(d)):
            rs = slice(sub * CHUNK, (sub + 1) * CHUNK)
            for h in range(DN_HEADS):
                lane = LANE_DN_A + d * DN_HEADS + h
                blane = LANE_DN_B + d * DN_HEADS + h
                k = k_ref[rs, h * DN_DK:(h + 1) * DN_DK]
                gcc = gc[rs, lane:lane + 1]
                beta = beta_all[rs, blane:blane + 1]
                hd.append(dict(
                    q=q_ref[rs, h * DN_DK:(h + 1) * DN_DK], k=k, kb=k * beta,
                    vb=v_ref[rs, h * DN_DV:(h + 1) * DN_DV] * beta, gcc=gcc,
                    tot=gc[sub * CHUNK + last:sub * CHUNK + last + 1, lane:lane + 1],
                    strict=strict, decay=_seg_decay(gcc, gct[lane:lane + 1, rs], incl),
                    pos=pos, rs=rs, o_ref=o_ref, osl=slice(h * DN_DV, (h + 1) * DN_DV),
                    sidx=d * DN_HEADS + h))

    nmat = [jnp.where(x["strict"], _mm(x["kb"], x["k"], _NT) * x["decay"], 0.0) for x in hd]
    attn = [_mm(x["q"], x["k"], _NT) * x["decay"] for x in hd]
    ri = lax.broadcasted_iota(jnp.int32, (CHUNK, CHUNK), 0)
    ci = lax.broadcasted_iota(jnp.int32, (CHUNK, CHUNK), 1)
    base = 8
    pw = [jnp.where(ri // base == ci // base, n, 0.0) for n in nmat]
    tinv = [eye - p for p in pw]
    for _ in range(2):
        pw = [_mm_inv(p, p) for p in pw]
        tinv = [t + _mm_inv(t, p) for t, p in zip(tinv, pw)]
    size = base
    while size < CHUNK:
        sib = (ri // (2 * size) == ci // (2 * size)) & (ri // size != ci // size)
        tc = [_mm_inv(t, jnp.where(sib, n, 0.0)) for t, n in zip(tinv, nmat)]
        tinv = [t - _mm_inv(c, t) for t, c in zip(tinv, tc)]
        size *= 2
    sol = [_mm_inv(t, jnp.concatenate([x["vb"], x["kb"] * jnp.exp(x["gcc"])], axis=-1))
           for t, x in zip(tinv, hd)]
    state = {j: s_ref[j] for j in range(2 * DN_HEADS)}
    for pos in range(SUBS):
        cur = [(x, so, a) for x, so, a in zip(hd, sol, attn) if x["pos"] == pos]
        v_new = [so[:, :DN_DV] - _mm(so[:, DN_DV:], state[x["sidx"]]) for x, so, _ in cur]
        for (x, _, a), vn in zip(cur, v_new):
            x["o_ref"][x["rs"], x["osl"]] = (_mm(x["q"] * jnp.exp(x["gcc"]), state[x["sidx"]])
                                             + _mm(a, vn))
        for (x, _, _), vn in zip(cur, v_new):
            state[x["sidx"]] = (state[x["sidx"]] * jnp.exp(x["tot"])
                                + _mm(x["k"] * jnp.exp(x["tot"] - x["gcc"]), vn, _TN))
    for j in range(2 * DN_HEADS):
        s_ref[j] = state[j]


def _dn_scan(p, par, s0):
    t = p.shape[0]
    nc = t // ROWS
    out = jax.ShapeDtypeStruct((t, GROUP_W), F32)
    qkv = lambda rev: [_col_spec(nc, GROUP_W, col, rev) for col in (COL_DN_Q, COL_DN_K, COL_DN_V)]
    return pl.pallas_call(
        _dn_kernel,
        grid=(nc,),
        in_specs=(qkv(False) + qkv(True)
                  + [_gate_spec(nc, False), _gate_spec(nc, True), _const_spec((8, 128)),
                     _const_spec((2 * DN_HEADS, DN_DK, DN_DV))]),
        out_specs=[_out_spec(nc, GROUP_W, False), _out_spec(nc, GROUP_W, True),
                   _const_spec((2 * DN_HEADS, DN_DK, DN_DV))],
        out_shape=[out, out, jax.ShapeDtypeStruct((2 * DN_HEADS, DN_DK, DN_DV), F32)],
        compiler_params=_cparams(("arbitrary",)),
        name="dn_scan",
    )(p, p, p, p, p, p, p, p, par, s0)


def _ssd_kernel(xf, xb, gf, gb, par_ref, s0_ref, of_ref, ob_ref, s_ref):
    i = pl.program_id(0)

    @pl.when(i == 0)
    def _():
        s_ref[...] = s0_ref[...]

    hp = SSD_HEADS * SSD_HEAD_DIM
    gw = hp // SSD_GROUPS
    rep = SSD_HEADS // SSD_GROUPS
    items = []
    for d, (x_ref, g_ref, o_ref) in enumerate(((xf, gf, of_ref), (xb, gb, ob_ref))):
        incl, _ = _masks(d)
        last = CHUNK - 1 if d == 0 else 0
        gates = g_ref[...]
        dt_all = jax.nn.softplus(gates + par_ref[0:1, :])
        a_all = -jnp.exp(par_ref[1:2, :]) * dt_all
        acs = _cumsum_dir(a_all, d)
        acst = acs.T
        er = lax.broadcasted_iota(jnp.int32, (128, hp), 0)
        ec = lax.broadcasted_iota(jnp.int32, (128, hp), 1)
        expand = jnp.where(er == LANE_SSD_DT + d * SSD_HEADS + ec // SSD_HEAD_DIM, 1.0, 0.0)
        both_x = _mm_spread(jnp.concatenate([dt_all, acs], axis=0), expand)
        dt_x, acs_x = both_x[:ROWS], both_x[ROWS:]
        xd_all = x_ref[:, :hp] * dt_x
        for pos, sub in enumerate(_sub_order(d)):
            rs = slice(sub * CHUNK, (sub + 1) * CHUNK)
            tot_x = acs_x[sub * CHUNK + last:sub * CHUNK + last + 1, :]
            xd = xd_all[rs]
            xdw = xd * jnp.exp(tot_x - acs_x[rs])
            for g in range(SSD_GROUPS):
                bm = x_ref[rs, hp + g * SSD_STATE:hp + (g + 1) * SSD_STATE]
                cm = x_ref[rs, hp + SSD_GROUPS * SSD_STATE + g * SSD_STATE:
                           hp + SSD_GROUPS * SSD_STATE + (g + 1) * SSD_STATE]
                cb = _mm(cm, bm, _NT)
                gs = slice(g * gw, (g + 1) * gw)
                y_diag = []
                for hh in range(rep):
                    h = g * rep + hh
                    lane = LANE_SSD_DT + d * SSD_HEADS + h
                    hs = slice(h * SSD_HEAD_DIM, (h + 1) * SSD_HEAD_DIM)
                    lmat = _seg_decay(acs[rs, lane:lane + 1], acst[lane:lane + 1, rs], incl)
                    y_diag.append(_mm(cb * lmat, xd[:, hs]))
                items.append(dict(
                    pos=pos, rs=rs, gs=gs, o_ref=o_ref, sidx=d * SSD_GROUPS + g, cm=cm,
                    y_diag=jnp.concatenate(y_diag, axis=-1), off_scale=jnp.exp(acs_x[rs, gs]),
                    dec=jnp.exp(tot_x[:, gs]), local=_mm(bm, xdw[:, gs], _TN)))
    state = {j: s_ref[j] for j in range(2 * SSD_GROUPS)}
    for pos in range(SUBS):
        for x in [x for x in items if x["pos"] == pos]:
            s = state[x["sidx"]]
            x["o_ref"][x["rs"], x["gs"]] = _mm(x["cm"], s) * x["off_scale"] + x["y_diag"]
            state[x["sidx"]] = s * x["dec"] + x["local"]
    for j in range(2 * SSD_GROUPS):
        s_ref[j] = state[j]


def _ssd_scan(p, par, s0):
    t = p.shape[0]
    nc = t // ROWS
    wx = 2 * GROUP_W
    hp = SSD_HEADS * SSD_HEAD_DIM
    st = (2 * SSD_GROUPS, SSD_STATE, hp // SSD_GROUPS)
    out = jax.ShapeDtypeStruct((t, hp), F32)
    return pl.pallas_call(
        _ssd_kernel,
        grid=(nc,),
        in_specs=[_col_spec(nc, wx, COL_SSD_XBC, False), _col_spec(nc, wx, COL_SSD_XBC, True),
                  _gate_spec(nc, False), _gate_spec(nc, True), _const_spec((8, 128)),
                  _const_spec(st)],
        out_specs=[_out_spec(nc, hp, False), _out_spec(nc, hp, True), _const_spec(st)],
        out_shape=[out, out, jax.ShapeDtypeStruct(st, F32)],
        compiler_params=_cparams(("arbitrary",)),
        name="ssd_scan",
    )(p, p, p, p, par, s0)


def _cummax_dir(x, d):
    rows = x.shape[0]
    sub = lax.broadcasted_iota(jnp.int32, x.shape, 0) % CHUNK
    sh = 1
    while sh < CHUNK:
        if d == 0:
            x = jnp.where(sub >= sh, jnp.maximum(x, pltpu.roll(x, sh, axis=0)), x)
        else:
            x = jnp.where(sub < CHUNK - sh, jnp.maximum(x, pltpu.roll(x, rows - sh, axis=0)), x)
        sh *= 2
    return x


ML_PACK = 3 * CHUNK + 8


def _ml_kernel(qkf, vf, qkb, vb, gf, gb, par_ref, c0_ref, m0_ref, of_ref, ob_ref, c_ref, m_ref):
    i = pl.program_id(0)

    @pl.when(i == 0)
    def _():
        c_ref[...] = c0_ref[...]
        m_ref[...] = m0_ref[...]

    lane = lax.broadcasted_iota(jnp.int32, (1, 128), 1)
    ones = jnp.ones((CHUNK, ML_DV), F32)
    hd, m_rows = [], []
    for d, (qk_ref, v_ref, g_ref, o_ref) in enumerate(((qkf, vf, gf, of_ref), (qkb, vb, gb, ob_ref))):
        incl, _ = _masks(d)
        last = CHUNK - 1 if d == 0 else 0
        lane0 = LANE_ML_F + d * ML_HEADS
        valid = (lane >= lane0) & (lane < lane0 + ML_HEADS)
        gates = g_ref[...]
        ig_all = pltpu.roll(gates + par_ref[0:1, :], LANE_ML_F - LANE_ML_I, axis=1)
        b_all = _cumsum_dir(jax.nn.log_sigmoid(gates + par_ref[1:2, :]), d)
        r_all = ig_all - b_all
        cmax = _cummax_dir(r_all, d)
        rt = r_all.T
        m_cur = m_ref[...]
        packed, wk_src = [], {}
        for sub in _sub_order(d):
            rs = slice(sub * CHUNK, (sub + 1) * CHUNK)
            b = b_all[rs]
            m_all = b + jnp.maximum(m_cur, cmax[rs])
            b_last = b_all[sub * CHUNK + last:sub * CHUNK + last + 1, :]
            log_g = b_last - b + ig_all[rs]
            m_new = jnp.maximum(b_last + m_cur, jnp.max(log_g, axis=0, keepdims=True))
            packed += [b - m_all, jnp.exp(b + m_cur - m_all), jnp.exp(-m_all),
                       jnp.broadcast_to(jnp.exp(b_last + m_cur - m_new), (8, 128))]
            wk_src[sub] = jnp.exp(log_g - m_new)
            m_cur = m_new
        m_rows.append(jnp.where(valid, m_cur, 0.0))
        sr = lax.broadcasted_iota(jnp.int32, (128, ML_HEADS * ML_DV), 0)
        sc = lax.broadcasted_iota(jnp.int32, (128, ML_HEADS * ML_DV), 1)
        wide = _mm_spread(jnp.where(valid, jnp.concatenate(packed, axis=0), 0.0),
                          jnp.where(sr == lane0 + sc // ML_DV, 1.0, 0.0))
        sr = lax.broadcasted_iota(jnp.int32, (128, ML_HEADS * ML_DK), 0)
        sc = lax.broadcasted_iota(jnp.int32, (128, ML_HEADS * ML_DK), 1)
        wkc = _mm_spread(
            jnp.where(valid, jnp.concatenate([wk_src[s] for s in range(SUBS)], axis=0), 0.0),
            jnp.where(sr == lane0 + sc // ML_DK, 1.0, 0.0))
        for pos, sub in enumerate(_sub_order(d)):
            rs = slice(sub * CHUNK, (sub + 1) * CHUNK)
            r0 = pos * ML_PACK
            for h in range(ML_HEADS):
                ws = slice(h * ML_DV, (h + 1) * ML_DV)
                k = qk_ref[rs, ML_HEADS * ML_DK + h * ML_DK:ML_HEADS * ML_DK + (h + 1) * ML_DK]
                hd.append(dict(
                    q=qk_ref[rs, h * ML_DK:(h + 1) * ML_DK] * ML_DK ** -0.5, k=k,
                    wk=wkc[rs, h * ML_DK:(h + 1) * ML_DK] * k,
                    v_aug=jnp.concatenate([v_ref[rs, h * ML_DV:(h + 1) * ML_DV], ones], axis=-1),
                    dmat=jnp.where(incl, jnp.exp(wide[r0:r0 + CHUNK, h * ML_DV:h * ML_DV + CHUNK]
                                                 + rt[lane0 + h:lane0 + h + 1, rs]), 0.0),
                    w_inter=wide[r0 + CHUNK:r0 + 2 * CHUNK, ws],
                    enm=wide[r0 + 2 * CHUNK:r0 + 3 * CHUNK, ws],
                    dec=wide[r0 + 3 * CHUNK:r0 + 3 * CHUNK + 1, ws],
                    pos=pos, rs=rs, sd=d * ML_HEADS + h, o_ref=o_ref, osl=ws))

    s = [_mm(x["q"], x["k"], _NT) * x["dmat"] for x in hd]
    intra = [_mm(s_h, x["v_aug"]) for s_h, x in zip(s, hd)]
    upd = [_mm(x["wk"], x["v_aug"], _TN) for x in hd]
    state = {j: c_ref[j] for j in range(2 * ML_HEADS)}
    for pos in range(SUBS):
        cur = [(x, ia, up) for x, ia, up in zip(hd, intra, upd) if x["pos"] == pos]
        inter = [_mm(x["q"], state[x["sd"]]) for x, _, _ in cur]
        for (x, ia, up), ie in zip(cur, inter):
            num = x["w_inter"] * ie[:, :ML_DV] + ia[:, :ML_DV]
            den = x["w_inter"] * ie[:, ML_DV:] + ia[:, ML_DV:]
            x["o_ref"][x["rs"], x["osl"]] = num / jnp.maximum(jnp.abs(den), x["enm"])
            state[x["sd"]] = jnp.concatenate([x["dec"], x["dec"]], axis=-1) * state[x["sd"]] + up
    for j in range(2 * ML_HEADS):
        c_ref[j] = state[j]
    m_ref[...] = m_rows[0] + m_rows[1]


def _ml_scan(p, par, c0, m0):
    t = p.shape[0]
    nc = t // ROWS
    cs, ms = (2 * ML_HEADS, ML_DK, 2 * ML_DV), (1, 128)
    out = jax.ShapeDtypeStruct((t, GROUP_W), F32)
    qkv = lambda rev: [_col_spec(nc, GROUP_W, COL_ML_QK, rev), _col_spec(nc, GROUP_W, COL_ML_V, rev)]
    return pl.pallas_call(
        _ml_kernel,
        grid=(nc,),
        in_specs=(qkv(False) + qkv(True)
                  + [_gate_spec(nc, False), _gate_spec(nc, True), _const_spec((8, 128)),
                     _const_spec(cs), _const_spec(ms)]),
        out_specs=[_out_spec(nc, GROUP_W, False), _out_spec(nc, GROUP_W, True),
                   _const_spec(cs), _const_spec(ms)],
        out_shape=[out, out, jax.ShapeDtypeStruct(cs, F32), jax.ShapeDtypeStruct(ms, F32)],
        compiler_params=_cparams(("arbitrary",)),
        name="ml_scan",
    )(p, p, p, p, p, p, par, c0, m0)


def _lru_kernel(rows, wcols, x_ref, cw_ref, cb_ref, wa_ref, ba_ref, wi_ref, bi_ref, lam_ref,
                h0_ref, o_ref, hfin_ref, xp_scr, a_scr, b_scr):
    t = rows * wcols
    sub = lax.broadcasted_iota(jnp.int32, (wcols, 128), 0)

    def shift_down(a):
        return jnp.where(sub >= 1, pltpu.roll(a, 1, axis=0), 0.0)

    def shift_up(a):
        return jnp.where(sub < wcols - 1, pltpu.roll(a, wcols - 1, axis=0), 0.0)

    def slab(r):
        return pl.ds(pl.multiple_of(r * wcols, wcols), wcols)

    xp_scr[pl.ds(2 * wcols, t), :] = x_ref[...]
    xp_scr[pl.ds(0, wcols), :] = shift_down(x_ref[pl.ds((rows - 2) * wcols, wcols), :])
    xp_scr[pl.ds(wcols, wcols), :] = shift_down(x_ref[pl.ds((rows - 1) * wcols, wcols), :])
    xp_scr[pl.ds((rows + 2) * wcols, wcols), :] = shift_up(x_ref[pl.ds(0, wcols), :])
    xp_scr[pl.ds((rows + 3) * wcols, wcols), :] = shift_up(x_ref[pl.ds(wcols, wcols), :])

    rb = 256 if t % 256 == 0 else t
    for d in range(2):
        sp_lam = jax.nn.softplus(-lam_ref[d:d + 1, :])

        def gate_body(blk, carry):
            base = pl.multiple_of(blk * rb, rb)
            xr = cb_ref[...] + cw_ref[0:1, :] * xp_scr[pl.ds(base, rb), :]
            for j in range(1, SHORT_CONV):
                xr = xr + cw_ref[j:j + 1, :] * xp_scr[pl.ds(pl.multiple_of(base + j * wcols, 8), rb), :]
            xb = xr.astype(BF16)
            r = jax.nn.sigmoid(jnp.dot(xb, wa_ref[d], preferred_element_type=F32) + ba_ref[d:d + 1, :])
            ii = jax.nn.sigmoid(jnp.dot(xb, wi_ref[d], preferred_element_type=F32) + bi_ref[d:d + 1, :])
            log_a = -LRU_C * r * sp_lam
            a = jnp.exp(log_a)
            b = jnp.sqrt(-jnp.tanh(log_a) * (a * a + 1.0)) * ii * xr
            a_scr[pl.ds(base, rb), :] = a
            b_scr[pl.ds(base, rb), :] = b
            return carry

        lax.fori_loop(0, t // rb, gate_body, 0)

        def scan_body(step, carry):
            h, acc = carry
            r = step if d == 0 else rows - 1 - step
            a = a_scr[slab(r), :]
            h = a * h + b_scr[slab(r), :]
            acc = a * acc
            b_scr[slab(r), :] = h
            a_scr[slab(r), :] = acc
            return h, acc

        h_end, a_end = lax.fori_loop(
            0, rows, scan_body, (jnp.zeros((wcols, 128), F32), jnp.ones((wcols, 128), F32)))

        sh = 1
        while sh < wcols:
            if d == 0:
                valid = sub >= sh
                a_sh, h_sh = pltpu.roll(a_end, sh, axis=0), pltpu.roll(h_end, sh, axis=0)
            else:
                valid = sub < wcols - sh
                a_sh, h_sh = pltpu.roll(a_end, wcols - sh, axis=0), pltpu.roll(h_end, wcols - sh, axis=0)
            h_end = jnp.where(valid, a_end * h_sh + h_end, h_end)
            a_end = jnp.where(valid, a_end * a_sh, a_end)
            sh *= 2
        h0 = h0_ref[d:d + 1, :]
        h_full = h_end + a_end * h0
        if d == 0:
            carry_in = jnp.where(sub >= 1, pltpu.roll(h_full, 1, axis=0), h0)
            hfin_ref[0:1, :] = h_full[wcols - 1:wcols, :]
        else:
            carry_in = jnp.where(sub < wcols - 1, pltpu.roll(h_full, wcols - 1, axis=0), h0)
            hfin_ref[1:2, :] = h_full[0:1, :]

        def fix_body(r, carry):
            hv = b_scr[slab(r), :] + a_scr[slab(r), :] * carry_in
            if d == 0:
                o_ref[slab(r), :] = hv
            else:
                o_ref[slab(r), :] += hv
            return carry

        lax.fori_loop(0, rows, fix_body, 0)


def _lru_scan(x, colblk, wcols, cw, cb, wa, ba, wi, bi, lam, h0):
    t = x.shape[0]
    rows = t // wcols
    nt = LRU_W // 128
    vec = lambda r: pl.BlockSpec((r, 128), lambda j: (0, j))
    wspec = pl.BlockSpec((2, 128, 128), lambda j: (0, j, j))
    return pl.pallas_call(
        functools.partial(_lru_kernel, rows, wcols),
        grid=(nt,),
        in_specs=[pl.BlockSpec((t, 128), lambda j: (0, colblk + j)),
                  vec(SHORT_CONV), vec(1), wspec, vec(2), wspec, vec(2), vec(2), vec(2)],
        out_specs=[pl.BlockSpec((t, 128), lambda j: (0, j)), vec(2)],
        out_shape=[jax.ShapeDtypeStruct((t, LRU_W), F32), jax.ShapeDtypeStruct((2, LRU_W), F32)],
        scratch_shapes=[pltpu.VMEM((t + 4 * wcols, 128), F32), pltpu.VMEM((t, 128), F32),
                        pltpu.VMEM((t, 128), F32)],
        compiler_params=_cparams(("arbitrary",)),
        name="lru_scan",
    )(x, cw, cb, wa, ba, wi, bi, lam, h0)


def _rms_groups(x, g, width):
    parts = []
    for s in range(0, x.shape[-1], width):
        xs = x[:, s:s + width]
        parts.append(xs * lax.rsqrt(jnp.mean(xs * xs, axis=-1, keepdims=True) + EPS) * g[:, s:s + width])
    return parts


def _outproj_kernel(x_ref, g1_ref, dnf, dnb, dnz, dng, sdf, sdb, sdx, sdz, sdd, sdg,
                    lrh, lrg, mlf, mlb, mlo, mlg, w_ref, o_ref, mix_scr):
    @pl.when(pl.program_id(1) == 0)
    def _():
        z = dnz[...]
        for h, part in enumerate(_rms_groups(dnf[...] + dnb[...], dng[...], DN_DV)):
            cs = slice(h * DN_DV, (h + 1) * DN_DV)
            mix_scr[:, cs] = (part * _silu(z[:, cs])).astype(BF16)
        y = (sdf[...] + sdb[...] + sdd[...] * sdx[...]) * _silu(sdz[...])
        gw = GROUP_W // SSD_GROUPS
        for g, part in enumerate(_rms_groups(y, sdg[...], gw)):
            mix_scr[:, GROUP_W + g * gw:GROUP_W + (g + 1) * gw] = part.astype(BF16)
        mix_scr[:, 2 * GROUP_W:3 * GROUP_W] = (lrh[...] * jax.nn.gelu(lrg[...])).astype(BF16)
        o = mlo[...]
        for h, part in enumerate(_rms_groups(mlf[...] + mlb[...], mlg[...], ML_DV)):
            cs = slice(h * ML_DV, (h + 1) * ML_DV)
            mix_scr[:, 3 * GROUP_W + h * ML_DV:3 * GROUP_W + (h + 1) * ML_DV] = (
                part * jax.nn.sigmoid(o[:, cs])).astype(BF16)

    o_ref[...] = x_ref[...] + g1_ref[...] * jnp.dot(mix_scr[...], w_ref[...],
                                                    preferred_element_type=F32)


def _outproj(x, g1, p, dn_f, dn_b, dn_g, sd_f, sd_b, sd_d, sd_g, lr_h, ml_f, ml_b, ml_g, w, l):
    t = x.shape[0]
    tm = min(t, 512)
    tn = D_MODEL // 2
    tok = pl.BlockSpec((tm, GROUP_W), lambda i, j: (i, 0))
    pcol = lambda col: pl.BlockSpec((tm, GROUP_W), lambda i, j: (i, col // GROUP_W))
    vec = lambda n: pl.BlockSpec((1, n), lambda i, j: (0, 0))
    return pl.pallas_call(
        _outproj_kernel,
        grid=(t // tm, D_MODEL // tn),
        in_specs=[pl.BlockSpec((tm, tn), lambda i, j: (i, j)),
                  pl.BlockSpec((1, tn), lambda i, j: (0, j)),
                  tok, tok, pcol(COL_DN_Z), vec(GROUP_W),
                  tok, tok, pcol(COL_SSD_XBC), pcol(COL_SSD_Z), vec(GROUP_W), vec(GROUP_W),
                  tok, pcol(COL_LRU_G),
                  tok, tok, pcol(COL_ML_O), vec(GROUP_W),
                  pl.BlockSpec((None, D_MODEL, tn), lambda i, j: (l, 0, j))],
        out_specs=pl.BlockSpec((tm, tn), lambda i, j: (i, j)),
        out_shape=jax.ShapeDtypeStruct((t, D_MODEL), F32),
        scratch_shapes=[pltpu.VMEM((tm, D_MODEL), BF16)],
        compiler_params=_cparams(("arbitrary", "arbitrary")),
        name="outproj",
    )(x, g1, dn_f, dn_b, p, dn_g, sd_f, sd_b, p, p, sd_d, sd_g, lr_h, p, ml_f, ml_b, p, ml_g, w)


def _ffn_kernel(nt, nk, final, x_ref, xp_ref, xn_ref, ng_ref, sc_ref, sh_ref, g2_ref, fg_ref,
                wu_ref, wg_ref, cw_ref, wd_ref, o_ref, h_scr, g_scr, acc_scr):
    i = pl.program_id(0)
    k = pl.program_id(1)
    tm = x_ref.shape[0]

    @pl.when(k == 0)
    def _():
        _fill_halo_tile(h_scr, i, nt, x_ref, xp_ref, xn_ref, ng_ref[...], sc_ref[...], sh_ref[...])
        acc_scr[...] = jnp.zeros_like(acc_scr)

    u = jnp.dot(h_scr[HALO:HALO + tm, :], wu_ref[...], preferred_element_type=F32)
    g_scr[...] = jnp.dot(h_scr[...], wg_ref[...], preferred_element_type=F32)
    conv = (cw_ref[0:1, :] * g_scr[pl.ds(HALO - 1, tm), :]
            + cw_ref[1:2, :] * g_scr[pl.ds(HALO, tm), :]
            + cw_ref[2:3, :] * g_scr[pl.ds(HALO + 1, tm), :])
    acc_scr[...] += jnp.dot((_silu(conv) * u).astype(BF16), wd_ref[...], preferred_element_type=F32)

    @pl.when(k == nk - 1)
    def _():
        y = x_ref[...] + g2_ref[...] * acc_scr[...]
        if final:
            y = y * lax.rsqrt(jnp.mean(y * y, axis=-1, keepdims=True) + EPS) * fg_ref[...]
        o_ref[...] = y


def _ffn(x, ng, sc, sh, g2, fg, w_up, cw, w_down, l, final):
    t = x.shape[0]
    tm = min(t, 512)
    bk = 512
    nt, nk = t // tm, D_FF // bk
    vec = pl.BlockSpec((1, D_MODEL), lambda i, k: (0, 0))
    return pl.pallas_call(
        functools.partial(_ffn_kernel, nt, nk, final),
        grid=(nt, nk),
        in_specs=_halo_specs(t, tm, 2) + [
                  vec, vec, vec, vec, vec,
                  pl.BlockSpec((None, D_MODEL, bk), lambda i, k: (l, 0, k)),
                  pl.BlockSpec((None, D_MODEL, bk), lambda i, k: (l, 0, nk + k)),
                  pl.BlockSpec((FFN_CONV, bk), lambda i, k: (0, k)),
                  pl.BlockSpec((None, bk, D_MODEL), lambda i, k: (l, k, 0))],
        out_specs=pl.BlockSpec((tm, D_MODEL), lambda i, k: (i, 0)),
        out_shape=jax.ShapeDtypeStruct((t, D_MODEL), F32),
        scratch_shapes=[pltpu.VMEM((tm + 2 * HALO, D_MODEL), BF16),
                        pltpu.VMEM((tm + 2 * HALO, bk), F32),
                        pltpu.VMEM((tm, D_MODEL), F32)],
        compiler_params=_cparams(("arbitrary", "arbitrary")),
        name="ffn",
    )(x, x, x, ng, sc, sh, g2, fg, w_up, w_up, cw, w_down)


def _perm_w_in(w):
    cols = [w[..., _SRC[n][0]:_SRC[n][0] + _SRC[n][1]].astype(BF16) for n in _DST_ORDER]
    used = sum(_SRC[n][1] for n in _DST_ORDER)
    cols.append(jnp.zeros(w.shape[:-1] + (N_PROJ - used,), BF16))
    return jnp.concatenate(cols, axis=-1)


def _lane_row(pairs):
    tile = jnp.zeros((8, 128), F32)
    for row, lane, vals in pairs:
        vals = vals.reshape(-1).astype(F32)
        tile = tile.at[row, lane:lane + vals.shape[0]].set(vals)
    return tile


def _block_diag(w):
    nb, bw = w.shape[1], w.shape[2]
    eye = jnp.eye(nb, dtype=w.dtype)
    full = jnp.einsum("dnjk,nm->dnjmk", w, eye)
    return full.reshape(2, nb * bw, nb * bw).astype(BF16)


def _ctx_to_cols(a, wcols):
    t, ch = a.shape
    return a.reshape(wcols, t // wcols, ch).transpose(1, 0, 2).reshape(t, ch)


def _ctx_from_cols(a, wcols):
    t, ch = a.shape
    return a.reshape(t // wcols, wcols, ch).transpose(1, 0, 2).reshape(t, ch)


CTX_COLS = 8


def kernel(x, c, ctx, c_ctx, ada_w, ada_b, norm1_g, norm2_g, w_in, dn_conv_w, dn_a_log, dn_dt_bias, dn_norm_g, ssd_conv_w, ssd_conv_b, ssd_a_log, ssd_dt_bias, ssd_d, ssd_norm_g, lru_conv_w, lru_conv_b, lru_w_a, lru_b_a, lru_w_i, lru_b_i, lru_lambda, ml_ig_b, ml_fg_b, ml_norm_g, w_out, ffn_w_up, ffn_conv_w, ffn_w_down, final_norm_g):
    assert x.shape[0] == 1 and c.shape[0] == 1
    lat, hctx = x[0], ctx[0]
    cc = jnp.zeros((8, D_MODEL), F32).at[0].set(c[0]).at[1].set(c_ctx)
    mods = _ada(cc, ada_w, ada_b)
    row = lambda v: v.reshape(1, -1).astype(F32)
    w_in_p = _perm_w_in(w_in)
    w_out_b = w_out.astype(BF16)
    w_up_b = ffn_w_up.astype(BF16)
    w_down_b = ffn_w_down.astype(BF16)

    for l in range(DEPTH):
        mod_l = mods[l, 0].reshape(N_MOD, 1, D_MODEL)
        mod_c = mods[l, 1].reshape(N_MOD, 1, D_MODEL)
        ng1, ng2 = row(norm1_g[l]), row(norm2_g[l])

        dn_par = _lane_row([(0, LANE_DN_A, dn_a_log[l]), (1, LANE_DN_A, dn_dt_bias[l])])
        ssd_par = _lane_row([(0, LANE_SSD_DT, ssd_dt_bias[l]), (1, LANE_SSD_DT, ssd_a_log[l])])
        ml_par = _lane_row([(0, LANE_ML_I, ml_ig_b[l]), (1, LANE_ML_F, ml_fg_b[l])])
        lru_args = (lru_conv_w[l], row(lru_conv_b[l]), _block_diag(lru_w_a[l]), lru_b_a[l],
                    _block_diag(lru_w_i[l]), lru_b_i[l], lru_lambda[l])
        dn_g = row(jnp.tile(dn_norm_g[l], DN_HEADS))
        sd_d = row(jnp.repeat(ssd_d[l], SSD_HEAD_DIM))
        sd_g, ml_g = row(ssd_norm_g[l]), row(ml_norm_g[l])

        conv_w = jnp.concatenate([ssd_conv_w[l], dn_conv_w[l]], axis=1)
        conv_b = jnp.concatenate([row(ssd_conv_b[l]), jnp.zeros((1, 3 * GROUP_W), F32)], axis=1)

        pc = _inproj(hctx, ng1, mod_c[1], mod_c[0], w_in_p, l, conv_w, conv_b)
        dn_cf, dn_cb, dn_s = _dn_scan(pc, dn_par, jnp.zeros((2 * DN_HEADS, DN_DK, DN_DV), F32))
        sd_cf, sd_cb, sd_s = _ssd_scan(
            pc, ssd_par, jnp.zeros((2 * SSD_GROUPS, SSD_STATE, GROUP_W // SSD_GROUPS), F32))
        ml_cf, ml_cb, ml_c, ml_m = _ml_scan(
            pc, ml_par, jnp.zeros((2 * ML_HEADS, ML_DK, 2 * ML_DV), F32), jnp.zeros((1, 128), F32))
        xc_cols = _ctx_to_cols(pc[:, COL_LRU_X:COL_LRU_X + LRU_W], CTX_COLS)
        lr_c, lr_s = _lru_scan(xc_cols, 0, CTX_COLS, *lru_args, jnp.zeros((2, LRU_W), F32))

        pl_ = _inproj(lat, ng1, mod_l[1], mod_l[0], w_in_p, l, conv_w, conv_b)
        dn_lf, dn_lb, _ = _dn_scan(pl_, dn_par, dn_s)
        sd_lf, sd_lb, _ = _ssd_scan(pl_, ssd_par, sd_s)
        ml_lf, ml_lb, _, _ = _ml_scan(pl_, ml_par, ml_c, ml_m)
        lr_l, _ = _lru_scan(pl_, COL_LRU_X // 128, GRID_W, *lru_args, lr_s)

        lat = _outproj(lat, mod_l[2], pl_, dn_lf, dn_lb, dn_g, sd_lf, sd_lb, sd_d, sd_g,
                       lr_l, ml_lf, ml_lb, ml_g, w_out_b, l)
        last = l == DEPTH - 1
        lat = _ffn(lat, ng2, mod_l[4], mod_l[3], mod_l[5], row(final_norm_g), w_up_b,
                   ffn_conv_w[l], w_down_b, l, last)

        if not last:
            lr_cn = _ctx_from_cols(lr_c, CTX_COLS)
            hctx = _outproj(hctx, mod_c[2], pc, dn_cf, dn_cb, dn_g, sd_cf, sd_cb, sd_d, sd_g,
                            lr_cn, ml_cf, ml_cb, ml_g, w_out_b, l)
            hctx = _ffn(hctx, ng2, mod_c[4], mod_c[3], mod_c[5], row(final_norm_g), w_up_b,
                        ffn_conv_w[l], w_down_b, l, False)

    return lat[None]
```

```python
import functools

import jax
import jax.numpy as jnp
from jax import lax
from jax.experimental import pallas as pl
from jax.experimental.pallas import tpu as pltpu

F32 = jnp.float32
BF16 = jnp.bfloat16

D_MODEL = 2048
DEPTH = 2
GRID_W = 64
GROUP_W = D_MODEL // 4
CHUNK = 64
SHORT_CONV = 5
FFN_CONV = 3
D_FF = ((8 * D_MODEL // 3 + 255) // 256) * 256
N_MOD = 6
EPS = 1e-6

DN_HEADS = 4
DN_DK = GROUP_W // DN_HEADS
DN_DV = GROUP_W // DN_HEADS
SSD_HEAD_DIM = 64
SSD_HEADS = GROUP_W // SSD_HEAD_DIM
SSD_GROUPS = 2
SSD_STATE = 128
LRU_W = GROUP_W
LRU_BLOCKS = 8
LRU_BW = LRU_W // LRU_BLOCKS
LRU_C = 8.0
ML_HEADS = 4
ML_DV = GROUP_W // ML_HEADS
ML_DK = ML_DV // 2

_SRC = {}
_off = 0
for _name, _w in (
    ("dn_q", 512), ("dn_k", 512), ("dn_v", 512), ("dn_z", 512), ("dn_a", 8), ("dn_b", 8),
    ("ssd_x", 512), ("ssd_z", 512), ("ssd_B", 256), ("ssd_C", 256), ("ssd_dt", 16),
    ("lru_x", 512), ("lru_g", 512),
    ("ml_q", 256), ("ml_k", 256), ("ml_v", 512), ("ml_o", 512), ("ml_i", 8), ("ml_f", 8),
):
    _SRC[_name] = (_off, _w)
    _off += _w
D_IN = _off

_DST_ORDER = ("ssd_x", "ssd_B", "ssd_C", "dn_q", "dn_k", "dn_v", "ml_q", "ml_k", "ml_v",
              "lru_x", "dn_z", "ssd_z", "lru_g", "ml_o", "dn_a", "dn_b", "ssd_dt", "ml_i", "ml_f")
N_PROJ = 6400
COL_SSD_XBC = 0
COL_DN_Q = 1024
COL_DN_K = 1536
COL_DN_V = 2048
N_CONV = 2560
COL_ML_QK = 2560
COL_ML_V = 3072
COL_LRU_X = 3584
COL_DN_Z = 4096
COL_SSD_Z = 4608
COL_LRU_G = 5120
COL_ML_O = 5632
COL_GATES = 6144
LANE_DN_A, LANE_DN_B, LANE_SSD_DT, LANE_ML_I, LANE_ML_F = 0, 8, 16, 32, 40

VMEM_LIMIT = 56 * 1024 * 1024


def _cparams(sem):
    return pltpu.CompilerParams(dimension_semantics=sem, vmem_limit_bytes=VMEM_LIMIT)


_NN = (((1,), (0,)), ((), ()))
_NT = (((1,), (1,)), ((), ()))
_TN = (((0,), (0,)), ((), ()))


def _dg(a, b, dims):
    return lax.dot_general(a, b, dims, preferred_element_type=F32)


def _mm(a, b, dims=_NN):
    return _dg(a.astype(BF16), b.astype(BF16), dims)


def _split3(a):
    hi = a.astype(BF16)
    r = a - hi.astype(F32)
    mid = r.astype(BF16)
    return hi, mid, (r - mid.astype(F32)).astype(BF16)


def _mm_sel(sel, x):
    sb = sel.astype(BF16)
    x0, x1, x2 = _split3(x)
    return _dg(sb, x0, _NN) + (_dg(sb, x1, _NN) + _dg(sb, x2, _NN))


def _mm_spread(x, sel):
    sb = sel.astype(BF16)
    x0, x1, x2 = _split3(x)
    return _dg(x0, sb, _NN) + (_dg(x1, sb, _NN) + _dg(x2, sb, _NN))


def _silu(x):
    return x * jax.nn.sigmoid(x)


def _masks(d):
    ri = lax.broadcasted_iota(jnp.int32, (CHUNK, CHUNK), 0)
    ci = lax.broadcasted_iota(jnp.int32, (CHUNK, CHUNK), 1)
    if d == 0:
        return ri >= ci, ri > ci
    return ri <= ci, ri < ci


def _seg_decay(col, row, incl):
    return jnp.where(incl, jnp.exp(jnp.where(incl, col - row, 0.0)), 0.0)


ADA_TK = 256


def _ada_kernel(c_ref, w_ref, b_ref, o_ref):
    k = pl.program_id(1)

    @pl.when(k == 0)
    def _():
        o_ref[...] = jnp.broadcast_to(b_ref[...], o_ref.shape)

    cc = c_ref[...]
    act = (cc * jax.nn.sigmoid(cc)).astype(BF16)
    o_ref[...] += jnp.dot(act, w_ref[...].astype(BF16), preferred_element_type=F32)


def _ada(cc, ada_w, ada_b):
    n = N_MOD * D_MODEL
    return pl.pallas_call(
        _ada_kernel,
        grid=(DEPTH, D_MODEL // ADA_TK),
        in_specs=[
            pl.BlockSpec((8, ADA_TK), lambda l, k: (0, k)),
            pl.BlockSpec((None, ADA_TK, n), lambda l, k: (l, k, 0)),
            pl.BlockSpec((None, 1, n), lambda l, k: (l, 0, 0)),
        ],
        out_specs=pl.BlockSpec((None, 8, n), lambda l, k: (l, 0, 0)),
        out_shape=jax.ShapeDtypeStruct((DEPTH, 8, n), F32),
        compiler_params=_cparams(("arbitrary", "arbitrary")),
        name="ada",
    )(cc, ada_w, ada_b.reshape(DEPTH, 1, n))


def _norm_mod(xf, ng, sc, sh):
    y = xf * lax.rsqrt(jnp.mean(xf * xf, axis=-1, keepdims=True) + EPS) * ng
    return y * (1.0 + sc) + sh


HALO = 16
PROJ_TN = 1280
PROJ_SUB = 256


def _fill_halo_tile(h_scr, i, nt, x_ref, xp_ref, xn_ref, ng, sc, sh):
    tm = x_ref.shape[0]
    hp = jnp.where(i > 0, _norm_mod(xp_ref[...], ng, sc, sh), 0.0)
    hn = jnp.where(i < nt - 1, _norm_mod(xn_ref[...], ng, sc, sh), 0.0)
    h_scr[0:HALO, :] = hp.astype(BF16)
    h_scr[HALO:HALO + tm, :] = _norm_mod(x_ref[...], ng, sc, sh).astype(BF16)
    h_scr[HALO + tm:2 * HALO + tm, :] = hn.astype(BF16)


def _inproj_kernel(nt, x_ref, xp_ref, xn_ref, ng_ref, sc_ref, sh_ref, w_ref, cw_ref, cb_ref,
                   o_ref, h_scr):
    i = pl.program_id(0)
    j = pl.program_id(1)
    tm = x_ref.shape[0]

    @pl.when(j == 0)
    def _():
        _fill_halo_tile(h_scr, i, nt, x_ref, xp_ref, xn_ref, ng_ref[...], sc_ref[...], sh_ref[...])

    for jj in range(N_CONV // PROJ_TN):
        @pl.when(j == jj)
        def _():
            rows = tm + 2 * HALO
            for c in range(PROJ_TN // PROJ_SUB):
                cs = slice(c * PROJ_SUB, (c + 1) * PROJ_SUB)
                y = jnp.dot(h_scr[...], w_ref[:, cs], preferred_element_type=F32)
                acc = cb_ref[:, cs] + cw_ref[2:3, cs] * y[HALO:HALO + tm]
                for tap in (0, 1, 3, 4):
                    shifted = pltpu.roll(y, (2 - tap) % rows, axis=0)
                    acc = acc + cw_ref[tap:tap + 1, cs] * shifted[HALO:HALO + tm]
                act = _silu(acc)
                for half in range(PROJ_SUB // 128):
                    col = jj * PROJ_TN + c * PROJ_SUB + half * 128
                    a = act[:, half * 128:(half + 1) * 128]
                    if COL_DN_Q <= col < COL_DN_V:
                        a = a * lax.rsqrt(jnp.sum(a * a, axis=-1, keepdims=True) + EPS)
                        if col < COL_DN_K:
                            a = a * DN_DK ** -0.5
                    o_ref[:, c * PROJ_SUB + half * 128:c * PROJ_SUB + (half + 1) * 128] = a

    @pl.when(j >= N_CONV // PROJ_TN)
    def _():
        o_ref[...] = jnp.dot(h_scr[HALO:HALO + tm, :], w_ref[...], preferred_element_type=F32)


def _halo_specs(t, tm):
    per, nhb = tm // HALO, t // HALO
    return [pl.BlockSpec((tm, D_MODEL), lambda i, k: (i, 0)),
            pl.BlockSpec((HALO, D_MODEL), lambda i, k: (jnp.maximum(i * per - 1, 0), 0)),
            pl.BlockSpec((HALO, D_MODEL), lambda i, k: (jnp.minimum((i + 1) * per, nhb - 1), 0))]


def _inproj(x, ng, sc, sh, w, l, cw, cb):
    t = x.shape[0]
    tm = min(t, 1024)
    tn = PROJ_TN
    nconv = N_CONV // tn
    row = lambda i, j: (0, 0)
    return pl.pallas_call(
        functools.partial(_inproj_kernel, t // tm),
        grid=(t // tm, N_PROJ // tn),
        in_specs=_halo_specs(t, tm) + [
            pl.BlockSpec((1, D_MODEL), row),
            pl.BlockSpec((1, D_MODEL), row),
            pl.BlockSpec((1, D_MODEL), row),
            pl.BlockSpec((None, D_MODEL, tn), lambda i, j: (l, 0, j)),
            pl.BlockSpec((SHORT_CONV, tn), lambda i, j: (0, jnp.minimum(j, nconv - 1))),
            pl.BlockSpec((1, tn), lambda i, j: (0, jnp.minimum(j, nconv - 1))),
        ],
        out_specs=pl.BlockSpec((tm, tn), lambda i, j: (i, j)),
        out_shape=jax.ShapeDtypeStruct((t, N_PROJ), F32),
        scratch_shapes=[pltpu.VMEM((tm + 2 * HALO, D_MODEL), BF16)],
        compiler_params=_cparams(("arbitrary", "arbitrary")),
        name="inproj",
    )(x, x, x, ng, sc, sh, w, cw, cb)


SUBS = 4
ROWS = SUBS * CHUNK


def _sub_order(d):
    return list(range(SUBS)) if d == 0 else list(range(SUBS - 1, -1, -1))


def _col_spec(nb, width, col, rev):
    blk = col // width
    assert blk * width == col
    if rev:
        return pl.BlockSpec((ROWS, width), lambda i: (nb - 1 - i, blk))
    return pl.BlockSpec((ROWS, width), lambda i: (i, blk))


def _gate_spec(nb, rev):
    return _col_spec(nb, 128, COL_GATES, rev)


def _out_spec(nb, width, rev):
    if rev:
        return pl.BlockSpec((ROWS, width), lambda i: (nb - 1 - i, 0))
    return pl.BlockSpec((ROWS, width), lambda i: (i, 0))


def _const_spec(shape):
    nd = len(shape)
    return pl.BlockSpec(shape, lambda i: (0,) * nd)


def _cumsum_dir(x, d):
    ri = lax.broadcasted_iota(jnp.int32, (ROWS, ROWS), 0)
    ci = lax.broadcasted_iota(jnp.int32, (ROWS, ROWS), 1)
    incl = (ri >= ci) if d == 0 else (ri <= ci)
    return _mm_sel(jnp.where((ri // CHUNK == ci // CHUNK) & incl, 1.0, 0.0), x)


def _dn_kernel(qf, kf, vf, qb, kb_, vb_, gf, gb, par_ref, s0_ref, of_ref, ob_ref, s_ref):
    i = pl.program_id(0)

    @pl.when(i == 0)
    def _():
        s_ref[...] = s0_ref[...]

    eye = jnp.where(_masks(0)[0] & _masks(1)[0], 1.0, 0.0).astype(F32)
    hd = []
    for d, (q_ref, k_ref, v_ref, g_ref, o_ref) in enumerate(((qf, kf, vf, gf, of_ref),
                                                             (qb, kb_, vb_, gb, ob_ref))):
        incl, strict = _masks(d)
        last = CHUNK - 1 if d == 0 else 0
        gates = g_ref[...]
        g_all = -jnp.exp(par_ref[0:1, :]) * jax.nn.softplus(gates + par_ref[1:2, :])
        beta_all = jax.nn.sigmoid(gates)
        gc = _cumsum_dir(g_all, d)
        gct = gc.T
        for pos, sub in enumerate(_sub_order(d)):
            rs = slice(sub * CHUNK, (sub + 1) * CHUNK)
            for h in range(DN_HEADS):
                lane = LANE_DN_A + d * DN_HEADS + h
                blane = LANE_DN_B + d * DN_HEADS + h
                k = k_ref[rs, h * DN_DK:(h + 1) * DN_DK]
                gcc = gc[rs, lane:lane + 1]
                beta = beta_all[rs, blane:blane + 1]
                hd.append(dict(
                    q=q_ref[rs, h * DN_DK:(h + 1) * DN_DK], k=k, kb=k * beta,
                    vb=v_ref[rs, h * DN_DV:(h + 1) * DN_DV] * beta, gcc=gcc,
                    tot=gc[sub * CHUNK + last:sub * CHUNK + last + 1, lane:lane + 1],
                    strict=strict, decay=_seg_decay(gcc, gct[lane:lane + 1, rs], incl),
                    pos=pos, rs=rs, o_ref=o_ref, osl=slice(h * DN_DV, (h + 1) * DN_DV),
                    sidx=d * DN_HEADS + h))

    nmat = [jnp.where(x["strict"], _mm(x["kb"], x["k"], _NT) * x["decay"], 0.0) for x in hd]
    attn = [_mm(x["q"], x["k"], _NT) * x["decay"] for x in hd]
    ri = lax.broadcasted_iota(jnp.int32, (CHUNK, CHUNK), 0)
    ci = lax.broadcasted_iota(jnp.int32, (CHUNK, CHUNK), 1)
    base = 8
    pw = [jnp.where(ri // base == ci // base, n, 0.0) for n in nmat]
    tinv = [eye - p for p in pw]
    for _ in range(2):
        pw = [_mm(p, p) for p in pw]
        tinv = [t + _mm(t, p) for t, p in zip(tinv, pw)]
    size = base
    while size < CHUNK:
        sib = (ri // (2 * size) == ci // (2 * size)) & (ri // size != ci // size)
        tc = [_mm(t, jnp.where(sib, n, 0.0)) for t, n in zip(tinv, nmat)]
        tinv = [t - _mm(c, t) for t, c in zip(tinv, tc)]
        size *= 2
    sol = [_mm(t, jnp.concatenate([x["vb"], x["kb"] * jnp.exp(x["gcc"])], axis=-1))
           for t, x in zip(tinv, hd)]
    state = {j: s_ref[j] for j in range(2 * DN_HEADS)}
    for pos in range(SUBS):
        cur = [(x, so, a) for x, so, a in zip(hd, sol, attn) if x["pos"] == pos]
        v_new = [so[:, :DN_DV] - _mm(so[:, DN_DV:], state[x["sidx"]]) for x, so, _ in cur]
        for (x, _, a), vn in zip(cur, v_new):
            x["o_ref"][x["rs"], x["osl"]] = (_mm(x["q"] * jnp.exp(x["gcc"]), state[x["sidx"]])
                                             + _mm(a, vn))
        for (x, _, _), vn in zip(cur, v_new):
            state[x["sidx"]] = (state[x["sidx"]] * jnp.exp(x["tot"])
                                + _mm(x["k"] * jnp.exp(x["tot"] - x["gcc"]), vn, _TN))
    for j in range(2 * DN_HEADS):
        s_ref[j] = state[j]


def _dn_scan(p, par, s0):
    t = p.shape[0]
    nb = t // ROWS
    out = jax.ShapeDtypeStruct((t, GROUP_W), F32)
    qkv = lambda rev: [_col_spec(nb, GROUP_W, col, rev) for col in (COL_DN_Q, COL_DN_K, COL_DN_V)]
    return pl.pallas_call(
        _dn_kernel,
        grid=(nb,),
        in_specs=(qkv(False) + qkv(True)
                  + [_gate_spec(nb, False), _gate_spec(nb, True), _const_spec((8, 128)),
                     _const_spec((2 * DN_HEADS, DN_DK, DN_DV))]),
        out_specs=[_out_spec(nb, GROUP_W, False), _out_spec(nb, GROUP_W, True),
                   _const_spec((2 * DN_HEADS, DN_DK, DN_DV))],
        out_shape=[out, out, jax.ShapeDtypeStruct((2 * DN_HEADS, DN_DK, DN_DV), F32)],
        compiler_params=_cparams(("arbitrary",)),
        name="dn_scan",
    )(p, p, p, p, p, p, p, p, par, s0)


def _ssd_kernel(xf, xb, gf, gb, par_ref, s0_ref, of_ref, ob_ref, s_ref):
    i = pl.program_id(0)

    @pl.when(i == 0)
    def _():
        s_ref[...] = s0_ref[...]

    hp = SSD_HEADS * SSD_HEAD_DIM
    gw = hp // SSD_GROUPS
    rep = SSD_HEADS // SSD_GROUPS
    items = []
    for d, (x_ref, g_ref, o_ref) in enumerate(((xf, gf, of_ref), (xb, gb, ob_ref))):
        incl, _ = _masks(d)
        last = CHUNK - 1 if d == 0 else 0
        gates = g_ref[...]
        dt_all = jax.nn.softplus(gates + par_ref[0:1, :])
        a_all = -jnp.exp(par_ref[1:2, :]) * dt_all
        acs = _cumsum_dir(a_all, d)
        acst = acs.T
        er = lax.broadcasted_iota(jnp.int32, (128, hp), 0)
        ec = lax.broadcasted_iota(jnp.int32, (128, hp), 1)
        expand = jnp.where(er == LANE_SSD_DT + d * SSD_HEADS + ec // SSD_HEAD_DIM, 1.0, 0.0)
        both_x = _mm_spread(jnp.concatenate([dt_all, acs], axis=0), expand)
        dt_x, acs_x = both_x[:ROWS], both_x[ROWS:]
        xd_all = x_ref[:, :hp] * dt_x
        for pos, sub in enumerate(_sub_order(d)):
            rs = slice(sub * CHUNK, (sub + 1) * CHUNK)
            tot_x = acs_x[sub * CHUNK + last:sub * CHUNK + last + 1, :]
            xd = xd_all[rs]
            xdw = xd * jnp.exp(tot_x - acs_x[rs])
            for g in range(SSD_GROUPS):
                bm = x_ref[rs, hp + g * SSD_STATE:hp + (g + 1) * SSD_STATE]
                cm = x_ref[rs, hp + SSD_GROUPS * SSD_STATE + g * SSD_STATE:
                           hp + SSD_GROUPS * SSD_STATE + (g + 1) * SSD_STATE]
                cb = _mm(cm, bm, _NT)
                gs = slice(g * gw, (g + 1) * gw)
                y_diag = []
                for hh in range(rep):
                    h = g * rep + hh
                    lane = LANE_SSD_DT + d * SSD_HEADS + h
                    hs = slice(h * SSD_HEAD_DIM, (h + 1) * SSD_HEAD_DIM)
                    lmat = _seg_decay(acs[rs, lane:lane + 1], acst[lane:lane + 1, rs], incl)
                    y_diag.append(_mm(cb * lmat, xd[:, hs]))
                items.append(dict(
                    pos=pos, rs=rs, gs=gs, o_ref=o_ref, sidx=d * SSD_GROUPS + g, cm=cm,
                    y_diag=jnp.concatenate(y_diag, axis=-1), off_scale=jnp.exp(acs_x[rs, gs]),
                    dec=jnp.exp(tot_x[:, gs]), local=_mm(bm, xdw[:, gs], _TN)))
    state = {j: s_ref[j] for j in range(2 * SSD_GROUPS)}
    for pos in range(SUBS):
        for x in [x for x in items if x["pos"] == pos]:
            s = state[x["sidx"]]
            x["o_ref"][x["rs"], x["gs"]] = _mm(x["cm"], s) * x["off_scale"] + x["y_diag"]
            state[x["sidx"]] = s * x["dec"] + x["local"]
    for j in range(2 * SSD_GROUPS):
        s_ref[j] = state[j]


def _ssd_scan(p, par, s0):
    t = p.shape[0]
    nb = t // ROWS
    wx = 2 * GROUP_W
    hp = SSD_HEADS * SSD_HEAD_DIM
    st = (2 * SSD_GROUPS, SSD_STATE, hp // SSD_GROUPS)
    out = jax.ShapeDtypeStruct((t, hp), F32)
    return pl.pallas_call(
        _ssd_kernel,
        grid=(nb,),
        in_specs=[_col_spec(nb, wx, COL_SSD_XBC, False), _col_spec(nb, wx, COL_SSD_XBC, True),
                  _gate_spec(nb, False), _gate_spec(nb, True), _const_spec((8, 128)),
                  _const_spec(st)],
        out_specs=[_out_spec(nb, hp, False), _out_spec(nb, hp, True), _const_spec(st)],
        out_shape=[out, out, jax.ShapeDtypeStruct(st, F32)],
        compiler_params=_cparams(("arbitrary",)),
        name="ssd_scan",
    )(p, p, p, p, par, s0)


def _cummax_dir(x, d):
    rows = x.shape[0]
    sub = lax.broadcasted_iota(jnp.int32, x.shape, 0) % CHUNK
    sh = 1
    while sh < CHUNK:
        if d == 0:
            x = jnp.where(sub >= sh, jnp.maximum(x, pltpu.roll(x, sh, axis=0)), x)
        else:
            x = jnp.where(sub < CHUNK - sh, jnp.maximum(x, pltpu.roll(x, rows - sh, axis=0)), x)
        sh *= 2
    return x


ML_PACK = 3 * CHUNK + 8


def _ml_kernel(qkf, vf, qkb, vb, gf, gb, par_ref, c0_ref, m0_ref, of_ref, ob_ref, c_ref, m_ref):
    i = pl.program_id(0)

    @pl.when(i == 0)
    def _():
        c_ref[...] = c0_ref[...]
        m_ref[...] = m0_ref[...]

    lane = lax.broadcasted_iota(jnp.int32, (1, 128), 1)
    ones = jnp.ones((CHUNK, ML_DV), F32)
    hd, m_rows = [], []
    for d, (qk_ref, v_ref, g_ref, o_ref) in enumerate(((qkf, vf, gf, of_ref), (qkb, vb, gb, ob_ref))):
        incl, _ = _masks(d)
        last = CHUNK - 1 if d == 0 else 0
        lane0 = LANE_ML_F + d * ML_HEADS
        valid = (lane >= lane0) & (lane < lane0 + ML_HEADS)
        gates = g_ref[...]
        ig_all = pltpu.roll(gates + par_ref[0:1, :], LANE_ML_F - LANE_ML_I, axis=1)
        b_all = _cumsum_dir(jax.nn.log_sigmoid(gates + par_ref[1:2, :]), d)
        r_all = ig_all - b_all
        cmax = _cummax_dir(r_all, d)
        rt = r_all.T
        m_cur = m_ref[...]
        packed, wk_src = [], {}
        for sub in _sub_order(d):
            rs = slice(sub * CHUNK, (sub + 1) * CHUNK)
            b = b_all[rs]
            m_all = b + jnp.maximum(m_cur, cmax[rs])
            b_last = b_all[sub * CHUNK + last:sub * CHUNK + last + 1, :]
            log_g = b_last - b + ig_all[rs]
            m_new = jnp.maximum(b_last + m_cur, jnp.max(log_g, axis=0, keepdims=True))
            packed += [b - m_all, jnp.exp(b + m_cur - m_all), jnp.exp(-m_all),
                       jnp.broadcast_to(jnp.exp(b_last + m_cur - m_new), (8, 128))]
            wk_src[sub] = jnp.exp(log_g - m_new)
            m_cur = m_new
        m_rows.append(jnp.where(valid, m_cur, 0.0))
        sr = lax.broadcasted_iota(jnp.int32, (128, ML_HEADS * ML_DV), 0)
        sc = lax.broadcasted_iota(jnp.int32, (128, ML_HEADS * ML_DV), 1)
        wide = _mm_spread(jnp.where(valid, jnp.concatenate(packed, axis=0), 0.0),
                          jnp.where(sr == lane0 + sc // ML_DV, 1.0, 0.0))
        sr = lax.broadcasted_iota(jnp.int32, (128, ML_HEADS * ML_DK), 0)
        sc = lax.broadcasted_iota(jnp.int32, (128, ML_HEADS * ML_DK), 1)
        wkc = _mm_spread(
            jnp.where(valid, jnp.concatenate([wk_src[s] for s in range(SUBS)], axis=0), 0.0),
            jnp.where(sr == lane0 + sc // ML_DK, 1.0, 0.0))
        for pos, sub in enumerate(_sub_order(d)):
            rs = slice(sub * CHUNK, (sub + 1) * CHUNK)
            r0 = pos * ML_PACK
            for h in range(ML_HEADS):
                ws = slice(h * ML_DV, (h + 1) * ML_DV)
                k = qk_ref[rs, ML_HEADS * ML_DK + h * ML_DK:ML_HEADS * ML_DK + (h + 1) * ML_DK]
                hd.append(dict(
                    q=qk_ref[rs, h * ML_DK:(h + 1) * ML_DK] * ML_DK ** -0.5, k=k,
                    wk=wkc[rs, h * ML_DK:(h + 1) * ML_DK] * k,
                    v_aug=jnp.concatenate([v_ref[rs, h * ML_DV:(h + 1) * ML_DV], ones], axis=-1),
                    dmat=jnp.where(incl, jnp.exp(wide[r0:r0 + CHUNK, h * ML_DV:h * ML_DV + CHUNK]
                                                 + rt[lane0 + h:lane0 + h + 1, rs]), 0.0),
                    w_inter=wide[r0 + CHUNK:r0 + 2 * CHUNK, ws],
                    enm=wide[r0 + 2 * CHUNK:r0 + 3 * CHUNK, ws],
                    dec=wide[r0 + 3 * CHUNK:r0 + 3 * CHUNK + 1, ws],
                    pos=pos, rs=rs, sd=d * ML_HEADS + h, o_ref=o_ref, osl=ws))

    s = [_mm(x["q"], x["k"], _NT) * x["dmat"] for x in hd]
    intra = [_mm(s_h, x["v_aug"]) for s_h, x in zip(s, hd)]
    upd = [_mm(x["wk"], x["v_aug"], _TN) for x in hd]
    state = {j: c_ref[j] for j in range(2 * ML_HEADS)}
    for pos in range(SUBS):
        cur = [(x, ia, up) for x, ia, up in zip(hd, intra, upd) if x["pos"] == pos]
        inter = [_mm(x["q"], state[x["sd"]]) for x, _, _ in cur]
        for (x, ia, up), ie in zip(cur, inter):
            num = x["w_inter"] * ie[:, :ML_DV] + ia[:, :ML_DV]
            den = x["w_inter"] * ie[:, ML_DV:] + ia[:, ML_DV:]
            x["o_ref"][x["rs"], x["osl"]] = num / jnp.maximum(jnp.abs(den), x["enm"])
            state[x["sd"]] = jnp.concatenate([x["dec"], x["dec"]], axis=-1) * state[x["sd"]] + up
    for j in range(2 * ML_HEADS):
        c_ref[j] = state[j]
    m_ref[...] = m_rows[0] + m_rows[1]


def _ml_scan(p, par, c0, m0):
    t = p.shape[0]
    nb = t // ROWS
    cs, ms = (2 * ML_HEADS, ML_DK, 2 * ML_DV), (1, 128)
    out = jax.ShapeDtypeStruct((t, GROUP_W), F32)
    qkv = lambda rev: [_col_spec(nb, GROUP_W, COL_ML_QK, rev), _col_spec(nb, GROUP_W, COL_ML_V, rev)]
    return pl.pallas_call(
        _ml_kernel,
        grid=(nb,),
        in_specs=(qkv(False) + qkv(True)
                  + [_gate_spec(nb, False), _gate_spec(nb, True), _const_spec((8, 128)),
                     _const_spec(cs), _const_spec(ms)]),
        out_specs=[_out_spec(nb, GROUP_W, False), _out_spec(nb, GROUP_W, True),
                   _const_spec(cs), _const_spec(ms)],
        out_shape=[out, out, jax.ShapeDtypeStruct(cs, F32), jax.ShapeDtypeStruct(ms, F32)],
        compiler_params=_cparams(("arbitrary",)),
        name="ml_scan",
    )(p, p, p, p, p, p, par, c0, m0)


def _lru_kernel(rows, wcols, x_ref, cw_ref, cb_ref, wa_ref, ba_ref, wi_ref, bi_ref, lam_ref,
                h0_ref, o_ref, hfin_ref, xp_scr, a_scr, b_scr):
    t = rows * wcols
    sub = lax.broadcasted_iota(jnp.int32, (wcols, 128), 0)

    def shift_down(a):
        return jnp.where(sub >= 1, pltpu.roll(a, 1, axis=0), 0.0)

    def shift_up(a):
        return jnp.where(sub < wcols - 1, pltpu.roll(a, wcols - 1, axis=0), 0.0)

    def slab(r):
        return pl.ds(pl.multiple_of(r * wcols, wcols), wcols)

    xp_scr[pl.ds(2 * wcols, t), :] = x_ref[...]
    xp_scr[pl.ds(0, wcols), :] = shift_down(x_ref[pl.ds((rows - 2) * wcols, wcols), :])
    xp_scr[pl.ds(wcols, wcols), :] = shift_down(x_ref[pl.ds((rows - 1) * wcols, wcols), :])
    xp_scr[pl.ds((rows + 2) * wcols, wcols), :] = shift_up(x_ref[pl.ds(0, wcols), :])
    xp_scr[pl.ds((rows + 3) * wcols, wcols), :] = shift_up(x_ref[pl.ds(wcols, wcols), :])

    rb = 256 if t % 256 == 0 else t
    for d in range(2):
        sp_lam = jax.nn.softplus(-lam_ref[d:d + 1, :])

        def gate_body(blk, carry):
            base = pl.multiple_of(blk * rb, rb)
            xr = cb_ref[...] + cw_ref[0:1, :] * xp_scr[pl.ds(base, rb), :]
            for j in range(1, SHORT_CONV):
                xr = xr + cw_ref[j:j + 1, :] * xp_scr[pl.ds(pl.multiple_of(base + j * wcols, 8), rb), :]
            xb = xr.astype(BF16)
            r = jax.nn.sigmoid(jnp.dot(xb, wa_ref[d], preferred_element_type=F32) + ba_ref[d:d + 1, :])
            ii = jax.nn.sigmoid(jnp.dot(xb, wi_ref[d], preferred_element_type=F32) + bi_ref[d:d + 1, :])
            log_a = -LRU_C * r * sp_lam
            a = jnp.exp(log_a)
            b = jnp.sqrt(-jnp.tanh(log_a) * (a * a + 1.0)) * ii * xr
            a_scr[pl.ds(base, rb), :] = a
            b_scr[pl.ds(base, rb), :] = b
            return carry

        lax.fori_loop(0, t // rb, gate_body, 0)

        def scan_body(step, carry):
            h, acc = carry
            r = step if d == 0 else rows - 1 - step
            a = a_scr[slab(r), :]
            h = a * h + b_scr[slab(r), :]
            acc = a * acc
            b_scr[slab(r), :] = h
            a_scr[slab(r), :] = acc
            return h, acc

        h_end, a_end = lax.fori_loop(
            0, rows, scan_body, (jnp.zeros((wcols, 128), F32), jnp.ones((wcols, 128), F32)))

        sh = 1
        while sh < wcols:
            if d == 0:
                valid = sub >= sh
                a_sh, h_sh = pltpu.roll(a_end, sh, axis=0), pltpu.roll(h_end, sh, axis=0)
            else:
                valid = sub < wcols - sh
                a_sh, h_sh = pltpu.roll(a_end, wcols - sh, axis=0), pltpu.roll(h_end, wcols - sh, axis=0)
            h_end = jnp.where(valid, a_end * h_sh + h_end, h_end)
            a_end = jnp.where(valid, a_end * a_sh, a_end)
            sh *= 2
        h0 = h0_ref[d:d + 1, :]
        h_full = h_end + a_end * h0
        if d == 0:
            carry_in = jnp.where(sub >= 1, pltpu.roll(h_full, 1, axis=0), h0)
            hfin_ref[0:1, :] = h_full[wcols - 1:wcols, :]
        else:
            carry_in = jnp.where(sub < wcols - 1, pltpu.roll(h_full, wcols - 1, axis=0), h0)
            hfin_ref[1:2, :] = h_full[0:1, :]

        def fix_body(r, carry):
            hv = b_scr[slab(r), :] + a_scr[slab(r), :] * carry_in
            if d == 0:
                o_ref[slab(r), :] = hv
            else:
                o_ref[slab(r), :] += hv
            return carry

        lax.fori_loop(0, rows, fix_body, 0)


def _lru_scan(x, colblk, wcols, cw, cb, wa, ba, wi, bi, lam, h0):
    t = x.shape[0]
    rows = t // wcols
    nt = LRU_W // 128
    vec = lambda r: pl.BlockSpec((r, 128), lambda j: (0, j))
    wspec = pl.BlockSpec((2, 128, 128), lambda j: (0, j, j))
    return pl.pallas_call(
        functools.partial(_lru_kernel, rows, wcols),
        grid=(nt,),
        in_specs=[pl.BlockSpec((t, 128), lambda j: (0, colblk + j)),
                  vec(SHORT_CONV), vec(1), wspec, vec(2), wspec, vec(2), vec(2), vec(2)],
        out_specs=[pl.BlockSpec((t, 128), lambda j: (0, j)), vec(2)],
        out_shape=[jax.ShapeDtypeStruct((t, LRU_W), F32), jax.ShapeDtypeStruct((2, LRU_W), F32)],
        scratch_shapes=[pltpu.VMEM((t + 4 * wcols, 128), F32), pltpu.VMEM((t, 128), F32),
                        pltpu.VMEM((t, 128), F32)],
        compiler_params=_cparams(("arbitrary",)),
        name="lru_scan",
    )(x, cw, cb, wa, ba, wi, bi, lam, h0)


def _rms_groups(x, g, width):
    parts = []
    for s in range(0, x.shape[-1], width):
        xs = x[:, s:s + width]
        parts.append(xs * lax.rsqrt(jnp.mean(xs * xs, axis=-1, keepdims=True) + EPS) * g[:, s:s + width])
    return parts


def _outproj_kernel(x_ref, g1_ref, dnf, dnb, dnz, dng, sdf, sdb, sdx, sdz, sdd, sdg,
                    lrh, lrg, mlf, mlb, mlo, mlg, w_ref, o_ref):
    acc = jnp.zeros(o_ref.shape, F32)
    z = dnz[...]
    for h, part in enumerate(_rms_groups(dnf[...] + dnb[...], dng[...], DN_DV)):
        cs = slice(h * DN_DV, (h + 1) * DN_DV)
        acc += _mm(part * _silu(z[:, cs]), w_ref[cs, :])
    y = (sdf[...] + sdb[...] + sdd[...] * sdx[...]) * _silu(sdz[...])
    gw = GROUP_W // SSD_GROUPS
    for g, part in enumerate(_rms_groups(y, sdg[...], gw)):
        acc += _mm(part, w_ref[GROUP_W + g * gw:GROUP_W + (g + 1) * gw, :])
    acc += _mm(lrh[...] * jax.nn.gelu(lrg[...]), w_ref[2 * GROUP_W:3 * GROUP_W, :])
    o = mlo[...]
    for h, part in enumerate(_rms_groups(mlf[...] + mlb[...], mlg[...], ML_DV)):
        cs = slice(h * ML_DV, (h + 1) * ML_DV)
        acc += _mm(part * jax.nn.sigmoid(o[:, cs]),
                   w_ref[3 * GROUP_W + h * ML_DV:3 * GROUP_W + (h + 1) * ML_DV, :])
    o_ref[...] = x_ref[...] + g1_ref[...] * acc


def _outproj(x, g1, p, dn_f, dn_b, dn_g, sd_f, sd_b, sd_d, sd_g, lr_h, ml_f, ml_b, ml_g, w, l):
    t = x.shape[0]
    tm = min(t, 256)
    tok = pl.BlockSpec((tm, GROUP_W), lambda i: (i, 0))
    pcol = lambda col: pl.BlockSpec((tm, GROUP_W), lambda i: (i, col // GROUP_W))
    vec = lambda n: pl.BlockSpec((1, n), lambda i: (0, 0))
    return pl.pallas_call(
        _outproj_kernel,
        grid=(t // tm,),
        in_specs=[pl.BlockSpec((tm, D_MODEL), lambda i: (i, 0)), vec(D_MODEL),
                  tok, tok, pcol(COL_DN_Z), vec(GROUP_W),
                  tok, tok, pcol(COL_SSD_XBC), pcol(COL_SSD_Z), vec(GROUP_W), vec(GROUP_W),
                  tok, pcol(COL_LRU_G),
                  tok, tok, pcol(COL_ML_O), vec(GROUP_W),
                  pl.BlockSpec((None, D_MODEL, D_MODEL), lambda i: (l, 0, 0))],
        out_specs=pl.BlockSpec((tm, D_MODEL), lambda i: (i, 0)),
        out_shape=jax.ShapeDtypeStruct((t, D_MODEL), F32),
        compiler_params=_cparams(("arbitrary",)),
        name="outproj",
    )(x, g1, dn_f, dn_b, p, dn_g, sd_f, sd_b, p, p, sd_d, sd_g, lr_h, p, ml_f, ml_b, p, ml_g, w)


def _ffn_kernel(nt, nk, final, x_ref, xp_ref, xn_ref, ng_ref, sc_ref, sh_ref, g2_ref, fg_ref,
                wu_ref, wg_ref, cw_ref, wd_ref, o_ref, h_scr, g_scr, acc_scr):
    i = pl.program_id(0)
    k = pl.program_id(1)
    tm = x_ref.shape[0]

    @pl.when(k == 0)
    def _():
        _fill_halo_tile(h_scr, i, nt, x_ref, xp_ref, xn_ref, ng_ref[...], sc_ref[...], sh_ref[...])
        acc_scr[...] = jnp.zeros_like(acc_scr)

    u = jnp.dot(h_scr[HALO:HALO + tm, :], wu_ref[...], preferred_element_type=F32)
    g_scr[...] = jnp.dot(h_scr[...], wg_ref[...], preferred_element_type=F32)
    conv = (cw_ref[0:1, :] * g_scr[pl.ds(HALO - 1, tm), :]
            + cw_ref[1:2, :] * g_scr[pl.ds(HALO, tm), :]
            + cw_ref[2:3, :] * g_scr[pl.ds(HALO + 1, tm), :])
    acc_scr[...] += jnp.dot((_silu(conv) * u).astype(BF16), wd_ref[...], preferred_element_type=F32)

    @pl.when(k == nk - 1)
    def _():
        y = x_ref[...] + g2_ref[...] * acc_scr[...]
        if final:
            y = y * lax.rsqrt(jnp.mean(y * y, axis=-1, keepdims=True) + EPS) * fg_ref[...]
        o_ref[...] = y


def _ffn(x, ng, sc, sh, g2, fg, w_up, cw, w_down, l, final):
    t = x.shape[0]
    tm = min(t, 512)
    bk = 512
    nt, nk = t // tm, D_FF // bk
    vec = pl.BlockSpec((1, D_MODEL), lambda i, k: (0, 0))
    return pl.pallas_call(
        functools.partial(_ffn_kernel, nt, nk, final),
        grid=(nt, nk),
        in_specs=_halo_specs(t, tm) + [
                  vec, vec, vec, vec, vec,
                  pl.BlockSpec((None, D_MODEL, bk), lambda i, k: (l, 0, k)),
                  pl.BlockSpec((None, D_MODEL, bk), lambda i, k: (l, 0, nk + k)),
                  pl.BlockSpec((FFN_CONV, bk), lambda i, k: (0, k)),
                  pl.BlockSpec((None, bk, D_MODEL), lambda i, k: (l, k, 0))],
        out_specs=pl.BlockSpec((tm, D_MODEL), lambda i, k: (i, 0)),
        out_shape=jax.ShapeDtypeStruct((t, D_MODEL), F32),
        scratch_shapes=[pltpu.VMEM((tm + 2 * HALO, D_MODEL), BF16),
                        pltpu.VMEM((tm + 2 * HALO, bk), F32),
                        pltpu.VMEM((tm, D_MODEL), F32)],
        compiler_params=_cparams(("arbitrary", "arbitrary")),
        name="ffn",
    )(x, x, x, ng, sc, sh, g2, fg, w_up, w_up, cw, w_down)


PREP_ROWS = 256


def _perm_kernel(w_ref, o_ref):
    w = w_ref[...]
    dst = 0
    for name in _DST_ORDER:
        src, width = _SRC[name]
        if width >= 128:
            o_ref[:, dst:dst + width] = w[:, src:src + width].astype(BF16)
            dst += width
    assert dst == COL_GATES
    o_ref[:, COL_GATES:N_PROJ] = jnp.zeros((PREP_ROWS, N_PROJ - COL_GATES), BF16)
    used = LANE_ML_F + 8
    lane = lax.broadcasted_iota(jnp.int32, (PREP_ROWS, used), 1)
    gates = jnp.where(lane < LANE_SSD_DT, w[:, _SRC["dn_a"][0]:_SRC["dn_a"][0] + used],
                      jnp.where(lane < LANE_ML_I,
                                w[:, _SRC["ssd_dt"][0] - LANE_SSD_DT:_SRC["ssd_dt"][0] - LANE_SSD_DT + used],
                                w[:, _SRC["ml_i"][0] - LANE_ML_I:_SRC["ml_i"][0] - LANE_ML_I + used]))
    o_ref[:, COL_GATES:COL_GATES + used] = gates.astype(BF16)


def _perm_w_in(w):
    assert (_SRC["ssd_dt"][0] - LANE_SSD_DT) % 128 == 0 and (_SRC["ml_i"][0] - LANE_ML_I) % 128 == 0
    assert _SRC["dn_a"][0] % 128 == 0 and _SRC["dn_b"][0] == _SRC["dn_a"][0] + LANE_DN_B
    assert _SRC["ml_f"][0] == _SRC["ml_i"][0] + LANE_ML_F - LANE_ML_I
    depth, rows, _ = w.shape
    return pl.pallas_call(
        _perm_kernel,
        grid=(depth, rows // PREP_ROWS),
        in_specs=[pl.BlockSpec((None, PREP_ROWS, D_IN), lambda l, i: (l, i, 0))],
        out_specs=pl.BlockSpec((None, PREP_ROWS, N_PROJ), lambda l, i: (l, i, 0)),
        out_shape=jax.ShapeDtypeStruct((depth, rows, N_PROJ), BF16),
        compiler_params=_cparams(("arbitrary", "arbitrary")),
        name="w_in_layout",
    )(w)


def _lane_row(pairs):
    tile = jnp.zeros((8, 128), F32)
    for row, lane, vals in pairs:
        vals = vals.reshape(-1).astype(F32)
        tile = tile.at[row, lane:lane + vals.shape[0]].set(vals)
    return tile


def _block_diag(w):
    nb, bw = w.shape[1], w.shape[2]
    eye = jnp.eye(nb, dtype=w.dtype)
    full = jnp.einsum("dnjk,nm->dnjmk", w, eye)
    return full.reshape(2, nb * bw, nb * bw).astype(BF16)


def _ctx_to_cols(a, wcols):
    t, ch = a.shape
    return a.reshape(wcols, t // wcols, ch).transpose(1, 0, 2).reshape(t, ch)


def _ctx_from_cols(a, wcols):
    t, ch = a.shape
    return a.reshape(t // wcols, wcols, ch).transpose(1, 0, 2).reshape(t, ch)


CTX_COLS = 8


def kernel(x, c, ctx, c_ctx, ada_w, ada_b, norm1_g, norm2_g, w_in, dn_conv_w, dn_a_log, dn_dt_bias, dn_norm_g, ssd_conv_w, ssd_conv_b, ssd_a_log, ssd_dt_bias, ssd_d, ssd_norm_g, lru_conv_w, lru_conv_b, lru_w_a, lru_b_a, lru_w_i, lru_b_i, lru_lambda, ml_ig_b, ml_fg_b, ml_norm_g, w_out, ffn_w_up, ffn_conv_w, ffn_w_down, final_norm_g):
    assert x.shape[0] == 1 and c.shape[0] == 1
    lat, hctx = x[0], ctx[0]
    cc = jnp.zeros((8, D_MODEL), F32).at[0].set(c[0]).at[1].set(c_ctx)
    mods = _ada(cc, ada_w, ada_b)
    row = lambda v: v.reshape(1, -1).astype(F32)
    w_in_p = _perm_w_in(w_in)
    w_out_b = w_out.astype(BF16)
    w_up_b = ffn_w_up.astype(BF16)
    w_down_b = ffn_w_down.astype(BF16)

    for l in range(DEPTH):
        mod_l = mods[l, 0].reshape(N_MOD, 1, D_MODEL)
        mod_c = mods[l, 1].reshape(N_MOD, 1, D_MODEL)
        ng1, ng2 = row(norm1_g[l]), row(norm2_g[l])

        dn_par = _lane_row([(0, LANE_DN_A, dn_a_log[l]), (1, LANE_DN_A, dn_dt_bias[l])])
        ssd_par = _lane_row([(0, LANE_SSD_DT, ssd_dt_bias[l]), (1, LANE_SSD_DT, ssd_a_log[l])])
        ml_par = _lane_row([(0, LANE_ML_I, ml_ig_b[l]), (1, LANE_ML_F, ml_fg_b[l])])
        lru_args = (lru_conv_w[l], row(lru_conv_b[l]), _block_diag(lru_w_a[l]), lru_b_a[l],
                    _block_diag(lru_w_i[l]), lru_b_i[l], lru_lambda[l])
        dn_g = row(jnp.tile(dn_norm_g[l], DN_HEADS))
        sd_d = row(jnp.repeat(ssd_d[l], SSD_HEAD_DIM))
        sd_g, ml_g = row(ssd_norm_g[l]), row(ml_norm_g[l])

        conv_w = jnp.concatenate([ssd_conv_w[l], dn_conv_w[l]], axis=1)
        conv_b = jnp.concatenate([row(ssd_conv_b[l]), jnp.zeros((1, 3 * GROUP_W), F32)], axis=1)

        pc = _inproj(hctx, ng1, mod_c[1], mod_c[0], w_in_p, l, conv_w, conv_b)
        dn_cf, dn_cb, dn_s = _dn_scan(pc, dn_par, jnp.zeros((2 * DN_HEADS, DN_DK, DN_DV), F32))
        sd_cf, sd_cb, sd_s = _ssd_scan(
            pc, ssd_par, jnp.zeros((2 * SSD_GROUPS, SSD_STATE, GROUP_W // SSD_GROUPS), F32))
        ml_cf, ml_cb, ml_c, ml_m = _ml_scan(
            pc, ml_par, jnp.zeros((2 * ML_HEADS, ML_DK, 2 * ML_DV), F32), jnp.zeros((1, 128), F32))
        xc_cols = _ctx_to_cols(pc[:, COL_LRU_X:COL_LRU_X + LRU_W], CTX_COLS)
        lr_c, lr_s = _lru_scan(xc_cols, 0, CTX_COLS, *lru_args, jnp.zeros((2, LRU_W), F32))

        pl_ = _inproj(lat, ng1, mod_l[1], mod_l[0], w_in_p, l, conv_w, conv_b)
        dn_lf, dn_lb, _ = _dn_scan(pl_, dn_par, dn_s)
        sd_lf, sd_lb, _ = _ssd_scan(pl_, ssd_par, sd_s)
        ml_lf, ml_lb, _, _ = _ml_scan(pl_, ml_par, ml_c, ml_m)
        lr_l, _ = _lru_scan(pl_, COL_LRU_X // 128, GRID_W, *lru_args, lr_s)

        lat = _outproj(lat, mod_l[2], pl_, dn_lf, dn_lb, dn_g, sd_lf, sd_lb, sd_d, sd_g,
                       lr_l, ml_lf, ml_lb, ml_g, w_out_b, l)
        last = l == DEPTH - 1
        lat = _ffn(lat, ng2, mod_l[4], mod_l[3], mod_l[5], row(final_norm_g), w_up_b,
                   ffn_conv_w[l], w_down_b, l, last)

        if not last:
            lr_cn = _ctx_from_cols(lr_c, CTX_COLS)
            hctx = _outproj(hctx, mod_c[2], pc, dn_cf, dn_cb, dn_g, sd_cf, sd_cb, sd_d, sd_g,
                            lr_cn, ml_cf, ml_cb, ml_g, w_out_b, l)
            hctx = _ffn(hctx, ng2, mod_c[4], mod_c[3], mod_c[5], row(final_norm_g), w_up_b,
                        ffn_conv_w[l], w_down_b, l, False)

    return lat[None]
```

```python
import functools

import jax
import jax.numpy as jnp
from jax import lax
from jax.experimental import pallas as pl
from jax.experimental.pallas import tpu as pltpu

F32 = jnp.float32
BF16 = jnp.bfloat16

D_MODEL = 2048
DEPTH = 2
GRID_W = 64
GROUP_W = D_MODEL // 4
CHUNK = 64
SHORT_CONV = 5
FFN_CONV = 3
D_FF = ((8 * D_MODEL // 3 + 255) // 256) * 256
N_MOD = 6
EPS = 1e-6

DN_HEADS = 4
DN_DK = GROUP_W // DN_HEADS
DN_DV = GROUP_W // DN_HEADS
SSD_HEAD_DIM = 64
SSD_HEADS = GROUP_W // SSD_HEAD_DIM
SSD_GROUPS = 2
SSD_STATE = 128
LRU_W = GROUP_W
LRU_BLOCKS = 8
LRU_BW = LRU_W // LRU_BLOCKS
LRU_C = 8.0
ML_HEADS = 4
ML_DV = GROUP_W // ML_HEADS
ML_DK = ML_DV // 2

_SRC = {}
_off = 0
for _name, _w in (
    ("dn_q", 512), ("dn_k", 512), ("dn_v", 512), ("dn_z", 512), ("dn_a", 8), ("dn_b", 8),
    ("ssd_x", 512), ("ssd_z", 512), ("ssd_B", 256), ("ssd_C", 256), ("ssd_dt", 16),
    ("lru_x", 512), ("lru_g", 512),
    ("ml_q", 256), ("ml_k", 256), ("ml_v", 512), ("ml_o", 512), ("ml_i", 8), ("ml_f", 8),
):
    _SRC[_name] = (_off, _w)
    _off += _w
D_IN = _off

_DST_ORDER = ("ssd_x", "ssd_B", "ssd_C", "dn_q", "dn_k", "dn_v", "ml_q", "ml_k", "ml_v",
              "lru_x", "dn_z", "ssd_z", "lru_g", "ml_o", "dn_a", "dn_b", "ssd_dt", "ml_i", "ml_f")
N_PROJ = 6400
COL_SSD_XBC = 0
COL_DN_Q = 1024
COL_DN_K = 1536
COL_DN_V = 2048
N_CONV = 2560
COL_ML_QK = 2560
COL_ML_V = 3072
COL_LRU_X = 3584
COL_DN_Z = 4096
COL_SSD_Z = 4608
COL_LRU_G = 5120
COL_ML_O = 5632
COL_GATES = 6144
LANE_DN_A, LANE_DN_B, LANE_SSD_DT, LANE_ML_I, LANE_ML_F = 0, 8, 16, 32, 40

VMEM_LIMIT = 56 * 1024 * 1024


def _cparams(sem):
    return pltpu.CompilerParams(dimension_semantics=sem, vmem_limit_bytes=VMEM_LIMIT)


_NN = (((1,), (0,)), ((), ()))
_NT = (((1,), (1,)), ((), ()))
_TN = (((0,), (0,)), ((), ()))


def _dg(a, b, dims):
    return lax.dot_general(a, b, dims, preferred_element_type=F32)


def _mm(a, b, dims=_NN):
    return _dg(a.astype(BF16), b.astype(BF16), dims)


def _split3(a):
    hi = a.astype(BF16)
    r = a - hi.astype(F32)
    mid = r.astype(BF16)
    return hi, mid, (r - mid.astype(F32)).astype(BF16)


def _mm_sel(sel, x):
    sb = sel.astype(BF16)
    x0, x1, x2 = _split3(x)
    return _dg(sb, x0, _NN) + (_dg(sb, x1, _NN) + _dg(sb, x2, _NN))


def _mm_spread(x, sel):
    sb = sel.astype(BF16)
    x0, x1, x2 = _split3(x)
    return _dg(x0, sb, _NN) + (_dg(x1, sb, _NN) + _dg(x2, sb, _NN))


def _silu(x):
    return x * jax.nn.sigmoid(x)


def _masks(d):
    ri = lax.broadcasted_iota(jnp.int32, (CHUNK, CHUNK), 0)
    ci = lax.broadcasted_iota(jnp.int32, (CHUNK, CHUNK), 1)
    if d == 0:
        return ri >= ci, ri > ci
    return ri <= ci, ri < ci


def _seg_decay(col, row, incl):
    return jnp.where(incl, jnp.exp(jnp.where(incl, col - row, 0.0)), 0.0)


ADA_TK = 256


def _ada_kernel(c_ref, w_ref, b_ref, o_ref):
    k = pl.program_id(1)

    @pl.when(k == 0)
    def _():
        o_ref[...] = jnp.broadcast_to(b_ref[...], o_ref.shape)

    cc = c_ref[...]
    act = (cc * jax.nn.sigmoid(cc)).astype(BF16)
    o_ref[...] += jnp.dot(act, w_ref[...].astype(BF16), preferred_element_type=F32)


def _ada(cc, ada_w, ada_b):
    n = N_MOD * D_MODEL
    return pl.pallas_call(
        _ada_kernel,
        grid=(DEPTH, D_MODEL // ADA_TK),
        in_specs=[
            pl.BlockSpec((8, ADA_TK), lambda l, k: (0, k)),
            pl.BlockSpec((None, ADA_TK, n), lambda l, k: (l, k, 0)),
            pl.BlockSpec((None, 1, n), lambda l, k: (l, 0, 0)),
        ],
        out_specs=pl.BlockSpec((None, 8, n), lambda l, k: (l, 0, 0)),
        out_shape=jax.ShapeDtypeStruct((DEPTH, 8, n), F32),
        compiler_params=_cparams(("arbitrary", "arbitrary")),
        name="ada",
    )(cc, ada_w, ada_b.reshape(DEPTH, 1, n))


def _norm_mod(xf, ng, sc, sh):
    y = xf * lax.rsqrt(jnp.mean(xf * xf, axis=-1, keepdims=True) + EPS) * ng
    return y * (1.0 + sc) + sh


HALO = 16
PROJ_TN = 1280
PROJ_SUB = 256


def _fill_halo_tile(h_scr, i, nt, x_ref, xp_ref, xn_ref, ng, sc, sh):
    tm = x_ref.shape[0]
    hp = jnp.where(i > 0, _norm_mod(xp_ref[...], ng, sc, sh), 0.0)
    hn = jnp.where(i < nt - 1, _norm_mod(xn_ref[...], ng, sc, sh), 0.0)
    h_scr[0:HALO, :] = hp.astype(BF16)
    h_scr[HALO:HALO + tm, :] = _norm_mod(x_ref[...], ng, sc, sh).astype(BF16)
    h_scr[HALO + tm:2 * HALO + tm, :] = hn.astype(BF16)


def _inproj_kernel(nt, x_ref, xp_ref, xn_ref, ng_ref, sc_ref, sh_ref, w_ref, cw_ref, cb_ref,
                   o_ref, h_scr):
    i = pl.program_id(0)
    j = pl.program_id(1)
    tm = x_ref.shape[0]

    @pl.when(j == 0)
    def _():
        _fill_halo_tile(h_scr, i, nt, x_ref, xp_ref, xn_ref, ng_ref[...], sc_ref[...], sh_ref[...])

    for jj in range(N_CONV // PROJ_TN):
        @pl.when(j == jj)
        def _():
            rows = tm + 2 * HALO
            for c in range(PROJ_TN // PROJ_SUB):
                cs = slice(c * PROJ_SUB, (c + 1) * PROJ_SUB)
                y = jnp.dot(h_scr[...], w_ref[:, cs], preferred_element_type=F32)
                acc = cb_ref[:, cs] + cw_ref[2:3, cs] * y[HALO:HALO + tm]
                for tap in (0, 1, 3, 4):
                    shifted = pltpu.roll(y, (2 - tap) % rows, axis=0)
                    acc = acc + cw_ref[tap:tap + 1, cs] * shifted[HALO:HALO + tm]
                act = _silu(acc)
                for half in range(PROJ_SUB // 128):
                    col = jj * PROJ_TN + c * PROJ_SUB + half * 128
                    a = act[:, half * 128:(half + 1) * 128]
                    if COL_DN_Q <= col < COL_DN_V:
                        a = a * lax.rsqrt(jnp.sum(a * a, axis=-1, keepdims=True) + EPS)
                        if col < COL_DN_K:
                            a = a * DN_DK ** -0.5
                    o_ref[:, c * PROJ_SUB + half * 128:c * PROJ_SUB + (half + 1) * 128] = a

    @pl.when(j >= N_CONV // PROJ_TN)
    def _():
        o_ref[...] = jnp.dot(h_scr[HALO:HALO + tm, :], w_ref[...], preferred_element_type=F32)


def _halo_specs(t, tm):
    per, nhb = tm // HALO, t // HALO
    return [pl.BlockSpec((tm, D_MODEL), lambda i, k: (i, 0)),
            pl.BlockSpec((HALO, D_MODEL), lambda i, k: (jnp.maximum(i * per - 1, 0), 0)),
            pl.BlockSpec((HALO, D_MODEL), lambda i, k: (jnp.minimum((i + 1) * per, nhb - 1), 0))]


def _inproj(x, ng, sc, sh, w, l, cw, cb):
    t = x.shape[0]
    tm = min(t, 1024)
    tn = PROJ_TN
    nconv = N_CONV // tn
    row = lambda i, j: (0, 0)
    return pl.pallas_call(
        functools.partial(_inproj_kernel, t // tm),
        grid=(t // tm, N_PROJ // tn),
        in_specs=_halo_specs(t, tm) + [
            pl.BlockSpec((1, D_MODEL), row),
            pl.BlockSpec((1, D_MODEL), row),
            pl.BlockSpec((1, D_MODEL), row),
            pl.BlockSpec((None, D_MODEL, tn), lambda i, j: (l, 0, j)),
            pl.BlockSpec((SHORT_CONV, tn), lambda i, j: (0, jnp.minimum(j, nconv - 1))),
            pl.BlockSpec((1, tn), lambda i, j: (0, jnp.minimum(j, nconv - 1))),
        ],
        out_specs=pl.BlockSpec((tm, tn), lambda i, j: (i, j)),
        out_shape=jax.ShapeDtypeStruct((t, N_PROJ), F32),
        scratch_shapes=[pltpu.VMEM((tm + 2 * HALO, D_MODEL), BF16)],
        compiler_params=_cparams(("arbitrary", "arbitrary")),
        name="inproj",
    )(x, x, x, ng, sc, sh, w, cw, cb)


SUBS = 4
ROWS = SUBS * CHUNK


def _sub_order(d):
    return list(range(SUBS)) if d == 0 else list(range(SUBS - 1, -1, -1))


def _col_spec(nb, width, col, rev):
    blk = col // width
    assert blk * width == col
    if rev:
        return pl.BlockSpec((ROWS, width), lambda i: (nb - 1 - i, blk))
    return pl.BlockSpec((ROWS, width), lambda i: (i, blk))


def _gate_spec(nb, rev):
    return _col_spec(nb, 128, COL_GATES, rev)


def _out_spec(nb, width, rev):
    if rev:
        return pl.BlockSpec((ROWS, width), lambda i: (nb - 1 - i, 0))
    return pl.BlockSpec((ROWS, width), lambda i: (i, 0))


def _const_spec(shape):
    nd = len(shape)
    return pl.BlockSpec(shape, lambda i: (0,) * nd)


def _cumsum_dir(x, d):
    ri = lax.broadcasted_iota(jnp.int32, (ROWS, ROWS), 0)
    ci = lax.broadcasted_iota(jnp.int32, (ROWS, ROWS), 1)
    incl = (ri >= ci) if d == 0 else (ri <= ci)
    return _mm_sel(jnp.where((ri // CHUNK == ci // CHUNK) & incl, 1.0, 0.0), x)


def _dn_kernel(qf, kf, vf, qb, kb_, vb_, gf, gb, par_ref, s0_ref, of_ref, ob_ref, s_ref):
    i = pl.program_id(0)

    @pl.when(i == 0)
    def _():
        s_ref[...] = s0_ref[...]

    eye = jnp.where(_masks(0)[0] & _masks(1)[0], 1.0, 0.0).astype(F32)
    hd = []
    for d, (q_ref, k_ref, v_ref, g_ref, o_ref) in enumerate(((qf, kf, vf, gf, of_ref),
                                                             (qb, kb_, vb_, gb, ob_ref))):
        incl, strict = _masks(d)
        last = CHUNK - 1 if d == 0 else 0
        gates = g_ref[...]
        g_all = -jnp.exp(par_ref[0:1, :]) * jax.nn.softplus(gates + par_ref[1:2, :])
        beta_all = jax.nn.sigmoid(gates)
        gc = _cumsum_dir(g_all, d)
        gct = gc.T
        for pos, sub in enumerate(_sub_order(d)):
            rs = slice(sub * CHUNK, (sub + 1) * CHUNK)
            for h in range(DN_HEADS):
                lane = LANE_DN_A + d * DN_HEADS + h
                blane = LANE_DN_B + d * DN_HEADS + h
                k = k_ref[rs, h * DN_DK:(h + 1) * DN_DK]
                gcc = gc[rs, lane:lane + 1]
                beta = beta_all[rs, blane:blane + 1]
                hd.append(dict(
                    q=q_ref[rs, h * DN_DK:(h + 1) * DN_DK], k=k, kb=k * beta,
                    vb=v_ref[rs, h * DN_DV:(h + 1) * DN_DV] * beta, gcc=gcc,
                    tot=gc[sub * CHUNK + last:sub * CHUNK + last + 1, lane:lane + 1],
                    strict=strict, decay=_seg_decay(gcc, gct[lane:lane + 1, rs], incl),
                    pos=pos, rs=rs, o_ref=o_ref, osl=slice(h * DN_DV, (h + 1) * DN_DV),
                    sidx=d * DN_HEADS + h))

    nmat = [jnp.where(x["strict"], _mm(x["kb"], x["k"], _NT) * x["decay"], 0.0) for x in hd]
    attn = [_mm(x["q"], x["k"], _NT) * x["decay"] for x in hd]
    ri = lax.broadcasted_iota(jnp.int32, (CHUNK, CHUNK), 0)
    ci = lax.broadcasted_iota(jnp.int32, (CHUNK, CHUNK), 1)
    base = 8
    pw = [jnp.where(ri // base == ci // base, n, 0.0) for n in nmat]
    tinv = [eye - p for p in pw]
    for _ in range(2):
        pw = [_mm(p, p) for p in pw]
        tinv = [t + _mm(t, p) for t, p in zip(tinv, pw)]
    size = base
    while size < CHUNK:
        sib = (ri // (2 * size) == ci // (2 * size)) & (ri // size != ci // size)
        tc = [_mm(t, jnp.where(sib, n, 0.0)) for t, n in zip(tinv, nmat)]
        tinv = [t - _mm(c, t) for t, c in zip(tinv, tc)]
        size *= 2
    sol = [_mm(t, jnp.concatenate([x["vb"], x["kb"] * jnp.exp(x["gcc"])], axis=-1))
           for t, x in zip(tinv, hd)]
    state = {j: s_ref[j] for j in range(2 * DN_HEADS)}
    for pos in range(SUBS):
        cur = [(x, so, a) for x, so, a in zip(hd, sol, attn) if x["pos"] == pos]
        v_new = [so[:, :DN_DV] - _mm(so[:, DN_DV:], state[x["sidx"]]) for x, so, _ in cur]
        for (x, _, a), vn in zip(cur, v_new):
            x["o_ref"][x["rs"], x["osl"]] = (_mm(x["q"] * jnp.exp(x["gcc"]), state[x["sidx"]])
                                             + _mm(a, vn))
        for (x, _, _), vn in zip(cur, v_new):
            state[x["sidx"]] = (state[x["sidx"]] * jnp.exp(x["tot"])
                                + _mm(x["k"] * jnp.exp(x["tot"] - x["gcc"]), vn, _TN))
    for j in range(2 * DN_HEADS):
        s_ref[j] = state[j]


def _dn_scan(p, par, s0):
    t = p.shape[0]
    nb = t // ROWS
    out = jax.ShapeDtypeStruct((t, GROUP_W), F32)
    qkv = lambda rev: [_col_spec(nb, GROUP_W, col, rev) for col in (COL_DN_Q, COL_DN_K, COL_DN_V)]
    return pl.pallas_call(
        _dn_kernel,
        grid=(nb,),
        in_specs=(qkv(False) + qkv(True)
                  + [_gate_spec(nb, False), _gate_spec(nb, True), _const_spec((8, 128)),
                     _const_spec((2 * DN_HEADS, DN_DK, DN_DV))]),
        out_specs=[_out_spec(nb, GROUP_W, False), _out_spec(nb, GROUP_W, True),
                   _const_spec((2 * DN_HEADS, DN_DK, DN_DV))],
        out_shape=[out, out, jax.ShapeDtypeStruct((2 * DN_HEADS, DN_DK, DN_DV), F32)],
        compiler_params=_cparams(("arbitrary",)),
        name="dn_scan",
    )(p, p, p, p, p, p, p, p, par, s0)


def _ssd_kernel(xf, xb, gf, gb, par_ref, s0_ref, of_ref, ob_ref, s_ref):
    i = pl.program_id(0)

    @pl.when(i == 0)
    def _():
        s_ref[...] = s0_ref[...]

    hp = SSD_HEADS * SSD_HEAD_DIM
    gw = hp // SSD_GROUPS
    rep = SSD_HEADS // SSD_GROUPS
    items = []
    for d, (x_ref, g_ref, o_ref) in enumerate(((xf, gf, of_ref), (xb, gb, ob_ref))):
        incl, _ = _masks(d)
        last = CHUNK - 1 if d == 0 else 0
        gates = g_ref[...]
        dt_all = jax.nn.softplus(gates + par_ref[0:1, :])
        a_all = -jnp.exp(par_ref[1:2, :]) * dt_all
        acs = _cumsum_dir(a_all, d)
        acst = acs.T
        er = lax.broadcasted_iota(jnp.int32, (128, hp), 0)
        ec = lax.broadcasted_iota(jnp.int32, (128, hp), 1)
        expand = jnp.where(er == LANE_SSD_DT + d * SSD_HEADS + ec // SSD_HEAD_DIM, 1.0, 0.0)
        both_x = _mm_spread(jnp.concatenate([dt_all, acs], axis=0), expand)
        dt_x, acs_x = both_x[:ROWS], both_x[ROWS:]
        xd_all = x_ref[:, :hp] * dt_x
        for pos, sub in enumerate(_sub_order(d)):
            rs = slice(sub * CHUNK, (sub + 1) * CHUNK)
            tot_x = acs_x[sub * CHUNK + last:sub * CHUNK + last + 1, :]
            xd = xd_all[rs]
            xdw = xd * jnp.exp(tot_x - acs_x[rs])
            for g in range(SSD_GROUPS):
                bm = x_ref[rs, hp + g * SSD_STATE:hp + (g + 1) * SSD_STATE]
                cm = x_ref[rs, hp + SSD_GROUPS * SSD_STATE + g * SSD_STATE:
                           hp + SSD_GROUPS * SSD_STATE + (g + 1) * SSD_STATE]
                cb = _mm(cm, bm, _NT)
                gs = slice(g * gw, (g + 1) * gw)
                y_diag = []
                for hh in range(rep):
                    h = g * rep + hh
                    lane = LANE_SSD_DT + d * SSD_HEADS + h
                    hs = slice(h * SSD_HEAD_DIM, (h + 1) * SSD_HEAD_DIM)
                    lmat = _seg_decay(acs[rs, lane:lane + 1], acst[lane:lane + 1, rs], incl)
                    y_diag.append(_mm(cb * lmat, xd[:, hs]))
                items.append(dict(
                    pos=pos, rs=rs, gs=gs, o_ref=o_ref, sidx=d * SSD_GROUPS + g, cm=cm,
                    y_diag=jnp.concatenate(y_diag, axis=-1), off_scale=jnp.exp(acs_x[rs, gs]),
                    dec=jnp.exp(tot_x[:, gs]), local=_mm(bm, xdw[:, gs], _TN)))
    state = {j: s_ref[j] for j in range(2 * SSD_GROUPS)}
    for pos in range(SUBS):
        for x in [x for x in items if x["pos"] == pos]:
            s = state[x["sidx"]]
            x["o_ref"][x["rs"], x["gs"]] = _mm(x["cm"], s) * x["off_scale"] + x["y_diag"]
            state[x["sidx"]] = s * x["dec"] + x["local"]
    for j in range(2 * SSD_GROUPS):
        s_ref[j] = state[j]


def _ssd_scan(p, par, s0):
    t = p.shape[0]
    nb = t // ROWS
    wx = 2 * GROUP_W
    hp = SSD_HEADS * SSD_HEAD_DIM
    st = (2 * SSD_GROUPS, SSD_STATE, hp // SSD_GROUPS)
    out = jax.ShapeDtypeStruct((t, hp), F32)
    return pl.pallas_call(
        _ssd_kernel,
        grid=(nb,),
        in_specs=[_col_spec(nb, wx, COL_SSD_XBC, False), _col_spec(nb, wx, COL_SSD_XBC, True),
                  _gate_spec(nb, False), _gate_spec(nb, True), _const_spec((8, 128)),
                  _const_spec(st)],
        out_specs=[_out_spec(nb, hp, False), _out_spec(nb, hp, True), _const_spec(st)],
        out_shape=[out, out, jax.ShapeDtypeStruct(st, F32)],
        compiler_params=_cparams(("arbitrary",)),
        name="ssd_scan",
    )(p, p, p, p, par, s0)


def _cummax_dir(x, d):
    rows = x.shape[0]
    sub = lax.broadcasted_iota(jnp.int32, x.shape, 0) % CHUNK
    sh = 1
    while sh < CHUNK:
        if d == 0:
            x = jnp.where(sub >= sh, jnp.maximum(x, pltpu.roll(x, sh, axis=0)), x)
        else:
            x = jnp.where(sub < CHUNK - sh, jnp.maximum(x, pltpu.roll(x, rows - sh, axis=0)), x)
        sh *= 2
    return x


ML_PACK = 3 * CHUNK + 8


def _ml_kernel(qkf, vf, qkb, vb, gf, gb, par_ref, c0_ref, m0_ref, of_ref, ob_ref, c_ref, m_ref):
    i = pl.program_id(0)

    @pl.when(i == 0)
    def _():
        c_ref[...] = c0_ref[...]
        m_ref[...] = m0_ref[...]

    lane = lax.broadcasted_iota(jnp.int32, (1, 128), 1)
    ones = jnp.ones((CHUNK, ML_DV), F32)
    hd, m_rows = [], []
    for d, (qk_ref, v_ref, g_ref, o_ref) in enumerate(((qkf, vf, gf, of_ref), (qkb, vb, gb, ob_ref))):
        incl, _ = _masks(d)
        last = CHUNK - 1 if d == 0 else 0
        lane0 = LANE_ML_F + d * ML_HEADS
        valid = (lane >= lane0) & (lane < lane0 + ML_HEADS)
        gates = g_ref[...]
        ig_all = pltpu.roll(gates + par_ref[0:1, :], LANE_ML_F - LANE_ML_I, axis=1)
        b_all = _cumsum_dir(jax.nn.log_sigmoid(gates + par_ref[1:2, :]), d)
        r_all = ig_all - b_all
        cmax = _cummax_dir(r_all, d)
        rt = r_all.T
        m_cur = m_ref[...]
        packed, wk_src = [], {}
        for sub in _sub_order(d):
            rs = slice(sub * CHUNK, (sub + 1) * CHUNK)
            b = b_all[rs]
            m_all = b + jnp.maximum(m_cur, cmax[rs])
            b_last = b_all[sub * CHUNK + last:sub * CHUNK + last + 1, :]
            log_g = b_last - b + ig_all[rs]
            m_new = jnp.maximum(b_last + m_cur, jnp.max(log_g, axis=0, keepdims=True))
            packed += [b - m_all, jnp.exp(b + m_cur - m_all), jnp.exp(-m_all),
                       jnp.broadcast_to(jnp.exp(b_last + m_cur - m_new), (8, 128))]
            wk_src[sub] = jnp.exp(log_g - m_new)
            m_cur = m_new
        m_rows.append(jnp.where(valid, m_cur, 0.0))
        sr = lax.broadcasted_iota(jnp.int32, (128, ML_HEADS * ML_DV), 0)
        sc = lax.broadcasted_iota(jnp.int32, (128, ML_HEADS * ML_DV), 1)
        wide = _mm_spread(jnp.where(valid, jnp.concatenate(packed, axis=0), 0.0),
                          jnp.where(sr == lane0 + sc // ML_DV, 1.0, 0.0))
        sr = lax.broadcasted_iota(jnp.int32, (128, ML_HEADS * ML_DK), 0)
        sc = lax.broadcasted_iota(jnp.int32, (128, ML_HEADS * ML_DK), 1)
        wkc = _mm_spread(
            jnp.where(valid, jnp.concatenate([wk_src[s] for s in range(SUBS)], axis=0), 0.0),
            jnp.where(sr == lane0 + sc // ML_DK, 1.0, 0.0))
        for pos, sub in enumerate(_sub_order(d)):
            rs = slice(sub * CHUNK, (sub + 1) * CHUNK)
            r0 = pos * ML_PACK
            for h in range(ML_HEADS):
                ws = slice(h * ML_DV, (h + 1) * ML_DV)
                k = qk_ref[rs, ML_HEADS * ML_DK + h * ML_DK:ML_HEADS * ML_DK + (h + 1) * ML_DK]
                hd.append(dict(
                    q=qk_ref[rs, h * ML_DK:(h + 1) * ML_DK] * ML_DK ** -0.5, k=k,
                    wk=wkc[rs, h * ML_DK:(h + 1) * ML_DK] * k,
                    v_aug=jnp.concatenate([v_ref[rs, h * ML_DV:(h + 1) * ML_DV], ones], axis=-1),
                    dmat=jnp.where(incl, jnp.exp(wide[r0:r0 + CHUNK, h * ML_DV:h * ML_DV + CHUNK]
                                                 + rt[lane0 + h:lane0 + h + 1, rs]), 0.0),
                    w_inter=wide[r0 + CHUNK:r0 + 2 * CHUNK, ws],
                    enm=wide[r0 + 2 * CHUNK:r0 + 3 * CHUNK, ws],
                    dec=wide[r0 + 3 * CHUNK:r0 + 3 * CHUNK + 1, ws],
                    pos=pos, rs=rs, sd=d * ML_HEADS + h, o_ref=o_ref, osl=ws))

    s = [_mm(x["q"], x["k"], _NT) * x["dmat"] for x in hd]
    intra = [_mm(s_h, x["v_aug"]) for s_h, x in zip(s, hd)]
    upd = [_mm(x["wk"], x["v_aug"], _TN) for x in hd]
    state = {j: c_ref[j] for j in range(2 * ML_HEADS)}
    for pos in range(SUBS):
        cur = [(x, ia, up) for x, ia, up in zip(hd, intra, upd) if x["pos"] == pos]
        inter = [_mm(x["q"], state[x["sd"]]) for x, _, _ in cur]
        for (x, ia, up), ie in zip(cur, inter):
            num = x["w_inter"] * ie[:, :ML_DV] + ia[:, :ML_DV]
            den = x["w_inter"] * ie[:, ML_DV:] + ia[:, ML_DV:]
            x["o_ref"][x["rs"], x["osl"]] = num / jnp.maximum(jnp.abs(den), x["enm"])
            state[x["sd"]] = jnp.concatenate([x["dec"], x["dec"]], axis=-1) * state[x["sd"]] + up
    for j in range(2 * ML_HEADS):
        c_ref[j] = state[j]
    m_ref[...] = m_rows[0] + m_rows[1]


def _ml_scan(p, par, c0, m0):
    t = p.shape[0]
    nb = t // ROWS
    cs, ms = (2 * ML_HEADS, ML_DK, 2 * ML_DV), (1, 128)
    out = jax.ShapeDtypeStruct((t, GROUP_W), F32)
    qkv = lambda rev: [_col_spec(nb, GROUP_W, COL_ML_QK, rev), _col_spec(nb, GROUP_W, COL_ML_V, rev)]
    return pl.pallas_call(
        _ml_kernel,
        grid=(nb,),
        in_specs=(qkv(False) + qkv(True)
                  + [_gate_spec(nb, False), _gate_spec(nb, True), _const_spec((8, 128)),
                     _const_spec(cs), _const_spec(ms)]),
        out_specs=[_out_spec(nb, GROUP_W, False), _out_spec(nb, GROUP_W, True),
                   _const_spec(cs), _const_spec(ms)],
        out_shape=[out, out, jax.ShapeDtypeStruct(cs, F32), jax.ShapeDtypeStruct(ms, F32)],
        compiler_params=_cparams(("arbitrary",)),
        name="ml_scan",
    )(p, p, p, p, p, p, par, c0, m0)


def _lru_kernel(rows, wcols, x_ref, cw_ref, cb_ref, wa_ref, ba_ref, wi_ref, bi_ref, lam_ref,
                h0_ref, o_ref, hfin_ref, xp_scr, a_scr, b_scr):
    t = rows * wcols
    sub = lax.broadcasted_iota(jnp.int32, (wcols, 128), 0)

    def shift_down(a):
        return jnp.where(sub >= 1, pltpu.roll(a, 1, axis=0), 0.0)

    def shift_up(a):
        return jnp.where(sub < wcols - 1, pltpu.roll(a, wcols - 1, axis=0), 0.0)

    def slab(r):
        return pl.ds(pl.multiple_of(r * wcols, wcols), wcols)

    xp_scr[pl.ds(2 * wcols, t), :] = x_ref[...]
    xp_scr[pl.ds(0, wcols), :] = shift_down(x_ref[pl.ds((rows - 2) * wcols, wcols), :])
    xp_scr[pl.ds(wcols, wcols), :] = shift_down(x_ref[pl.ds((rows - 1) * wcols, wcols), :])
    xp_scr[pl.ds((rows + 2) * wcols, wcols), :] = shift_up(x_ref[pl.ds(0, wcols), :])
    xp_scr[pl.ds((rows + 3) * wcols, wcols), :] = shift_up(x_ref[pl.ds(wcols, wcols), :])

    rb = 256 if t % 256 == 0 else t
    for d in range(2):
        sp_lam = jax.nn.softplus(-lam_ref[d:d + 1, :])

        def gate_body(blk, carry):
            base = pl.multiple_of(blk * rb, rb)
            xr = cb_ref[...] + cw_ref[0:1, :] * xp_scr[pl.ds(base, rb), :]
            for j in range(1, SHORT_CONV):
                xr = xr + cw_ref[j:j + 1, :] * xp_scr[pl.ds(pl.multiple_of(base + j * wcols, 8), rb), :]
            xb = xr.astype(BF16)
            r = jax.nn.sigmoid(jnp.dot(xb, wa_ref[d], preferred_element_type=F32) + ba_ref[d:d + 1, :])
            ii = jax.nn.sigmoid(jnp.dot(xb, wi_ref[d], preferred_element_type=F32) + bi_ref[d:d + 1, :])
            log_a = -LRU_C * r * sp_lam
            a = jnp.exp(log_a)
            b = jnp.sqrt(-jnp.tanh(log_a) * (a * a + 1.0)) * ii * xr
            a_scr[pl.ds(base, rb), :] = a
            b_scr[pl.ds(base, rb), :] = b
            return carry

        lax.fori_loop(0, t // rb, gate_body, 0)

        def scan_body(step, carry):
            h, acc = carry
            r = step if d == 0 else rows - 1 - step
            a = a_scr[slab(r), :]
            h = a * h + b_scr[slab(r), :]
            acc = a * acc
            b_scr[slab(r), :] = h
            a_scr[slab(r), :] = acc
            return h, acc

        h_end, a_end = lax.fori_loop(
            0, rows, scan_body, (jnp.zeros((wcols, 128), F32), jnp.ones((wcols, 128), F32)))

        sh = 1
        while sh < wcols:
            if d == 0:
                valid = sub >= sh
                a_sh, h_sh = pltpu.roll(a_end, sh, axis=0), pltpu.roll(h_end, sh, axis=0)
            else:
                valid = sub < wcols - sh
                a_sh, h_sh = pltpu.roll(a_end, wcols - sh, axis=0), pltpu.roll(h_end, wcols - sh, axis=0)
            h_end = jnp.where(valid, a_end * h_sh + h_end, h_end)
            a_end = jnp.where(valid, a_end * a_sh, a_end)
            sh *= 2
        h0 = h0_ref[d:d + 1, :]
        h_full = h_end + a_end * h0
        if d == 0:
            carry_in = jnp.where(sub >= 1, pltpu.roll(h_full, 1, axis=0), h0)
            hfin_ref[0:1, :] = h_full[wcols - 1:wcols, :]
        else:
            carry_in = jnp.where(sub < wcols - 1, pltpu.roll(h_full, wcols - 1, axis=0), h0)
            hfin_ref[1:2, :] = h_full[0:1, :]

        def fix_body(r, carry):
            hv = b_scr[slab(r), :] + a_scr[slab(r), :] * carry_in
            if d == 0:
                o_ref[slab(r), :] = hv
            else:
                o_ref[slab(r), :] += hv
            return carry

        lax.fori_loop(0, rows, fix_body, 0)


def _lru_scan(x, colblk, wcols, cw, cb, wa, ba, wi, bi, lam, h0):
    t = x.shape[0]
    rows = t // wcols
    nt = LRU_W // 128
    vec = lambda r: pl.BlockSpec((r, 128), lambda j: (0, j))
    wspec = pl.BlockSpec((2, 128, 128), lambda j: (0, j, j))
    return pl.pallas_call(
        functools.partial(_lru_kernel, rows, wcols),
        grid=(nt,),
        in_specs=[pl.BlockSpec((t, 128), lambda j: (0, colblk + j)),
                  vec(SHORT_CONV), vec(1), wspec, vec(2), wspec, vec(2), vec(2), vec(2)],
        out_specs=[pl.BlockSpec((t, 128), lambda j: (0, j)), vec(2)],
        out_shape=[jax.ShapeDtypeStruct((t, LRU_W), F32), jax.ShapeDtypeStruct((2, LRU_W), F32)],
        scratch_shapes=[pltpu.VMEM((t + 4 * wcols, 128), F32), pltpu.VMEM((t, 128), F32),
                        pltpu.VMEM((t, 128), F32)],
        compiler_params=_cparams(("arbitrary",)),
        name="lru_scan",
    )(x, cw, cb, wa, ba, wi, bi, lam, h0)


def _rms_groups(x, g, width):
    parts = []
    for s in range(0, x.shape[-1], width):
        xs = x[:, s:s + width]
        parts.append(xs * lax.rsqrt(jnp.mean(xs * xs, axis=-1, keepdims=True) + EPS) * g[:, s:s + width])
    return parts


def _outproj_kernel(x_ref, g1_ref, dnf, dnb, dnz, dng, sdf, sdb, sdx, sdz, sdd, sdg,
                    lrh, lrg, mlf, mlb, mlo, mlg, w_ref, o_ref):
    acc = jnp.zeros(o_ref.shape, F32)
    z = dnz[...]
    for h, part in enumerate(_rms_groups(dnf[...] + dnb[...], dng[...], DN_DV)):
        cs = slice(h * DN_DV, (h + 1) * DN_DV)
        acc += _mm(part * _silu(z[:, cs]), w_ref[cs, :])
    y = (sdf[...] + sdb[...] + sdd[...] * sdx[...]) * _silu(sdz[...])
    gw = GROUP_W // SSD_GROUPS
    for g, part in enumerate(_rms_groups(y, sdg[...], gw)):
        acc += _mm(part, w_ref[GROUP_W + g * gw:GROUP_W + (g + 1) * gw, :])
    acc += _mm(lrh[...] * jax.nn.gelu(lrg[...]), w_ref[2 * GROUP_W:3 * GROUP_W, :])
    o = mlo[...]
    for h, part in enumerate(_rms_groups(mlf[...] + mlb[...], mlg[...], ML_DV)):
        cs = slice(h * ML_DV, (h + 1) * ML_DV)
        acc += _mm(part * jax.nn.sigmoid(o[:, cs]),
                   w_ref[3 * GROUP_W + h * ML_DV:3 * GROUP_W + (h + 1) * ML_DV, :])
    o_ref[...] = x_ref[...] + g1_ref[...] * acc


def _outproj(x, g1, p, dn_f, dn_b, dn_g, sd_f, sd_b, sd_d, sd_g, lr_h, ml_f, ml_b, ml_g, w, l):
    t = x.shape[0]
    tm = min(t, 256)
    tok = pl.BlockSpec((tm, GROUP_W), lambda i: (i, 0))
    pcol = lambda col: pl.BlockSpec((tm, GROUP_W), lambda i: (i, col // GROUP_W))
    vec = lambda n: pl.BlockSpec((1, n), lambda i: (0, 0))
    return pl.pallas_call(
        _outproj_kernel,
        grid=(t // tm,),
        in_specs=[pl.BlockSpec((tm, D_MODEL), lambda i: (i, 0)), vec(D_MODEL),
                  tok, tok, pcol(COL_DN_Z), vec(GROUP_W),
                  tok, tok, pcol(COL_SSD_XBC), pcol(COL_SSD_Z), vec(GROUP_W), vec(GROUP_W),
                  tok, pcol(COL_LRU_G),
                  tok, tok, pcol(COL_ML_O), vec(GROUP_W),
                  pl.BlockSpec((None, D_MODEL, D_MODEL), lambda i: (l, 0, 0))],
        out_specs=pl.BlockSpec((tm, D_MODEL), lambda i: (i, 0)),
        out_shape=jax.ShapeDtypeStruct((t, D_MODEL), F32),
        compiler_params=_cparams(("arbitrary",)),
        name="outproj",
    )(x, g1, dn_f, dn_b, p, dn_g, sd_f, sd_b, p, p, sd_d, sd_g, lr_h, p, ml_f, ml_b, p, ml_g, w)


def _ffn_kernel(nt, nk, final, x_ref, xp_ref, xn_ref, ng_ref, sc_ref, sh_ref, g2_ref, fg_ref,
                wu_ref, wg_ref, cw_ref, wd_ref, o_ref, h_scr, g_scr, acc_scr):
    i = pl.program_id(0)
    k = pl.program_id(1)
    tm = x_ref.shape[0]

    @pl.when(k == 0)
    def _():
        _fill_halo_tile(h_scr, i, nt, x_ref, xp_ref, xn_ref, ng_ref[...], sc_ref[...], sh_ref[...])
        acc_scr[...] = jnp.zeros_like(acc_scr)

    u = jnp.dot(h_scr[HALO:HALO + tm, :], wu_ref[...], preferred_element_type=F32)
    g_scr[...] = jnp.dot(h_scr[...], wg_ref[...], preferred_element_type=F32)
    conv = (cw_ref[0:1, :] * g_scr[pl.ds(HALO - 1, tm), :]
            + cw_ref[1:2, :] * g_scr[pl.ds(HALO, tm), :]
            + cw_ref[2:3, :] * g_scr[pl.ds(HALO + 1, tm), :])
    acc_scr[...] += jnp.dot((_silu(conv) * u).astype(BF16), wd_ref[...], preferred_element_type=F32)

    @pl.when(k == nk - 1)
    def _():
        y = x_ref[...] + g2_ref[...] * acc_scr[...]
        if final:
            y = y * lax.rsqrt(jnp.mean(y * y, axis=-1, keepdims=True) + EPS) * fg_ref[...]
        o_ref[...] = y


def _ffn(x, ng, sc, sh, g2, fg, w_up, cw, w_down, l, final):
    t = x.shape[0]
    tm = min(t, 512)
    bk = 512
    nt, nk = t // tm, D_FF // bk
    vec = pl.BlockSpec((1, D_MODEL), lambda i, k: (0, 0))
    return pl.pallas_call(
        functools.partial(_ffn_kernel, nt, nk, final),
        grid=(nt, nk),
        in_specs=_halo_specs(t, tm) + [
                  vec, vec, vec, vec, vec,
                  pl.BlockSpec((None, D_MODEL, bk), lambda i, k: (l, 0, k)),
                  pl.BlockSpec((None, D_MODEL, bk), lambda i, k: (l, 0, nk + k)),
                  pl.BlockSpec((FFN_CONV, bk), lambda i, k: (0, k)),
                  pl.BlockSpec((None, bk, D_MODEL), lambda i, k: (l, k, 0))],
        out_specs=pl.BlockSpec((tm, D_MODEL), lambda i, k: (i, 0)),
        out_shape=jax.ShapeDtypeStruct((t, D_MODEL), F32),
        scratch_shapes=[pltpu.VMEM((tm + 2 * HALO, D_MODEL), BF16),
                        pltpu.VMEM((tm + 2 * HALO, bk), F32),
                        pltpu.VMEM((tm, D_MODEL), F32)],
        compiler_params=_cparams(("arbitrary", "arbitrary")),
        name="ffn",
    )(x, x, x, ng, sc, sh, g2, fg, w_up, w_up, cw, w_down)


PREP_ROWS = 256


def _perm_kernel(wt_ref, o_ref):
    dst = 0
    for name in _DST_ORDER:
        src, width = _SRC[name]
        if width >= 128:
            o_ref[:, dst:dst + width] = wt_ref[src:src + width, :].T.astype(BF16)
            dst += width
    assert dst == COL_GATES
    gate_rows = [wt_ref[_SRC[n][0]:_SRC[n][0] + _SRC[n][1], :]
                 for n in ("dn_a", "dn_b", "ssd_dt", "ml_i", "ml_f")]
    used = sum(r.shape[0] for r in gate_rows)
    gate_rows.append(jnp.zeros((128 - used, PREP_ROWS), F32))
    o_ref[:, COL_GATES:COL_GATES + 128] = jnp.concatenate(gate_rows, axis=0).T.astype(BF16)
    o_ref[:, COL_GATES + 128:N_PROJ] = jnp.zeros((PREP_ROWS, N_PROJ - COL_GATES - 128), BF16)


def _perm_w_in(w):
    assert (LANE_DN_A, LANE_DN_B, LANE_SSD_DT, LANE_ML_I, LANE_ML_F) == (0, 8, 16, 32, 40)
    depth, rows, _ = w.shape
    return pl.pallas_call(
        _perm_kernel,
        grid=(depth, rows // PREP_ROWS),
        in_specs=[pl.BlockSpec((None, D_IN, PREP_ROWS), lambda l, i: (l, 0, i))],
        out_specs=pl.BlockSpec((None, PREP_ROWS, N_PROJ), lambda l, i: (l, i, 0)),
        out_shape=jax.ShapeDtypeStruct((depth, rows, N_PROJ), BF16),
        compiler_params=_cparams(("arbitrary", "arbitrary")),
        name="w_in_layout",
    )(jnp.swapaxes(w, 1, 2))


def _lane_row(pairs):
    tile = jnp.zeros((8, 128), F32)
    for row, lane, vals in pairs:
        vals = vals.reshape(-1).astype(F32)
        tile = tile.at[row, lane:lane + vals.shape[0]].set(vals)
    return tile


def _block_diag(w):
    nb, bw = w.shape[1], w.shape[2]
    eye = jnp.eye(nb, dtype=w.dtype)
    full = jnp.einsum("dnjk,nm->dnjmk", w, eye)
    return full.reshape(2, nb * bw, nb * bw).astype(BF16)


def _ctx_to_cols(a, wcols):
    t, ch = a.shape
    return a.reshape(wcols, t // wcols, ch).transpose(1, 0, 2).reshape(t, ch)


def _ctx_from_cols(a, wcols):
    t, ch = a.shape
    return a.reshape(t // wcols, wcols, ch).transpose(1, 0, 2).reshape(t, ch)


CTX_COLS = 8


def kernel(x, c, ctx, c_ctx, ada_w, ada_b, norm1_g, norm2_g, w_in, dn_conv_w, dn_a_log, dn_dt_bias, dn_norm_g, ssd_conv_w, ssd_conv_b, ssd_a_log, ssd_dt_bias, ssd_d, ssd_norm_g, lru_conv_w, lru_conv_b, lru_w_a, lru_b_a, lru_w_i, lru_b_i, lru_lambda, ml_ig_b, ml_fg_b, ml_norm_g, w_out, ffn_w_up, ffn_conv_w, ffn_w_down, final_norm_g):
    assert x.shape[0] == 1 and c.shape[0] == 1
    lat, hctx = x[0], ctx[0]
    cc = jnp.zeros((8, D_MODEL), F32).at[0].set(c[0]).at[1].set(c_ctx)
    mods = _ada(cc, ada_w, ada_b)
    row = lambda v: v.reshape(1, -1).astype(F32)
    w_in_p = _perm_w_in(w_in)
    w_out_b = w_out.astype(BF16)
    w_up_b = ffn_w_up.astype(BF16)
    w_down_b = ffn_w_down.astype(BF16)

    for l in range(DEPTH):
        mod_l = mods[l, 0].reshape(N_MOD, 1, D_MODEL)
        mod_c = mods[l, 1].reshape(N_MOD, 1, D_MODEL)
        ng1, ng2 = row(norm1_g[l]), row(norm2_g[l])

        dn_par = _lane_row([(0, LANE_DN_A, dn_a_log[l]), (1, LANE_DN_A, dn_dt_bias[l])])
        ssd_par = _lane_row([(0, LANE_SSD_DT, ssd_dt_bias[l]), (1, LANE_SSD_DT, ssd_a_log[l])])
        ml_par = _lane_row([(0, LANE_ML_I, ml_ig_b[l]), (1, LANE_ML_F, ml_fg_b[l])])
        lru_args = (lru_conv_w[l], row(lru_conv_b[l]), _block_diag(lru_w_a[l]), lru_b_a[l],
                    _block_diag(lru_w_i[l]), lru_b_i[l], lru_lambda[l])
        dn_g = row(jnp.tile(dn_norm_g[l], DN_HEADS))
        sd_d = row(jnp.repeat(ssd_d[l], SSD_HEAD_DIM))
        sd_g, ml_g = row(ssd_norm_g[l]), row(ml_norm_g[l])

        conv_w = jnp.concatenate([ssd_conv_w[l], dn_conv_w[l]], axis=1)
        conv_b = jnp.concatenate([row(ssd_conv_b[l]), jnp.zeros((1, 3 * GROUP_W), F32)], axis=1)

        pc = _inproj(hctx, ng1, mod_c[1], mod_c[0], w_in_p, l, conv_w, conv_b)
        dn_cf, dn_cb, dn_s = _dn_scan(pc, dn_par, jnp.zeros((2 * DN_HEADS, DN_DK, DN_DV), F32))
        sd_cf, sd_cb, sd_s = _ssd_scan(
            pc, ssd_par, jnp.zeros((2 * SSD_GROUPS, SSD_STATE, GROUP_W // SSD_GROUPS), F32))
        ml_cf, ml_cb, ml_c, ml_m = _ml_scan(
            pc, ml_par, jnp.zeros((2 * ML_HEADS, ML_DK, 2 * ML_DV), F32), jnp.zeros((1, 128), F32))
        xc_cols = _ctx_to_cols(pc[:, COL_LRU_X:COL_LRU_X + LRU_W], CTX_COLS)
        lr_c, lr_s = _lru_scan(xc_cols, 0, CTX_COLS, *lru_args, jnp.zeros((2, LRU_W), F32))

        pl_ = _inproj(lat, ng1, mod_l[1], mod_l[0], w_in_p, l, conv_w, conv_b)
        dn_lf, dn_lb, _ = _dn_scan(pl_, dn_par, dn_s)
        sd_lf, sd_lb, _ = _ssd_scan(pl_, ssd_par, sd_s)
        ml_lf, ml_lb, _, _ = _ml_scan(pl_, ml_par, ml_c, ml_m)
        lr_l, _ = _lru_scan(pl_, COL_LRU_X // 128, GRID_W, *lru_args, lr_s)

        lat = _outproj(lat, mod_l[2], pl_, dn_lf, dn_lb, dn_g, sd_lf, sd_lb, sd_d, sd_g,
                       lr_l, ml_lf, ml_lb, ml_g, w_out_b, l)
        last = l == DEPTH - 1
        lat = _ffn(lat, ng2, mod_l[4], mod_l[3], mod_l[5], row(final_norm_g), w_up_b,
                   ffn_conv_w[l], w_down_b, l, last)

        if not last:
            lr_cn = _ctx_from_cols(lr_c, CTX_COLS)
            hctx = _outproj(hctx, mod_c[2], pc, dn_cf, dn_cb, dn_g, sd_cf, sd_cb, sd_d, sd_g,
                            lr_cn, ml_cf, ml_cb, ml_g, w_out_b, l)
            hctx = _ffn(hctx, ng2, mod_c[4], mod_c[3], mod_c[5], row(final_norm_g), w_up_b,
                        ffn_conv_w[l], w_down_b, l, False)

    return lat[None]
```

```python
import functools

import jax
import jax.numpy as jnp
from jax import lax
from jax.experimental import pallas as pl
from jax.experimental.pallas import tpu as pltpu

F32 = jnp.float32
BF16 = jnp.bfloat16

D_MODEL = 2048
DEPTH = 2
GRID_W = 64
GROUP_W = D_MODEL // 4
CHUNK = 64
SHORT_CONV = 5
FFN_CONV = 3
D_FF = ((8 * D_MODEL // 3 + 255) // 256) * 256
N_MOD = 6
EPS = 1e-6

DN_HEADS = 4
DN_DK = GROUP_W // DN_HEADS
DN_DV = GROUP_W // DN_HEADS
SSD_HEAD_DIM = 64
SSD_HEADS = GROUP_W // SSD_HEAD_DIM
SSD_GROUPS = 2
SSD_STATE = 128
LRU_W = GROUP_W
LRU_BLOCKS = 8
LRU_BW = LRU_W // LRU_BLOCKS
LRU_C = 8.0
ML_HEADS = 4
ML_DV = GROUP_W // ML_HEADS
ML_DK = ML_DV // 2

_SRC = {}
_off = 0
for _name, _w in (
    ("dn_q", 512), ("dn_k", 512), ("dn_v", 512), ("dn_z", 512), ("dn_a", 8), ("dn_b", 8),
    ("ssd_x", 512), ("ssd_z", 512), ("ssd_B", 256), ("ssd_C", 256), ("ssd_dt", 16),
    ("lru_x", 512), ("lru_g", 512),
    ("ml_q", 256), ("ml_k", 256), ("ml_v", 512), ("ml_o", 512), ("ml_i", 8), ("ml_f", 8),
):
    _SRC[_name] = (_off, _w)
    _off += _w
D_IN = _off

_DST_ORDER = ("ssd_x", "ssd_B", "ssd_C", "dn_q", "dn_k", "dn_v", "ml_q", "ml_k", "ml_v",
              "lru_x", "dn_z", "ssd_z", "lru_g", "ml_o", "dn_a", "dn_b", "ssd_dt", "ml_i", "ml_f")
N_PROJ = 6400
COL_SSD_XBC = 0
COL_DN_Q = 1024
COL_DN_K = 1536
COL_DN_V = 2048
N_CONV = 2560
COL_ML_QK = 2560
COL_ML_V = 3072
COL_LRU_X = 3584
COL_DN_Z = 4096
COL_SSD_Z = 4608
COL_LRU_G = 5120
COL_ML_O = 5632
COL_GATES = 6144
LANE_DN_A, LANE_DN_B, LANE_SSD_DT, LANE_ML_I, LANE_ML_F = 0, 8, 16, 32, 40

VMEM_LIMIT = 56 * 1024 * 1024
MXU_K = 256


def _cparams(sem):
    return pltpu.CompilerParams(dimension_semantics=sem, vmem_limit_bytes=VMEM_LIMIT)


_NN = (((1,), (0,)), ((), ()))
_NT = (((1,), (1,)), ((), ()))
_TN = (((0,), (0,)), ((), ()))


def _dg(a, b, dims):
    return lax.dot_general(a, b, dims, preferred_element_type=F32)


def _mm(a, b, dims=_NN):
    return _dg(a.astype(BF16), b.astype(BF16), dims)


def _split3(a):
    hi = a.astype(BF16)
    r = a - hi.astype(F32)
    mid = r.astype(BF16)
    return hi, mid, (r - mid.astype(F32)).astype(BF16)


def _mm_sel(sel, x):
    sb = sel.astype(BF16)
    x0, x1, x2 = _split3(x)
    return _dg(sb, x0, _NN) + (_dg(sb, x1, _NN) + _dg(sb, x2, _NN))


def _mm_spread(x, sel):
    sb = sel.astype(BF16)
    x0, x1, x2 = _split3(x)
    return _dg(x0, sb, _NN) + (_dg(x1, sb, _NN) + _dg(x2, sb, _NN))


def _silu(x):
    return x * jax.nn.sigmoid(x)


def _masks(d):
    ri = lax.broadcasted_iota(jnp.int32, (CHUNK, CHUNK), 0)
    ci = lax.broadcasted_iota(jnp.int32, (CHUNK, CHUNK), 1)
    if d == 0:
        return ri >= ci, ri > ci
    return ri <= ci, ri < ci


def _seg_decay(col, row, incl):
    return jnp.where(incl, jnp.exp(jnp.where(incl, col - row, 0.0)), 0.0)


ADA_TK = 256


def _ada_kernel(c_ref, w_ref, b_ref, o_ref):
    k = pl.program_id(1)

    @pl.when(k == 0)
    def _():
        o_ref[...] = jnp.broadcast_to(b_ref[...], o_ref.shape)

    cc = c_ref[...]
    act = (cc * jax.nn.sigmoid(cc)).astype(BF16)
    o_ref[...] += jnp.dot(act, w_ref[...].astype(BF16), preferred_element_type=F32)


def _ada(cc, ada_w, ada_b):
    n = N_MOD * D_MODEL
    return pl.pallas_call(
        _ada_kernel,
        grid=(DEPTH, D_MODEL // ADA_TK),
        in_specs=[
            pl.BlockSpec((8, ADA_TK), lambda l, k: (0, k)),
            pl.BlockSpec((None, ADA_TK, n), lambda l, k: (l, k, 0)),
            pl.BlockSpec((None, 1, n), lambda l, k: (l, 0, 0)),
        ],
        out_specs=pl.BlockSpec((None, 8, n), lambda l, k: (l, 0, 0)),
        out_shape=jax.ShapeDtypeStruct((DEPTH, 8, n), F32),
        compiler_params=_cparams(("arbitrary", "arbitrary")),
        name="ada",
    )(cc, ada_w, ada_b.reshape(DEPTH, 1, n))


def _norm_mod(xf, ng, sc, sh):
    y = xf * lax.rsqrt(jnp.mean(xf * xf, axis=-1, keepdims=True) + EPS) * ng
    return y * (1.0 + sc) + sh


HALO = 16
PROJ_TN = 1280
PROJ_SUB = 256


def _fill_halo_tile(h_scr, i, nt, x_ref, xp_ref, xn_ref, ng, sc, sh):
    tm = x_ref.shape[0]
    hp = jnp.where(i > 0, _norm_mod(xp_ref[...], ng, sc, sh), 0.0)
    hn = jnp.where(i < nt - 1, _norm_mod(xn_ref[...], ng, sc, sh), 0.0)
    h_scr[0:HALO, :] = hp.astype(BF16)
    h_scr[HALO:HALO + tm, :] = _norm_mod(x_ref[...], ng, sc, sh).astype(BF16)
    h_scr[HALO + tm:2 * HALO + tm, :] = hn.astype(BF16)


def _inproj_kernel(nt, x_ref, xp_ref, xn_ref, ng_ref, sc_ref, sh_ref, w_ref, cw_ref, cb_ref,
                   o_ref, h_scr):
    i = pl.program_id(0)
    j = pl.program_id(1)
    tm = x_ref.shape[0]

    @pl.when(j == 0)
    def _():
        _fill_halo_tile(h_scr, i, nt, x_ref, xp_ref, xn_ref, ng_ref[...], sc_ref[...], sh_ref[...])

    for jj in range(N_CONV // PROJ_TN):
        @pl.when(j == jj)
        def _():
            rows = tm + 2 * HALO
            for c in range(PROJ_TN // PROJ_SUB):
                cs = slice(c * PROJ_SUB, (c + 1) * PROJ_SUB)
                y = jnp.dot(h_scr[...], w_ref[:, cs], preferred_element_type=F32)
                acc = cb_ref[:, cs] + cw_ref[2:3, cs] * y[HALO:HALO + tm]
                for tap in (0, 1, 3, 4):
                    shifted = pltpu.roll(y, (2 - tap) % rows, axis=0)
                    acc = acc + cw_ref[tap:tap + 1, cs] * shifted[HALO:HALO + tm]
                act = _silu(acc)
                for half in range(PROJ_SUB // 128):
                    col = jj * PROJ_TN + c * PROJ_SUB + half * 128
                    a = act[:, half * 128:(half + 1) * 128]
                    if COL_DN_Q <= col < COL_DN_V:
                        a = a * lax.rsqrt(jnp.sum(a * a, axis=-1, keepdims=True) + EPS)
                        if col < COL_DN_K:
                            a = a * DN_DK ** -0.5
                    o_ref[:, c * PROJ_SUB + half * 128:c * PROJ_SUB + (half + 1) * 128] = a

    @pl.when(j >= N_CONV // PROJ_TN)
    def _():
        o_ref[...] = jnp.dot(h_scr[HALO:HALO + tm, :], w_ref[...], preferred_element_type=F32)


def _halo_specs(t, tm):
    per, nhb = tm // HALO, t // HALO
    return [pl.BlockSpec((tm, D_MODEL), lambda i, k: (i, 0)),
            pl.BlockSpec((HALO, D_MODEL), lambda i, k: (jnp.maximum(i * per - 1, 0), 0)),
            pl.BlockSpec((HALO, D_MODEL), lambda i, k: (jnp.minimum((i + 1) * per, nhb - 1), 0))]


def _inproj(x, ng, sc, sh, w, l, cw, cb):
    t = x.shape[0]
    tm = min(t, 1024)
    tn = PROJ_TN
    nconv = N_CONV // tn
    row = lambda i, j: (0, 0)
    return pl.pallas_call(
        functools.partial(_inproj_kernel, t // tm),
        grid=(t // tm, N_PROJ // tn),
        in_specs=_halo_specs(t, tm) + [
            pl.BlockSpec((1, D_MODEL), row),
            pl.BlockSpec((1, D_MODEL), row),
            pl.BlockSpec((1, D_MODEL), row),
            pl.BlockSpec((None, D_MODEL, tn), lambda i, j: (l, 0, j)),
            pl.BlockSpec((SHORT_CONV, tn), lambda i, j: (0, jnp.minimum(j, nconv - 1))),
            pl.BlockSpec((1, tn), lambda i, j: (0, jnp.minimum(j, nconv - 1))),
        ],
        out_specs=pl.BlockSpec((tm, tn), lambda i, j: (i, j)),
        out_shape=jax.ShapeDtypeStruct((t, N_PROJ), F32),
        scratch_shapes=[pltpu.VMEM((tm + 2 * HALO, D_MODEL), BF16)],
        compiler_params=_cparams(("arbitrary", "arbitrary")),
        name="inproj",
    )(x, x, x, ng, sc, sh, w, cw, cb)


SUBS = 4
ROWS = SUBS * CHUNK


def _sub_order(d):
    return list(range(SUBS)) if d == 0 else list(range(SUBS - 1, -1, -1))


def _col_spec(nb, width, col, rev):
    blk = col // width
    assert blk * width == col
    if rev:
        return pl.BlockSpec((ROWS, width), lambda i: (nb - 1 - i, blk))
    return pl.BlockSpec((ROWS, width), lambda i: (i, blk))


def _gate_spec(nb, rev):
    return _col_spec(nb, 128, COL_GATES, rev)


def _out_spec(nb, width, rev):
    if rev:
        return pl.BlockSpec((ROWS, width), lambda i: (nb - 1 - i, 0))
    return pl.BlockSpec((ROWS, width), lambda i: (i, 0))


def _const_spec(shape):
    nd = len(shape)
    return pl.BlockSpec(shape, lambda i: (0,) * nd)


def _cumsum_dir(x, d):
    ri = lax.broadcasted_iota(jnp.int32, (ROWS, ROWS), 0)
    ci = lax.broadcasted_iota(jnp.int32, (ROWS, ROWS), 1)
    incl = (ri >= ci) if d == 0 else (ri <= ci)
    return _mm_sel(jnp.where((ri // CHUNK == ci // CHUNK) & incl, 1.0, 0.0), x)


def _dn_kernel(qf, kf, vf, qb, kb_, vb_, gf, gb, par_ref, s0_ref, of_ref, ob_ref, s_ref):
    i = pl.program_id(0)

    @pl.when(i == 0)
    def _():
        s_ref[...] = s0_ref[...]

    eye = jnp.where(_masks(0)[0] & _masks(1)[0], 1.0, 0.0).astype(F32)
    hd = []
    for d, (q_ref, k_ref, v_ref, g_ref, o_ref) in enumerate(((qf, kf, vf, gf, of_ref),
                                                             (qb, kb_, vb_, gb, ob_ref))):
        incl, strict = _masks(d)
        last = CHUNK - 1 if d == 0 else 0
        gates = g_ref[...]
        g_all = -jnp.exp(par_ref[0:1, :]) * jax.nn.softplus(gates + par_ref[1:2, :])
        beta_all = jax.nn.sigmoid(gates)
        gc = _cumsum_dir(g_all, d)
        gct = gc.T
        for pos, sub in enumerate(_sub_order(d)):
            rs = slice(sub * CHUNK, (sub + 1) * CHUNK)
            for h in range(DN_HEADS):
                lane = LANE_DN_A + d * DN_HEADS + h
                blane = LANE_DN_B + d * DN_HEADS + h
                k = k_ref[rs, h * DN_DK:(h + 1) * DN_DK]
                gcc = gc[rs, lane:lane + 1]
                beta = beta_all[rs, blane:blane + 1]
                hd.append(dict(
                    q=q_ref[rs, h * DN_DK:(h + 1) * DN_DK], k=k, kb=k * beta,
                    vb=v_ref[rs, h * DN_DV:(h + 1) * DN_DV] * beta, gcc=gcc,
                    tot=gc[sub * CHUNK + last:sub * CHUNK + last + 1, lane:lane + 1],
                    strict=strict, decay=_seg_decay(gcc, gct[lane:lane + 1, rs], incl),
                    pos=pos, rs=rs, o_ref=o_ref, osl=slice(h * DN_DV, (h + 1) * DN_DV),
                    sidx=d * DN_HEADS + h))

    nmat = [jnp.where(x["strict"], _mm(x["kb"], x["k"], _NT) * x["decay"], 0.0) for x in hd]
    attn = [_mm(x["q"], x["k"], _NT) * x["decay"] for x in hd]
    ri = lax.broadcasted_iota(jnp.int32, (CHUNK, CHUNK), 0)
    ci = lax.broadcasted_iota(jnp.int32, (CHUNK, CHUNK), 1)
    base = 8
    pw = [jnp.where(ri // base == ci // base, n, 0.0) for n in nmat]
    tinv = [eye - p for p in pw]
    for _ in range(2):
        pw = [_mm(p, p) for p in pw]
        tinv = [t + _mm(t, p) for t, p in zip(tinv, pw)]
    size = base
    while size < CHUNK:
        sib = (ri // (2 * size) == ci // (2 * size)) & (ri // size != ci // size)
        tc = [_mm(t, jnp.where(sib, n, 0.0)) for t, n in zip(tinv, nmat)]
        tinv = [t - _mm(c, t) for t, c in zip(tinv, tc)]
        size *= 2
    sol = [_mm(t, jnp.concatenate([x["vb"], x["kb"] * jnp.exp(x["gcc"])], axis=-1))
           for t, x in zip(tinv, hd)]
    state = {j: s_ref[j] for j in range(2 * DN_HEADS)}
    for pos in range(SUBS):
        cur = [(x, so, a) for x, so, a in zip(hd, sol, attn) if x["pos"] == pos]
        v_new = [so[:, :DN_DV] - _mm(so[:, DN_DV:], state[x["sidx"]]) for x, so, _ in cur]
        for (x, _, a), vn in zip(cur, v_new):
            x["o_ref"][x["rs"], x["osl"]] = (_mm(x["q"] * jnp.exp(x["gcc"]), state[x["sidx"]])
                                             + _mm(a, vn))
        for (x, _, _), vn in zip(cur, v_new):
            state[x["sidx"]] = (state[x["sidx"]] * jnp.exp(x["tot"])
                                + _mm(x["k"] * jnp.exp(x["tot"] - x["gcc"]), vn, _TN))
    for j in range(2 * DN_HEADS):
        s_ref[j] = state[j]


def _dn_scan(p, par, s0):
    t = p.shape[0]
    nb = t // ROWS
    out = jax.ShapeDtypeStruct((t, GROUP_W), F32)
    qkv = lambda rev: [_col_spec(nb, GROUP_W, col, rev) for col in (COL_DN_Q, COL_DN_K, COL_DN_V)]
    return pl.pallas_call(
        _dn_kernel,
        grid=(nb,),
        in_specs=(qkv(False) + qkv(True)
                  + [_gate_spec(nb, False), _gate_spec(nb, True), _const_spec((8, 128)),
                     _const_spec((2 * DN_HEADS, DN_DK, DN_DV))]),
        out_specs=[_out_spec(nb, GROUP_W, False), _out_spec(nb, GROUP_W, True),
                   _const_spec((2 * DN_HEADS, DN_DK, DN_DV))],
        out_shape=[out, out, jax.ShapeDtypeStruct((2 * DN_HEADS, DN_DK, DN_DV), F32)],
        compiler_params=_cparams(("arbitrary",)),
        name="dn_scan",
    )(p, p, p, p, p, p, p, p, par, s0)


def _ssd_kernel(xf, xb, gf, gb, par_ref, s0_ref, of_ref, ob_ref, s_ref):
    i = pl.program_id(0)

    @pl.when(i == 0)
    def _():
        s_ref[...] = s0_ref[...]

    hp = SSD_HEADS * SSD_HEAD_DIM
    gw = hp // SSD_GROUPS
    rep = SSD_HEADS // SSD_GROUPS
    items = []
    for d, (x_ref, g_ref, o_ref) in enumerate(((xf, gf, of_ref), (xb, gb, ob_ref))):
        incl, _ = _masks(d)
        last = CHUNK - 1 if d == 0 else 0
        gates = g_ref[...]
        dt_all = jax.nn.softplus(gates + par_ref[0:1, :])
        a_all = -jnp.exp(par_ref[1:2, :]) * dt_all
        acs = _cumsum_dir(a_all, d)
        acst = acs.T
        er = lax.broadcasted_iota(jnp.int32, (128, hp), 0)
        ec = lax.broadcasted_iota(jnp.int32, (128, hp), 1)
        expand = jnp.where(er == LANE_SSD_DT + d * SSD_HEADS + ec // SSD_HEAD_DIM, 1.0, 0.0)
        both_x = _mm_spread(jnp.concatenate([dt_all, acs], axis=0), expand)
        dt_x, acs_x = both_x[:ROWS], both_x[ROWS:]
        xd_all = x_ref[:, :hp] * dt_x
        for pos, sub in enumerate(_sub_order(d)):
            rs = slice(sub * CHUNK, (sub + 1) * CHUNK)
            tot_x = acs_x[sub * CHUNK + last:sub * CHUNK + last + 1, :]
            xd = xd_all[rs]
            xdw = xd * jnp.exp(tot_x - acs_x[rs])
            for g in range(SSD_GROUPS):
                bm = x_ref[rs, hp + g * SSD_STATE:hp + (g + 1) * SSD_STATE]
                cm = x_ref[rs, hp + SSD_GROUPS * SSD_STATE + g * SSD_STATE:
                           hp + SSD_GROUPS * SSD_STATE + (g + 1) * SSD_STATE]
                cb = _mm(cm, bm, _NT)
                gs = slice(g * gw, (g + 1) * gw)
                y_diag = []
                for hh in range(rep):
                    h = g * rep + hh
                    lane = LANE_SSD_DT + d * SSD_HEADS + h
                    hs = slice(h * SSD_HEAD_DIM, (h + 1) * SSD_HEAD_DIM)
                    lmat = _seg_decay(acs[rs, lane:lane + 1], acst[lane:lane + 1, rs], incl)
                    y_diag.append(_mm(cb * lmat, xd[:, hs]))
                items.append(dict(
                    pos=pos, rs=rs, gs=gs, o_ref=o_ref, sidx=d * SSD_GROUPS + g, cm=cm,
                    y_diag=jnp.concatenate(y_diag, axis=-1), off_scale=jnp.exp(acs_x[rs, gs]),
                    dec=jnp.exp(tot_x[:, gs]), local=_mm(bm, xdw[:, gs], _TN)))
    state = {j: s_ref[j] for j in range(2 * SSD_GROUPS)}
    for pos in range(SUBS):
        for x in [x for x in items if x["pos"] == pos]:
            s = state[x["sidx"]]
            x["o_ref"][x["rs"], x["gs"]] = _mm(x["cm"], s) * x["off_scale"] + x["y_diag"]
            state[x["sidx"]] = s * x["dec"] + x["local"]
    for j in range(2 * SSD_GROUPS):
        s_ref[j] = state[j]


def _ssd_scan(p, par, s0):
    t = p.shape[0]
    nb = t // ROWS
    wx = 2 * GROUP_W
    hp = SSD_HEADS * SSD_HEAD_DIM
    st = (2 * SSD_GROUPS, SSD_STATE, hp // SSD_GROUPS)
    out = jax.ShapeDtypeStruct((t, hp), F32)
    return pl.pallas_call(
        _ssd_kernel,
        grid=(nb,),
        in_specs=[_col_spec(nb, wx, COL_SSD_XBC, False), _col_spec(nb, wx, COL_SSD_XBC, True),
                  _gate_spec(nb, False), _gate_spec(nb, True), _const_spec((8, 128)),
                  _const_spec(st)],
        out_specs=[_out_spec(nb, hp, False), _out_spec(nb, hp, True), _const_spec(st)],
        out_shape=[out, out, jax.ShapeDtypeStruct(st, F32)],
        compiler_params=_cparams(("arbitrary",)),
        name="ssd_scan",
    )(p, p, p, p, par, s0)


def _cummax_dir(x, d):
    rows = x.shape[0]
    sub = lax.broadcasted_iota(jnp.int32, x.shape, 0) % CHUNK
    sh = 1
    while sh < CHUNK:
        if d == 0:
            x = jnp.where(sub >= sh, jnp.maximum(x, pltpu.roll(x, sh, axis=0)), x)
        else:
            x = jnp.where(sub < CHUNK - sh, jnp.maximum(x, pltpu.roll(x, rows - sh, axis=0)), x)
        sh *= 2
    return x


ML_PACK = 3 * CHUNK + 8


def _ml_kernel(qkf, vf, qkb, vb, gf, gb, par_ref, c0_ref, m0_ref, of_ref, ob_ref, c_ref, m_ref):
    i = pl.program_id(0)

    @pl.when(i == 0)
    def _():
        c_ref[...] = c0_ref[...]
        m_ref[...] = m0_ref[...]

    lane = lax.broadcasted_iota(jnp.int32, (1, 128), 1)
    ones = jnp.ones((CHUNK, ML_DV), F32)
    hd, m_rows = [], []
    for d, (qk_ref, v_ref, g_ref, o_ref) in enumerate(((qkf, vf, gf, of_ref), (qkb, vb, gb, ob_ref))):
        incl, _ = _masks(d)
        last = CHUNK - 1 if d == 0 else 0
        lane0 = LANE_ML_F + d * ML_HEADS
        valid = (lane >= lane0) & (lane < lane0 + ML_HEADS)
        gates = g_ref[...]
        ig_all = pltpu.roll(gates + par_ref[0:1, :], LANE_ML_F - LANE_ML_I, axis=1)
        b_all = _cumsum_dir(jax.nn.log_sigmoid(gates + par_ref[1:2, :]), d)
        r_all = ig_all - b_all
        cmax = _cummax_dir(r_all, d)
        rt = r_all.T
        m_cur = m_ref[...]
        packed, wk_src = [], {}
        for sub in _sub_order(d):
            rs = slice(sub * CHUNK, (sub + 1) * CHUNK)
            b = b_all[rs]
            m_all = b + jnp.maximum(m_cur, cmax[rs])
            b_last = b_all[sub * CHUNK + last:sub * CHUNK + last + 1, :]
            log_g = b_last - b + ig_all[rs]
            m_new = jnp.maximum(b_last + m_cur, jnp.max(log_g, axis=0, keepdims=True))
            packed += [b - m_all, jnp.exp(b + m_cur - m_all), jnp.exp(-m_all),
                       jnp.broadcast_to(jnp.exp(b_last + m_cur - m_new), (8, 128))]
            wk_src[sub] = jnp.exp(log_g - m_new)
            m_cur = m_new
        m_rows.append(jnp.where(valid, m_cur, 0.0))
        sr = lax.broadcasted_iota(jnp.int32, (128, ML_HEADS * ML_DV), 0)
        sc = lax.broadcasted_iota(jnp.int32, (128, ML_HEADS * ML_DV), 1)
        wide = _mm_spread(jnp.where(valid, jnp.concatenate(packed, axis=0), 0.0),
                          jnp.where(sr == lane0 + sc // ML_DV, 1.0, 0.0))
        sr = lax.broadcasted_iota(jnp.int32, (128, ML_HEADS * ML_DK), 0)
        sc = lax.broadcasted_iota(jnp.int32, (128, ML_HEADS * ML_DK), 1)
        wkc = _mm_spread(
            jnp.where(valid, jnp.concatenate([wk_src[s] for s in range(SUBS)], axis=0), 0.0),
            jnp.where(sr == lane0 + sc // ML_DK, 1.0, 0.0))
        for pos, sub in enumerate(_sub_order(d)):
            rs = slice(sub * CHUNK, (sub + 1) * CHUNK)
            r0 = pos * ML_PACK
            for h in range(ML_HEADS):
                ws = slice(h * ML_DV, (h + 1) * ML_DV)
                k = qk_ref[rs, ML_HEADS * ML_DK + h * ML_DK:ML_HEADS * ML_DK + (h + 1) * ML_DK]
                hd.append(dict(
                    q=qk_ref[rs, h * ML_DK:(h + 1) * ML_DK] * ML_DK ** -0.5, k=k,
                    wk=wkc[rs, h * ML_DK:(h + 1) * ML_DK] * k,
                    v_aug=jnp.concatenate([v_ref[rs, h * ML_DV:(h + 1) * ML_DV], ones], axis=-1),
                    dmat=jnp.where(incl, jnp.exp(wide[r0:r0 + CHUNK, h * ML_DV:h * ML_DV + CHUNK]
                                                 + rt[lane0 + h:lane0 + h + 1, rs]), 0.0),
                    w_inter=wide[r0 + CHUNK:r0 + 2 * CHUNK, ws],
                    enm=wide[r0 + 2 * CHUNK:r0 + 3 * CHUNK, ws],
                    dec=wide[r0 + 3 * CHUNK:r0 + 3 * CHUNK + 1, ws],
                    pos=pos, rs=rs, sd=d * ML_HEADS + h, o_ref=o_ref, osl=ws))

    s = [_mm(x["q"], x["k"], _NT) * x["dmat"] for x in hd]
    intra = [_mm(s_h, x["v_aug"]) for s_h, x in zip(s, hd)]
    upd = [_mm(x["wk"], x["v_aug"], _TN) for x in hd]
    state = {j: c_ref[j] for j in range(2 * ML_HEADS)}
    for pos in range(SUBS):
        cur = [(x, ia, up) for x, ia, up in zip(hd, intra, upd) if x["pos"] == pos]
        inter = [_mm(x["q"], state[x["sd"]]) for x, _, _ in cur]
        for (x, ia, up), ie in zip(cur, inter):
            num = x["w_inter"] * ie[:, :ML_DV] + ia[:, :ML_DV]
            den = x["w_inter"] * ie[:, ML_DV:] + ia[:, ML_DV:]
            x["o_ref"][x["rs"], x["osl"]] = num / jnp.maximum(jnp.abs(den), x["enm"])
            state[x["sd"]] = jnp.concatenate([x["dec"], x["dec"]], axis=-1) * state[x["sd"]] + up
    for j in range(2 * ML_HEADS):
        c_ref[j] = state[j]
    m_ref[...] = m_rows[0] + m_rows[1]


def _ml_scan(p, par, c0, m0):
    t = p.shape[0]
    nb = t // ROWS
    cs, ms = (2 * ML_HEADS, ML_DK, 2 * ML_DV), (1, 128)
    out = jax.ShapeDtypeStruct((t, GROUP_W), F32)
    qkv = lambda rev: [_col_spec(nb, GROUP_W, COL_ML_QK, rev), _col_spec(nb, GROUP_W, COL_ML_V, rev)]
    return pl.pallas_call(
        _ml_kernel,
        grid=(nb,),
        in_specs=(qkv(False) + qkv(True)
                  + [_gate_spec(nb, False), _gate_spec(nb, True), _const_spec((8, 128)),
                     _const_spec(cs), _const_spec(ms)]),
        out_specs=[_out_spec(nb, GROUP_W, False), _out_spec(nb, GROUP_W, True),
                   _const_spec(cs), _const_spec(ms)],
        out_shape=[out, out, jax.ShapeDtypeStruct(cs, F32), jax.ShapeDtypeStruct(ms, F32)],
        compiler_params=_cparams(("arbitrary",)),
        name="ml_scan",
    )(p, p, p, p, p, p, par, c0, m0)


def _lru_kernel(rows, wcols, x_ref, cw_ref, cb_ref, wa_ref, ba_ref, wi_ref, bi_ref, lam_ref,
                h0_ref, o_ref, hfin_ref, xp_scr, a_scr, b_scr):
    t = rows * wcols
    sub = lax.broadcasted_iota(jnp.int32, (wcols, 128), 0)

    def shift_down(a):
        return jnp.where(sub >= 1, pltpu.roll(a, 1, axis=0), 0.0)

    def shift_up(a):
        return jnp.where(sub < wcols - 1, pltpu.roll(a, wcols - 1, axis=0), 0.0)

    def slab(r):
        return pl.ds(pl.multiple_of(r * wcols, wcols), wcols)

    xp_scr[pl.ds(2 * wcols, t), :] = x_ref[...]
    xp_scr[pl.ds(0, wcols), :] = shift_down(x_ref[pl.ds((rows - 2) * wcols, wcols), :])
    xp_scr[pl.ds(wcols, wcols), :] = shift_down(x_ref[pl.ds((rows - 1) * wcols, wcols), :])
    xp_scr[pl.ds((rows + 2) * wcols, wcols), :] = shift_up(x_ref[pl.ds(0, wcols), :])
    xp_scr[pl.ds((rows + 3) * wcols, wcols), :] = shift_up(x_ref[pl.ds(wcols, wcols), :])

    rb = 256 if t % 256 == 0 else t
    for d in range(2):
        sp_lam = jax.nn.softplus(-lam_ref[d:d + 1, :])

        def gate_body(blk, carry):
            base = pl.multiple_of(blk * rb, rb)
            xr = cb_ref[...] + cw_ref[0:1, :] * xp_scr[pl.ds(base, rb), :]
            for j in range(1, SHORT_CONV):
                xr = xr + cw_ref[j:j + 1, :] * xp_scr[pl.ds(pl.multiple_of(base + j * wcols, 8), rb), :]
            xb = xr.astype(BF16)
            r = jax.nn.sigmoid(jnp.dot(xb, wa_ref[d], preferred_element_type=F32) + ba_ref[d:d + 1, :])
            ii = jax.nn.sigmoid(jnp.dot(xb, wi_ref[d], preferred_element_type=F32) + bi_ref[d:d + 1, :])
            log_a = -LRU_C * r * sp_lam
            a = jnp.exp(log_a)
            b = jnp.sqrt(-jnp.tanh(log_a) * (a * a + 1.0)) * ii * xr
            a_scr[pl.ds(base, rb), :] = a
            b_scr[pl.ds(base, rb), :] = b
            return carry

        lax.fori_loop(0, t // rb, gate_body, 0)

        def scan_body(step, carry):
            h, acc = carry
            r = step if d == 0 else rows - 1 - step
            a = a_scr[slab(r), :]
            h = a * h + b_scr[slab(r), :]
            acc = a * acc
            b_scr[slab(r), :] = h
            a_scr[slab(r), :] = acc
            return h, acc

        h_end, a_end = lax.fori_loop(
            0, rows, scan_body, (jnp.zeros((wcols, 128), F32), jnp.ones((wcols, 128), F32)))

        sh = 1
        while sh < wcols:
            if d == 0:
                valid = sub >= sh
                a_sh, h_sh = pltpu.roll(a_end, sh, axis=0), pltpu.roll(h_end, sh, axis=0)
            else:
                valid = sub < wcols - sh
                a_sh, h_sh = pltpu.roll(a_end, wcols - sh, axis=0), pltpu.roll(h_end, wcols - sh, axis=0)
            h_end = jnp.where(valid, a_end * h_sh + h_end, h_end)
            a_end = jnp.where(valid, a_end * a_sh, a_end)
            sh *= 2
        h0 = h0_ref[d:d + 1, :]
        h_full = h_end + a_end * h0
        if d == 0:
            carry_in = jnp.where(sub >= 1, pltpu.roll(h_full, 1, axis=0), h0)
            hfin_ref[0:1, :] = h_full[wcols - 1:wcols, :]
        else:
            carry_in = jnp.where(sub < wcols - 1, pltpu.roll(h_full, wcols - 1, axis=0), h0)
            hfin_ref[1:2, :] = h_full[0:1, :]

        def fix_body(r, carry):
            hv = b_scr[slab(r), :] + a_scr[slab(r), :] * carry_in
            if d == 0:
                o_ref[slab(r), :] = hv
            else:
                o_ref[slab(r), :] += hv
            return carry

        lax.fori_loop(0, rows, fix_body, 0)


def _lru_scan(x, colblk, wcols, cw, cb, wa, ba, wi, bi, lam, h0):
    t = x.shape[0]
    rows = t // wcols
    nt = LRU_W // 128
    vec = lambda r: pl.BlockSpec((r, 128), lambda j: (0, j))
    wspec = pl.BlockSpec((2, 128, 128), lambda j: (0, j, j))
    return pl.pallas_call(
        functools.partial(_lru_kernel, rows, wcols),
        grid=(nt,),
        in_specs=[pl.BlockSpec((t, 128), lambda j: (0, colblk + j)),
                  vec(SHORT_CONV), vec(1), wspec, vec(2), wspec, vec(2), vec(2), vec(2)],
        out_specs=[pl.BlockSpec((t, 128), lambda j: (0, j)), vec(2)],
        out_shape=[jax.ShapeDtypeStruct((t, LRU_W), F32), jax.ShapeDtypeStruct((2, LRU_W), F32)],
        scratch_shapes=[pltpu.VMEM((t + 4 * wcols, 128), F32), pltpu.VMEM((t, 128), F32),
                        pltpu.VMEM((t, 128), F32)],
        compiler_params=_cparams(("arbitrary",)),
        name="lru_scan",
    )(x, cw, cb, wa, ba, wi, bi, lam, h0)


def _rms_groups(x, g, width):
    parts = []
    for s in range(0, x.shape[-1], width):
        xs = x[:, s:s + width]
        parts.append(xs * lax.rsqrt(jnp.mean(xs * xs, axis=-1, keepdims=True) + EPS) * g[:, s:s + width])
    return parts


def _outproj_kernel(x_ref, g1_ref, dnf, dnb, dnz, dng, sdf, sdb, sdx, sdz, sdd, sdg,
                    lrh, lrg, mlf, mlb, mlo, mlg, w_ref, o_ref):
    def slabs(parts, base):
        per = MXU_K // parts[0].shape[-1]
        for s in range(0, len(parts), per):
            yield jnp.concatenate(parts[s:s + per], axis=-1), base + (s // per) * MXU_K

    pieces = []
    z = dnz[...]
    dn = [part * _silu(z[:, h * DN_DV:(h + 1) * DN_DV])
          for h, part in enumerate(_rms_groups(dnf[...] + dnb[...], dng[...], DN_DV))]
    pieces += list(slabs(dn, 0))
    y = (sdf[...] + sdb[...] + sdd[...] * sdx[...]) * _silu(sdz[...])
    pieces += list(slabs(_rms_groups(y, sdg[...], GROUP_W // SSD_GROUPS), GROUP_W))
    lr = lrh[...] * jax.nn.gelu(lrg[...])
    pieces += list(slabs([lr[:, s:s + MXU_K] for s in range(0, GROUP_W, MXU_K)], 2 * GROUP_W))
    o = mlo[...]
    ml = [part * jax.nn.sigmoid(o[:, h * ML_DV:(h + 1) * ML_DV])
          for h, part in enumerate(_rms_groups(mlf[...] + mlb[...], mlg[...], ML_DV))]
    pieces += list(slabs(ml, 3 * GROUP_W))
    acc = jnp.zeros(o_ref.shape, F32)
    for slab, row0 in pieces:
        acc += _mm(slab, w_ref[row0:row0 + MXU_K, :])
    o_ref[...] = x_ref[...] + g1_ref[...] * acc


def _outproj(x, g1, p, dn_f, dn_b, dn_g, sd_f, sd_b, sd_d, sd_g, lr_h, ml_f, ml_b, ml_g, w, l):
    t = x.shape[0]
    tm = min(t, 256)
    tok = pl.BlockSpec((tm, GROUP_W), lambda i: (i, 0))
    pcol = lambda col: pl.BlockSpec((tm, GROUP_W), lambda i: (i, col // GROUP_W))
    vec = lambda n: pl.BlockSpec((1, n), lambda i: (0, 0))
    return pl.pallas_call(
        _outproj_kernel,
        grid=(t // tm,),
        in_specs=[pl.BlockSpec((tm, D_MODEL), lambda i: (i, 0)), vec(D_MODEL),
                  tok, tok, pcol(COL_DN_Z), vec(GROUP_W),
                  tok, tok, pcol(COL_SSD_XBC), pcol(COL_SSD_Z), vec(GROUP_W), vec(GROUP_W),
                  tok, pcol(COL_LRU_G),
                  tok, tok, pcol(COL_ML_O), vec(GROUP_W),
                  pl.BlockSpec((None, D_MODEL, D_MODEL), lambda i: (l, 0, 0))],
        out_specs=pl.BlockSpec((tm, D_MODEL), lambda i: (i, 0)),
        out_shape=jax.ShapeDtypeStruct((t, D_MODEL), F32),
        compiler_params=_cparams(("arbitrary",)),
        name="outproj",
    )(x, g1, dn_f, dn_b, p, dn_g, sd_f, sd_b, p, p, sd_d, sd_g, lr_h, p, ml_f, ml_b, p, ml_g, w)


def _ffn_kernel(nt, nk, final, x_ref, xp_ref, xn_ref, ng_ref, sc_ref, sh_ref, g2_ref, fg_ref,
                wu_ref, wg_ref, cw_ref, wd_ref, o_ref, h_scr, g_scr, acc_scr):
    i = pl.program_id(0)
    k = pl.program_id(1)
    tm = x_ref.shape[0]

    @pl.when(k == 0)
    def _():
        _fill_halo_tile(h_scr, i, nt, x_ref, xp_ref, xn_ref, ng_ref[...], sc_ref[...], sh_ref[...])
        acc_scr[...] = jnp.zeros_like(acc_scr)

    u = jnp.dot(h_scr[HALO:HALO + tm, :], wu_ref[...], preferred_element_type=F32)
    g_scr[...] = jnp.dot(h_scr[...], wg_ref[...], preferred_element_type=F32)
    conv = (cw_ref[0:1, :] * g_scr[pl.ds(HALO - 1, tm), :]
            + cw_ref[1:2, :] * g_scr[pl.ds(HALO, tm), :]
            + cw_ref[2:3, :] * g_scr[pl.ds(HALO + 1, tm), :])
    acc_scr[...] += jnp.dot((_silu(conv) * u).astype(BF16), wd_ref[...], preferred_element_type=F32)

    @pl.when(k == nk - 1)
    def _():
        y = x_ref[...] + g2_ref[...] * acc_scr[...]
        if final:
            y = y * lax.rsqrt(jnp.mean(y * y, axis=-1, keepdims=True) + EPS) * fg_ref[...]
        o_ref[...] = y


def _ffn(x, ng, sc, sh, g2, fg, w_up, cw, w_down, l, final):
    t = x.shape[0]
    tm = min(t, 512)
    bk = 512
    nt, nk = t // tm, D_FF // bk
    vec = pl.BlockSpec((1, D_MODEL), lambda i, k: (0, 0))
    return pl.pallas_call(
        functools.partial(_ffn_kernel, nt, nk, final),
        grid=(nt, nk),
        in_specs=_halo_specs(t, tm) + [
                  vec, vec, vec, vec, vec,
                  pl.BlockSpec((None, D_MODEL, bk), lambda i, k: (l, 0, k)),
                  pl.BlockSpec((None, D_MODEL, bk), lambda i, k: (l, 0, nk + k)),
                  pl.BlockSpec((FFN_CONV, bk), lambda i, k: (0, k)),
                  pl.BlockSpec((None, bk, D_MODEL), lambda i, k: (l, k, 0))],
        out_specs=pl.BlockSpec((tm, D_MODEL), lambda i, k: (i, 0)),
        out_shape=jax.ShapeDtypeStruct((t, D_MODEL), F32),
        scratch_shapes=[pltpu.VMEM((tm + 2 * HALO, D_MODEL), BF16),
                        pltpu.VMEM((tm + 2 * HALO, bk), F32),
                        pltpu.VMEM((tm, D_MODEL), F32)],
        compiler_params=_cparams(("arbitrary", "arbitrary")),
        name="ffn",
    )(x, x, x, ng, sc, sh, g2, fg, w_up, w_up, cw, w_down)


PREP_ROWS = 256


def _perm_kernel(wt_ref, o_ref):
    dst = 0
    for name in _DST_ORDER:
        src, width = _SRC[name]
        if width >= 128:
            o_ref[:, dst:dst + width] = wt_ref[src:src + width, :].T.astype(BF16)
            dst += width
    assert dst == COL_GATES
    gate_rows = [wt_ref[_SRC[n][0]:_SRC[n][0] + _SRC[n][1], :]
                 for n in ("dn_a", "dn_b", "ssd_dt", "ml_i", "ml_f")]
    used = sum(r.shape[0] for r in gate_rows)
    gate_rows.append(jnp.zeros((128 - used, PREP_ROWS), F32))
    o_ref[:, COL_GATES:COL_GATES + 128] = jnp.concatenate(gate_rows, axis=0).T.astype(BF16)
    o_ref[:, COL_GATES + 128:N_PROJ] = jnp.zeros((PREP_ROWS, N_PROJ - COL_GATES - 128), BF16)


def _perm_w_in(w):
    assert (LANE_DN_A, LANE_DN_B, LANE_SSD_DT, LANE_ML_I, LANE_ML_F) == (0, 8, 16, 32, 40)
    depth, rows, _ = w.shape
    return pl.pallas_call(
        _perm_kernel,
        grid=(depth, rows // PREP_ROWS),
        in_specs=[pl.BlockSpec((None, D_IN, PREP_ROWS), lambda l, i: (l, 0, i))],
        out_specs=pl.BlockSpec((None, PREP_ROWS, N_PROJ), lambda l, i: (l, i, 0)),
        out_shape=jax.ShapeDtypeStruct((depth, rows, N_PROJ), BF16),
        compiler_params=_cparams(("arbitrary", "arbitrary")),
        name="w_in_layout",
    )(jnp.swapaxes(w, 1, 2))


def _lane_row(pairs):
    tile = jnp.zeros((8, 128), F32)
    for row, lane, vals in pairs:
        vals = vals.reshape(-1).astype(F32)
        tile = tile.at[row, lane:lane + vals.shape[0]].set(vals)
    return tile


def _block_diag(w):
    nb, bw = w.shape[1], w.shape[2]
    eye = jnp.eye(nb, dtype=w.dtype)
    full = jnp.einsum("dnjk,nm->dnjmk", w, eye)
    return full.reshape(2, nb * bw, nb * bw).astype(BF16)


def _ctx_to_cols(a, wcols):
    t, ch = a.shape
    return a.reshape(wcols, t // wcols, ch).transpose(1, 0, 2).reshape(t, ch)


def _ctx_from_cols(a, wcols):
    t, ch = a.shape
    return a.reshape(t // wcols, wcols, ch).transpose(1, 0, 2).reshape(t, ch)


CTX_COLS = 8


def kernel(x, c, ctx, c_ctx, ada_w, ada_b, norm1_g, norm2_g, w_in, dn_conv_w, dn_a_log, dn_dt_bias, dn_norm_g, ssd_conv_w, ssd_conv_b, ssd_a_log, ssd_dt_bias, ssd_d, ssd_norm_g, lru_conv_w, lru_conv_b, lru_w_a, lru_b_a, lru_w_i, lru_b_i, lru_lambda, ml_ig_b, ml_fg_b, ml_norm_g, w_out, ffn_w_up, ffn_conv_w, ffn_w_down, final_norm_g):
    assert x.shape[0] == 1 and c.shape[0] == 1
    lat, hctx = x[0], ctx[0]
    cc = jnp.zeros((8, D_MODEL), F32).at[0].set(c[0]).at[1].set(c_ctx)
    mods = _ada(cc, ada_w, ada_b)
    row = lambda v: v.reshape(1, -1).astype(F32)
    w_in_p = _perm_w_in(w_in)
    w_out_b = w_out.astype(BF16)
    w_up_b = ffn_w_up.astype(BF16)
    w_down_b = ffn_w_down.astype(BF16)

    for l in range(DEPTH):
        mod_l = mods[l, 0].reshape(N_MOD, 1, D_MODEL)
        mod_c = mods[l, 1].reshape(N_MOD, 1, D_MODEL)
        ng1, ng2 = row(norm1_g[l]), row(norm2_g[l])

        dn_par = _lane_row([(0, LANE_DN_A, dn_a_log[l]), (1, LANE_DN_A, dn_dt_bias[l])])
        ssd_par = _lane_row([(0, LANE_SSD_DT, ssd_dt_bias[l]), (1, LANE_SSD_DT, ssd_a_log[l])])
        ml_par = _lane_row([(0, LANE_ML_I, ml_ig_b[l]), (1, LANE_ML_F, ml_fg_b[l])])
        lru_args = (lru_conv_w[l], row(lru_conv_b[l]), _block_diag(lru_w_a[l]), lru_b_a[l],
                    _block_diag(lru_w_i[l]), lru_b_i[l], lru_lambda[l])
        dn_g = row(jnp.tile(dn_norm_g[l], DN_HEADS))
        sd_d = row(jnp.repeat(ssd_d[l], SSD_HEAD_DIM))
        sd_g, ml_g = row(ssd_norm_g[l]), row(ml_norm_g[l])

        conv_w = jnp.concatenate([ssd_conv_w[l], dn_conv_w[l]], axis=1)
        conv_b = jnp.concatenate([row(ssd_conv_b[l]), jnp.zeros((1, 3 * GROUP_W), F32)], axis=1)

        pc = _inproj(hctx, ng1, mod_c[1], mod_c[0], w_in_p, l, conv_w, conv_b)
        dn_cf, dn_cb, dn_s = _dn_scan(pc, dn_par, jnp.zeros((2 * DN_HEADS, DN_DK, DN_DV), F32))
        sd_cf, sd_cb, sd_s = _ssd_scan(
            pc, ssd_par, jnp.zeros((2 * SSD_GROUPS, SSD_STATE, GROUP_W // SSD_GROUPS), F32))
        ml_cf, ml_cb, ml_c, ml_m = _ml_scan(
            pc, ml_par, jnp.zeros((2 * ML_HEADS, ML_DK, 2 * ML_DV), F32), jnp.zeros((1, 128), F32))
        xc_cols = _ctx_to_cols(pc[:, COL_LRU_X:COL_LRU_X + LRU_W], CTX_COLS)
        lr_c, lr_s = _lru_scan(xc_cols, 0, CTX_COLS, *lru_args, jnp.zeros((2, LRU_W), F32))

        pl_ = _inproj(lat, ng1, mod_l[1], mod_l[0], w_in_p, l, conv_w, conv_b)
        dn_lf, dn_lb, _ = _dn_scan(pl_, dn_par, dn_s)
        sd_lf, sd_lb, _ = _ssd_scan(pl_, ssd_par, sd_s)
        ml_lf, ml_lb, _, _ = _ml_scan(pl_, ml_par, ml_c, ml_m)
        lr_l, _ = _lru_scan(pl_, COL_LRU_X // 128, GRID_W, *lru_args, lr_s)

        lat = _outproj(lat, mod_l[2], pl_, dn_lf, dn_lb, dn_g, sd_lf, sd_lb, sd_d, sd_g,
                       lr_l, ml_lf, ml_lb, ml_g, w_out_b, l)
        last = l == DEPTH - 1
        lat = _ffn(lat, ng2, mod_l[4], mod_l[3], mod_l[5], row(final_norm_g), w_up_b,
                   ffn_conv_w[l], w_down_b, l, last)

        if not last:
            lr_cn = _ctx_from_cols(lr_c, CTX_COLS)
            hctx = _outproj(hctx, mod_c[2], pc, dn_cf, dn_cb, dn_g, sd_cf, sd_cb, sd_d, sd_g,
                            lr_cn, ml_cf, ml_cb, ml_g, w_out_b, l)
            hctx = _ffn(hctx, ng2, mod_c[4], mod_c[3], mod_c[5], row(final_norm_g), w_up_b,
                        ffn_conv_w[l], w_down_b, l, False)

    return lat[None]
```

```python
import functools

import jax
import jax.numpy as jnp
from jax import lax
from jax.experimental import pallas as pl
from jax.experimental.pallas import tpu as pltpu

F32 = jnp.float32
BF16 = jnp.bfloat16

D_MODEL = 2048
DEPTH = 2
GRID_W = 64
GROUP_W = D_MODEL // 4
CHUNK = 64
SHORT_CONV = 5
FFN_CONV = 3
D_FF = ((8 * D_MODEL // 3 + 255) // 256) * 256
N_MOD = 6
EPS = 1e-6

DN_HEADS = 4
DN_DK = GROUP_W // DN_HEADS
DN_DV = GROUP_W // DN_HEADS
SSD_HEAD_DIM = 64
SSD_HEADS = GROUP_W // SSD_HEAD_DIM
SSD_GROUPS = 2
SSD_STATE = 128
LRU_W = GROUP_W
LRU_BLOCKS = 8
LRU_BW = LRU_W // LRU_BLOCKS
LRU_C = 8.0
ML_HEADS = 4
ML_DV = GROUP_W // ML_HEADS
ML_DK = ML_DV // 2

_SRC = {}
_off = 0
for _name, _w in (
    ("dn_q", 512), ("dn_k", 512), ("dn_v", 512), ("dn_z", 512), ("dn_a", 8), ("dn_b", 8),
    ("ssd_x", 512), ("ssd_z", 512), ("ssd_B", 256), ("ssd_C", 256), ("ssd_dt", 16),
    ("lru_x", 512), ("lru_g", 512),
    ("ml_q", 256), ("ml_k", 256), ("ml_v", 512), ("ml_o", 512), ("ml_i", 8), ("ml_f", 8),
):
    _SRC[_name] = (_off, _w)
    _off += _w
D_IN = _off

_DST_ORDER = ("ssd_x", "ssd_B", "ssd_C", "dn_q", "dn_k", "dn_v", "ml_q", "ml_k", "ml_v",
              "lru_x", "dn_z", "ssd_z", "lru_g", "ml_o", "dn_a", "dn_b", "ssd_dt", "ml_i", "ml_f")
N_PROJ = 6400
COL_SSD_XBC = 0
COL_DN_Q = 1024
COL_DN_K = 1536
COL_DN_V = 2048
N_CONV = 2560
COL_ML_QK = 2560
COL_ML_V = 3072
COL_LRU_X = 3584
COL_DN_Z = 4096
COL_SSD_Z = 4608
COL_LRU_G = 5120
COL_ML_O = 5632
COL_GATES = 6144
LANE_DN_A, LANE_DN_B, LANE_SSD_DT, LANE_ML_I, LANE_ML_F = 0, 8, 16, 32, 40

VMEM_LIMIT = 56 * 1024 * 1024
MXU_K = 256


def _cparams(sem):
    return pltpu.CompilerParams(dimension_semantics=sem, vmem_limit_bytes=VMEM_LIMIT)


_NN = (((1,), (0,)), ((), ()))
_NT = (((1,), (1,)), ((), ()))
_TN = (((0,), (0,)), ((), ()))


def _dg(a, b, dims):
    return lax.dot_general(a, b, dims, preferred_element_type=F32)


def _mm(a, b, dims=_NN):
    return _dg(a.astype(BF16), b.astype(BF16), dims)


def _split3(a):
    hi = a.astype(BF16)
    r = a - hi.astype(F32)
    mid = r.astype(BF16)
    return hi, mid, (r - mid.astype(F32)).astype(BF16)


def _mm_sel(sel, x):
    sb = sel.astype(BF16)
    x0, x1, x2 = _split3(x)
    return _dg(sb, x0, _NN) + (_dg(sb, x1, _NN) + _dg(sb, x2, _NN))


def _mm_spread(x, sel):
    sb = sel.astype(BF16)
    x0, x1, x2 = _split3(x)
    return _dg(x0, sb, _NN) + (_dg(x1, sb, _NN) + _dg(x2, sb, _NN))


def _silu(x):
    return x * jax.nn.sigmoid(x)


def _masks(d):
    ri = lax.broadcasted_iota(jnp.int32, (CHUNK, CHUNK), 0)
    ci = lax.broadcasted_iota(jnp.int32, (CHUNK, CHUNK), 1)
    if d == 0:
        return ri >= ci, ri > ci
    return ri <= ci, ri < ci


def _seg_decay(col, row, incl):
    return jnp.where(incl, jnp.exp(jnp.where(incl, col - row, 0.0)), 0.0)


ADA_TK = 256


def _ada_kernel(c_ref, w_ref, b_ref, o_ref):
    k = pl.program_id(1)

    @pl.when(k == 0)
    def _():
        o_ref[...] = jnp.broadcast_to(b_ref[...], o_ref.shape)

    cc = c_ref[...]
    act = (cc * jax.nn.sigmoid(cc)).astype(BF16)
    o_ref[...] += jnp.dot(act, w_ref[...].astype(BF16), preferred_element_type=F32)


def _ada(cc, ada_w, ada_b):
    n = N_MOD * D_MODEL
    return pl.pallas_call(
        _ada_kernel,
        grid=(DEPTH, D_MODEL // ADA_TK),
        in_specs=[
            pl.BlockSpec((8, ADA_TK), lambda l, k: (0, k)),
            pl.BlockSpec((None, ADA_TK, n), lambda l, k: (l, k, 0)),
            pl.BlockSpec((None, 1, n), lambda l, k: (l, 0, 0)),
        ],
        out_specs=pl.BlockSpec((None, 8, n), lambda l, k: (l, 0, 0)),
        out_shape=jax.ShapeDtypeStruct((DEPTH, 8, n), F32),
        compiler_params=_cparams(("arbitrary", "arbitrary")),
        name="ada",
    )(cc, ada_w, ada_b.reshape(DEPTH, 1, n))


def _norm_mod(xf, ng, sc, sh):
    y = xf * lax.rsqrt(jnp.mean(xf * xf, axis=-1, keepdims=True) + EPS) * ng
    return y * (1.0 + sc) + sh


HALO = 16
PROJ_TN = 1280
PROJ_SUB = 256


def _fill_halo_tile(h_scr, i, nt, x_ref, xp_ref, xn_ref, ng, sc, sh):
    tm = x_ref.shape[0]
    hp = jnp.where(i > 0, _norm_mod(xp_ref[...], ng, sc, sh), 0.0)
    hn = jnp.where(i < nt - 1, _norm_mod(xn_ref[...], ng, sc, sh), 0.0)
    h_scr[0:HALO, :] = hp.astype(BF16)
    h_scr[HALO:HALO + tm, :] = _norm_mod(x_ref[...], ng, sc, sh).astype(BF16)
    h_scr[HALO + tm:2 * HALO + tm, :] = hn.astype(BF16)


def _inproj_kernel(nt, x_ref, xp_ref, xn_ref, ng_ref, sc_ref, sh_ref, w_ref, cw_ref, cb_ref,
                   o_ref, h_scr):
    i = pl.program_id(0)
    j = pl.program_id(1)
    tm = x_ref.shape[0]

    @pl.when(j == 0)
    def _():
        _fill_halo_tile(h_scr, i, nt, x_ref, xp_ref, xn_ref, ng_ref[...], sc_ref[...], sh_ref[...])

    for jj in range(N_CONV // PROJ_TN):
        @pl.when(j == jj)
        def _():
            rows = tm + 2 * HALO
            for c in range(PROJ_TN // PROJ_SUB):
                cs = slice(c * PROJ_SUB, (c + 1) * PROJ_SUB)
                y = jnp.dot(h_scr[...], w_ref[:, cs], preferred_element_type=F32)
                acc = cb_ref[:, cs] + cw_ref[2:3, cs] * y[HALO:HALO + tm]
                for tap in (0, 1, 3, 4):
                    shifted = pltpu.roll(y, (2 - tap) % rows, axis=0)
                    acc = acc + cw_ref[tap:tap + 1, cs] * shifted[HALO:HALO + tm]
                act = _silu(acc)
                for half in range(PROJ_SUB // 128):
                    col = jj * PROJ_TN + c * PROJ_SUB + half * 128
                    a = act[:, half * 128:(half + 1) * 128]
                    if COL_DN_Q <= col < COL_DN_V:
                        a = a * lax.rsqrt(jnp.sum(a * a, axis=-1, keepdims=True) + EPS)
                        if col < COL_DN_K:
                            a = a * DN_DK ** -0.5
                    o_ref[:, c * PROJ_SUB + half * 128:c * PROJ_SUB + (half + 1) * 128] = a

    @pl.when(j >= N_CONV // PROJ_TN)
    def _():
        o_ref[...] = jnp.dot(h_scr[HALO:HALO + tm, :], w_ref[...], preferred_element_type=F32)


def _halo_specs(t, tm):
    per, nhb = tm // HALO, t // HALO
    return [pl.BlockSpec((tm, D_MODEL), lambda i, k: (i, 0)),
            pl.BlockSpec((HALO, D_MODEL), lambda i, k: (jnp.maximum(i * per - 1, 0), 0)),
            pl.BlockSpec((HALO, D_MODEL), lambda i, k: (jnp.minimum((i + 1) * per, nhb - 1), 0))]


def _inproj(x, ng, sc, sh, w, l, cw, cb):
    t = x.shape[0]
    tm = min(t, 1024)
    tn = PROJ_TN
    nconv = N_CONV // tn
    row = lambda i, j: (0, 0)
    return pl.pallas_call(
        functools.partial(_inproj_kernel, t // tm),
        grid=(t // tm, N_PROJ // tn),
        in_specs=_halo_specs(t, tm) + [
            pl.BlockSpec((1, D_MODEL), row),
            pl.BlockSpec((1, D_MODEL), row),
            pl.BlockSpec((1, D_MODEL), row),
            pl.BlockSpec((None, D_MODEL, tn), lambda i, j: (l, 0, j)),
            pl.BlockSpec((SHORT_CONV, tn), lambda i, j: (0, jnp.minimum(j, nconv - 1))),
            pl.BlockSpec((1, tn), lambda i, j: (0, jnp.minimum(j, nconv - 1))),
        ],
        out_specs=pl.BlockSpec((tm, tn), lambda i, j: (i, j)),
        out_shape=jax.ShapeDtypeStruct((t, N_PROJ), F32),
        scratch_shapes=[pltpu.VMEM((tm + 2 * HALO, D_MODEL), BF16)],
        compiler_params=_cparams(("arbitrary", "arbitrary")),
        name="inproj",
    )(x, x, x, ng, sc, sh, w, cw, cb)


SUBS = 4
ROWS = SUBS * CHUNK


def _sub_order(d):
    return list(range(SUBS)) if d == 0 else list(range(SUBS - 1, -1, -1))


def _col_spec(nb, width, col, rev):
    blk = col // width
    assert blk * width == col
    if rev:
        return pl.BlockSpec((ROWS, width), lambda i: (nb - 1 - i, blk))
    return pl.BlockSpec((ROWS, width), lambda i: (i, blk))


def _gate_spec(nb, rev):
    return _col_spec(nb, 128, COL_GATES, rev)


def _out_spec(nb, width, rev):
    if rev:
        return pl.BlockSpec((ROWS, width), lambda i: (nb - 1 - i, 0))
    return pl.BlockSpec((ROWS, width), lambda i: (i, 0))


def _const_spec(shape):
    nd = len(shape)
    return pl.BlockSpec(shape, lambda i: (0,) * nd)


def _cumsum_dir(x, d):
    ri = lax.broadcasted_iota(jnp.int32, (ROWS, ROWS), 0)
    ci = lax.broadcasted_iota(jnp.int32, (ROWS, ROWS), 1)
    incl = (ri >= ci) if d == 0 else (ri <= ci)
    return _mm_sel(jnp.where((ri // CHUNK == ci // CHUNK) & incl, 1.0, 0.0), x)


def _dn_kernel(qf, kf, vf, qb, kb_, vb_, gf, gb, par_ref, s0_ref, of_ref, ob_ref, s_ref):
    i = pl.program_id(0)

    @pl.when(i == 0)
    def _():
        s_ref[...] = s0_ref[...]

    eye = jnp.where(_masks(0)[0] & _masks(1)[0], 1.0, 0.0).astype(F32)
    hd = []
    for d, (q_ref, k_ref, v_ref, g_ref, o_ref) in enumerate(((qf, kf, vf, gf, of_ref),
                                                             (qb, kb_, vb_, gb, ob_ref))):
        incl, strict = _masks(d)
        last = CHUNK - 1 if d == 0 else 0
        gates = g_ref[...]
        g_all = -jnp.exp(par_ref[0:1, :]) * jax.nn.softplus(gates + par_ref[1:2, :])
        beta_all = jax.nn.sigmoid(gates)
        gc = _cumsum_dir(g_all, d)
        gct = gc.T
        for pos, sub in enumerate(_sub_order(d)):
            rs = slice(sub * CHUNK, (sub + 1) * CHUNK)
            for h in range(DN_HEADS):
                lane = LANE_DN_A + d * DN_HEADS + h
                blane = LANE_DN_B + d * DN_HEADS + h
                k = k_ref[rs, h * DN_DK:(h + 1) * DN_DK]
                gcc = gc[rs, lane:lane + 1]
                beta = beta_all[rs, blane:blane + 1]
                hd.append(dict(
                    q=q_ref[rs, h * DN_DK:(h + 1) * DN_DK], k=k, kb=k * beta,
                    vb=v_ref[rs, h * DN_DV:(h + 1) * DN_DV] * beta, gcc=gcc,
                    tot=gc[sub * CHUNK + last:sub * CHUNK + last + 1, lane:lane + 1],
                    strict=strict, decay=_seg_decay(gcc, gct[lane:lane + 1, rs], incl),
                    pos=pos, rs=rs, o_ref=o_ref, osl=slice(h * DN_DV, (h + 1) * DN_DV),
                    sidx=d * DN_HEADS + h))

    nmat = [jnp.where(x["strict"], _mm(x["kb"], x["k"], _NT) * x["decay"], 0.0) for x in hd]
    attn = [_mm(x["q"], x["k"], _NT) * x["decay"] for x in hd]
    ri = lax.broadcasted_iota(jnp.int32, (CHUNK, CHUNK), 0)
    ci = lax.broadcasted_iota(jnp.int32, (CHUNK, CHUNK), 1)
    base = 8
    pw = [jnp.where(ri // base == ci // base, n, 0.0) for n in nmat]
    tinv = [eye - p for p in pw]
    for _ in range(2):
        pw = [_mm(p, p) for p in pw]
        tinv = [t + _mm(t, p) for t, p in zip(tinv, pw)]
    size = base
    while size < CHUNK:
        sib = (ri // (2 * size) == ci // (2 * size)) & (ri // size != ci // size)
        tc = [_mm(t, jnp.where(sib, n, 0.0)) for t, n in zip(tinv, nmat)]
        tinv = [t - _mm(c, t) for t, c in zip(tinv, tc)]
        size *= 2
    sol = [_mm(t, jnp.concatenate([x["vb"], x["kb"] * jnp.exp(x["gcc"])], axis=-1))
           for t, x in zip(tinv, hd)]
    state = {j: s_ref[j] for j in range(2 * DN_HEADS)}
    for pos in range(SUBS):
        cur = [(x, so, a) for x, so, a in zip(hd, sol, attn) if x["pos"] == pos]
        v_new = [so[:, :DN_DV] - _mm(so[:, DN_DV:], state[x["sidx"]]) for x, so, _ in cur]
        for (x, _, a), vn in zip(cur, v_new):
            x["o_ref"][x["rs"], x["osl"]] = (_mm(x["q"] * jnp.exp(x["gcc"]), state[x["sidx"]])
                                             + _mm(a, vn))
        for (x, _, _), vn in zip(cur, v_new):
            state[x["sidx"]] = (state[x["sidx"]] * jnp.exp(x["tot"])
                                + _mm(x["k"] * jnp.exp(x["tot"] - x["gcc"]), vn, _TN))
    for j in range(2 * DN_HEADS):
        s_ref[j] = state[j]


def _dn_scan(p, par, s0):
    t = p.shape[0]
    nb = t // ROWS
    out = jax.ShapeDtypeStruct((t, GROUP_W), F32)
    qkv = lambda rev: [_col_spec(nb, GROUP_W, col, rev) for col in (COL_DN_Q, COL_DN_K, COL_DN_V)]
    return pl.pallas_call(
        _dn_kernel,
        grid=(nb,),
        in_specs=(qkv(False) + qkv(True)
                  + [_gate_spec(nb, False), _gate_spec(nb, True), _const_spec((8, 128)),
                     _const_spec((2 * DN_HEADS, DN_DK, DN_DV))]),
        out_specs=[_out_spec(nb, GROUP_W, False), _out_spec(nb, GROUP_W, True),
                   _const_spec((2 * DN_HEADS, DN_DK, DN_DV))],
        out_shape=[out, out, jax.ShapeDtypeStruct((2 * DN_HEADS, DN_DK, DN_DV), F32)],
        compiler_params=_cparams(("arbitrary",)),
        name="dn_scan",
    )(p, p, p, p, p, p, p, p, par, s0)


def _ssd_kernel(xf, xb, gf, gb, par_ref, s0_ref, of_ref, ob_ref, s_ref):
    i = pl.program_id(0)

    @pl.when(i == 0)
    def _():
        s_ref[...] = s0_ref[...]

    hp = SSD_HEADS * SSD_HEAD_DIM
    gw = hp // SSD_GROUPS
    rep = SSD_HEADS // SSD_GROUPS
    items = []
    for d, (x_ref, g_ref, o_ref) in enumerate(((xf, gf, of_ref), (xb, gb, ob_ref))):
        incl, _ = _masks(d)
        last = CHUNK - 1 if d == 0 else 0
        gates = g_ref[...]
        dt_all = jax.nn.softplus(gates + par_ref[0:1, :])
        a_all = -jnp.exp(par_ref[1:2, :]) * dt_all
        acs = _cumsum_dir(a_all, d)
        acst = acs.T
        er = lax.broadcasted_iota(jnp.int32, (128, hp), 0)
        ec = lax.broadcasted_iota(jnp.int32, (128, hp), 1)
        expand = jnp.where(er == LANE_SSD_DT + d * SSD_HEADS + ec // SSD_HEAD_DIM, 1.0, 0.0)
        both_x = _mm_spread(jnp.concatenate([dt_all, acs], axis=0), expand)
        dt_x, acs_x = both_x[:ROWS], both_x[ROWS:]
        xd_all = x_ref[:, :hp] * dt_x
        for pos, sub in enumerate(_sub_order(d)):
            rs = slice(sub * CHUNK, (sub + 1) * CHUNK)
            tot_x = acs_x[sub * CHUNK + last:sub * CHUNK + last + 1, :]
            xd = xd_all[rs]
            xdw = xd * jnp.exp(tot_x - acs_x[rs])
            for g in range(SSD_GROUPS):
                bm = x_ref[rs, hp + g * SSD_STATE:hp + (g + 1) * SSD_STATE]
                cm = x_ref[rs, hp + SSD_GROUPS * SSD_STATE + g * SSD_STATE:
                           hp + SSD_GROUPS * SSD_STATE + (g + 1) * SSD_STATE]
                cb = _mm(cm, bm, _NT)
                gs = slice(g * gw, (g + 1) * gw)
                y_diag = []
                for hh in range(rep):
                    h = g * rep + hh
                    lane = LANE_SSD_DT + d * SSD_HEADS + h
                    hs = slice(h * SSD_HEAD_DIM, (h + 1) * SSD_HEAD_DIM)
                    lmat = _seg_decay(acs[rs, lane:lane + 1], acst[lane:lane + 1, rs], incl)
                    y_diag.append(_mm(cb * lmat, xd[:, hs]))
                items.append(dict(
                    pos=pos, rs=rs, gs=gs, o_ref=o_ref, sidx=d * SSD_GROUPS + g, cm=cm,
                    y_diag=jnp.concatenate(y_diag, axis=-1), off_scale=jnp.exp(acs_x[rs, gs]),
                    dec=jnp.exp(tot_x[:, gs]), local=_mm(bm, xdw[:, gs], _TN)))
    state = {j: s_ref[j] for j in range(2 * SSD_GROUPS)}
    for pos in range(SUBS):
        for x in [x for x in items if x["pos"] == pos]:
            s = state[x["sidx"]]
            x["o_ref"][x["rs"], x["gs"]] = _mm(x["cm"], s) * x["off_scale"] + x["y_diag"]
            state[x["sidx"]] = s * x["dec"] + x["local"]
    for j in range(2 * SSD_GROUPS):
        s_ref[j] = state[j]


def _ssd_scan(p, par, s0):
    t = p.shape[0]
    nb = t // ROWS
    wx = 2 * GROUP_W
    hp = SSD_HEADS * SSD_HEAD_DIM
    st = (2 * SSD_GROUPS, SSD_STATE, hp // SSD_GROUPS)
    out = jax.ShapeDtypeStruct((t, hp), F32)
    return pl.pallas_call(
        _ssd_kernel,
        grid=(nb,),
        in_specs=[_col_spec(nb, wx, COL_SSD_XBC, False), _col_spec(nb, wx, COL_SSD_XBC, True),
                  _gate_spec(nb, False), _gate_spec(nb, True), _const_spec((8, 128)),
                  _const_spec(st)],
        out_specs=[_out_spec(nb, hp, False), _out_spec(nb, hp, True), _const_spec(st)],
        out_shape=[out, out, jax.ShapeDtypeStruct(st, F32)],
        compiler_params=_cparams(("arbitrary",)),
        name="ssd_scan",
    )(p, p, p, p, par, s0)


def _cummax_dir(x, d):
    rows = x.shape[0]
    sub = lax.broadcasted_iota(jnp.int32, x.shape, 0) % CHUNK
    sh = 1
    while sh < CHUNK:
        if d == 0:
            x = jnp.where(sub >= sh, jnp.maximum(x, pltpu.roll(x, sh, axis=0)), x)
        else:
            x = jnp.where(sub < CHUNK - sh, jnp.maximum(x, pltpu.roll(x, rows - sh, axis=0)), x)
        sh *= 2
    return x


ML_PACK = 3 * CHUNK + 8


def _ml_kernel(qkf, vf, qkb, vb, gf, gb, par_ref, c0_ref, m0_ref, of_ref, ob_ref, c_ref, m_ref):
    i = pl.program_id(0)

    @pl.when(i == 0)
    def _():
        c_ref[...] = c0_ref[...]
        m_ref[...] = m0_ref[...]

    lane = lax.broadcasted_iota(jnp.int32, (1, 128), 1)
    ones = jnp.ones((CHUNK, ML_DV), F32)
    hd, m_rows = [], []
    for d, (qk_ref, v_ref, g_ref, o_ref) in enumerate(((qkf, vf, gf, of_ref), (qkb, vb, gb, ob_ref))):
        incl, _ = _masks(d)
        last = CHUNK - 1 if d == 0 else 0
        lane0 = LANE_ML_F + d * ML_HEADS
        valid = (lane >= lane0) & (lane < lane0 + ML_HEADS)
        gates = g_ref[...]
        ig_all = pltpu.roll(gates + par_ref[0:1, :], LANE_ML_F - LANE_ML_I, axis=1)
        b_all = _cumsum_dir(jax.nn.log_sigmoid(gates + par_ref[1:2, :]), d)
        r_all = ig_all - b_all
        cmax = _cummax_dir(r_all, d)
        rt = r_all.T
        m_cur = m_ref[...]
        packed, wk_src = [], {}
        for sub in _sub_order(d):
            rs = slice(sub * CHUNK, (sub + 1) * CHUNK)
            b = b_all[rs]
            m_all = b + jnp.maximum(m_cur, cmax[rs])
            b_last = b_all[sub * CHUNK + last:sub * CHUNK + last + 1, :]
            log_g = b_last - b + ig_all[rs]
            m_new = jnp.maximum(b_last + m_cur, jnp.max(log_g, axis=0, keepdims=True))
            packed += [b - m_all, jnp.exp(b + m_cur - m_all), jnp.exp(-m_all),
                       jnp.broadcast_to(jnp.exp(b_last + m_cur - m_new), (8, 128))]
            wk_src[sub] = jnp.exp(log_g - m_new)
            m_cur = m_new
        m_rows.append(jnp.where(valid, m_cur, 0.0))
        sr = lax.broadcasted_iota(jnp.int32, (128, ML_HEADS * ML_DV), 0)
        sc = lax.broadcasted_iota(jnp.int32, (128, ML_HEADS * ML_DV), 1)
        wide = _mm_spread(jnp.where(valid, jnp.concatenate(packed, axis=0), 0.0),
                          jnp.where(sr == lane0 + sc // ML_DV, 1.0, 0.0))
        sr = lax.broadcasted_iota(jnp.int32, (128, ML_HEADS * ML_DK), 0)
        sc = lax.broadcasted_iota(jnp.int32, (128, ML_HEADS * ML_DK), 1)
        wkc = _mm_spread(
            jnp.where(valid, jnp.concatenate([wk_src[s] for s in range(SUBS)], axis=0), 0.0),
            jnp.where(sr == lane0 + sc // ML_DK, 1.0, 0.0))
        for pos, sub in enumerate(_sub_order(d)):
            rs = slice(sub * CHUNK, (sub + 1) * CHUNK)
            r0 = pos * ML_PACK
            for h in range(ML_HEADS):
                ws = slice(h * ML_DV, (h + 1) * ML_DV)
                k = qk_ref[rs, ML_HEADS * ML_DK + h * ML_DK:ML_HEADS * ML_DK + (h + 1) * ML_DK]
                hd.append(dict(
                    q=qk_ref[rs, h * ML_DK:(h + 1) * ML_DK] * ML_DK ** -0.5, k=k,
                    wk=wkc[rs, h * ML_DK:(h + 1) * ML_DK] * k,
                    v_aug=jnp.concatenate([v_ref[rs, h * ML_DV:(h + 1) * ML_DV], ones], axis=-1),
                    dmat=jnp.where(incl, jnp.exp(wide[r0:r0 + CHUNK, h * ML_DV:h * ML_DV + CHUNK]
                                                 + rt[lane0 + h:lane0 + h + 1, rs]), 0.0),
                    w_inter=wide[r0 + CHUNK:r0 + 2 * CHUNK, ws],
                    enm=wide[r0 + 2 * CHUNK:r0 + 3 * CHUNK, ws],
                    dec=wide[r0 + 3 * CHUNK:r0 + 3 * CHUNK + 1, ws],
                    pos=pos, rs=rs, sd=d * ML_HEADS + h, o_ref=o_ref, osl=ws))

    s = [_mm(x["q"], x["k"], _NT) * x["dmat"] for x in hd]
    intra = [_mm(s_h, x["v_aug"]) for s_h, x in zip(s, hd)]
    upd = [_mm(x["wk"], x["v_aug"], _TN) for x in hd]
    state = {j: c_ref[j] for j in range(2 * ML_HEADS)}
    for pos in range(SUBS):
        cur = [(x, ia, up) for x, ia, up in zip(hd, intra, upd) if x["pos"] == pos]
        inter = [_mm(x["q"], state[x["sd"]]) for x, _, _ in cur]
        for (x, ia, up), ie in zip(cur, inter):
            num = x["w_inter"] * ie[:, :ML_DV] + ia[:, :ML_DV]
            den = x["w_inter"] * ie[:, ML_DV:] + ia[:, ML_DV:]
            x["o_ref"][x["rs"], x["osl"]] = num / jnp.maximum(jnp.abs(den), x["enm"])
            state[x["sd"]] = jnp.concatenate([x["dec"], x["dec"]], axis=-1) * state[x["sd"]] + up
    for j in range(2 * ML_HEADS):
        c_ref[j] = state[j]
    m_ref[...] = m_rows[0] + m_rows[1]


def _ml_scan(p, par, c0, m0):
    t = p.shape[0]
    nb = t // ROWS
    cs, ms = (2 * ML_HEADS, ML_DK, 2 * ML_DV), (1, 128)
    out = jax.ShapeDtypeStruct((t, GROUP_W), F32)
    qkv = lambda rev: [_col_spec(nb, GROUP_W, COL_ML_QK, rev), _col_spec(nb, GROUP_W, COL_ML_V, rev)]
    return pl.pallas_call(
        _ml_kernel,
        grid=(nb,),
        in_specs=(qkv(False) + qkv(True)
                  + [_gate_spec(nb, False), _gate_spec(nb, True), _const_spec((8, 128)),
                     _const_spec(cs), _const_spec(ms)]),
        out_specs=[_out_spec(nb, GROUP_W, False), _out_spec(nb, GROUP_W, True),
                   _const_spec(cs), _const_spec(ms)],
        out_shape=[out, out, jax.ShapeDtypeStruct(cs, F32), jax.ShapeDtypeStruct(ms, F32)],
        compiler_params=_cparams(("arbitrary",)),
        name="ml_scan",
    )(p, p, p, p, p, p, par, c0, m0)


def _lru_kernel(rows, wcols, x_ref, cw_ref, cb_ref, wa_ref, ba_ref, wi_ref, bi_ref, lam_ref,
                h0_ref, o_ref, hfin_ref, xp_scr, a_all, b_all):
    t = rows * wcols
    sub = lax.broadcasted_iota(jnp.int32, (wcols, 128), 0)

    def shift_down(a):
        return jnp.where(sub >= 1, pltpu.roll(a, 1, axis=0), 0.0)

    def shift_up(a):
        return jnp.where(sub < wcols - 1, pltpu.roll(a, wcols - 1, axis=0), 0.0)

    def slab(r):
        return pl.ds(pl.multiple_of(r * wcols, wcols), wcols)

    xp_scr[pl.ds(2 * wcols, t), :] = x_ref[...]
    xp_scr[pl.ds(0, wcols), :] = shift_down(x_ref[pl.ds((rows - 2) * wcols, wcols), :])
    xp_scr[pl.ds(wcols, wcols), :] = shift_down(x_ref[pl.ds((rows - 1) * wcols, wcols), :])
    xp_scr[pl.ds((rows + 2) * wcols, wcols), :] = shift_up(x_ref[pl.ds(0, wcols), :])
    xp_scr[pl.ds((rows + 3) * wcols, wcols), :] = shift_up(x_ref[pl.ds(wcols, wcols), :])

    rb = 256 if t % 256 == 0 else t
    sp_lam = jax.nn.softplus(-lam_ref[...])

    def sigmoid(v):
        return 0.5 * jnp.tanh(0.5 * v) + 0.5

    def gate_body(blk, carry):
        base = pl.multiple_of(blk * rb, rb)
        xr = cb_ref[...] + cw_ref[0:1, :] * xp_scr[pl.ds(base, rb), :]
        for j in range(1, SHORT_CONV):
            xr = xr + cw_ref[j:j + 1, :] * xp_scr[pl.ds(pl.multiple_of(base + j * wcols, 8), rb), :]
        xb = xr.astype(BF16)
        for d in range(2):
            r = sigmoid(jnp.dot(xb, wa_ref[d], preferred_element_type=F32) + ba_ref[d:d + 1, :])
            ii = sigmoid(jnp.dot(xb, wi_ref[d], preferred_element_type=F32) + bi_ref[d:d + 1, :])
            log_a = -LRU_C * r * sp_lam[d:d + 1, :]
            a = jnp.exp(log_a)
            a_all[d, pl.ds(base, rb), :] = a
            b_all[d, pl.ds(base, rb), :] = jnp.sqrt(-jnp.tanh(log_a) * (a * a + 1.0)) * ii * xr
        return carry

    lax.fori_loop(0, t // rb, gate_body, 0)

    for d in range(2):
        a_scr, b_scr = a_all.at[d], b_all.at[d]

        def scan_body(step, carry):
            h, acc = carry
            r = step if d == 0 else rows - 1 - step
            a = a_scr[slab(r), :]
            h = a * h + b_scr[slab(r), :]
            acc = a * acc
            b_scr[slab(r), :] = h
            a_scr[slab(r), :] = acc
            return h, acc

        h_end, a_end = lax.fori_loop(
            0, rows, scan_body, (jnp.zeros((wcols, 128), F32), jnp.ones((wcols, 128), F32)))

        sh = 1
        while sh < wcols:
            if d == 0:
                valid = sub >= sh
                a_sh, h_sh = pltpu.roll(a_end, sh, axis=0), pltpu.roll(h_end, sh, axis=0)
            else:
                valid = sub < wcols - sh
                a_sh, h_sh = pltpu.roll(a_end, wcols - sh, axis=0), pltpu.roll(h_end, wcols - sh, axis=0)
            h_end = jnp.where(valid, a_end * h_sh + h_end, h_end)
            a_end = jnp.where(valid, a_end * a_sh, a_end)
            sh *= 2
        h0 = h0_ref[d:d + 1, :]
        h_full = h_end + a_end * h0
        if d == 0:
            carry_in = jnp.where(sub >= 1, pltpu.roll(h_full, 1, axis=0), h0)
            hfin_ref[0:1, :] = h_full[wcols - 1:wcols, :]
        else:
            carry_in = jnp.where(sub < wcols - 1, pltpu.roll(h_full, wcols - 1, axis=0), h0)
            hfin_ref[1:2, :] = h_full[0:1, :]

        def fix_body(r, carry):
            hv = b_scr[slab(r), :] + a_scr[slab(r), :] * carry_in
            if d == 0:
                o_ref[slab(r), :] = hv
            else:
                o_ref[slab(r), :] += hv
            return carry

        lax.fori_loop(0, rows, fix_body, 0)


def _lru_scan(x, colblk, wcols, cw, cb, wa, ba, wi, bi, lam, h0):
    t = x.shape[0]
    rows = t // wcols
    nt = LRU_W // 128
    vec = lambda r: pl.BlockSpec((r, 128), lambda j: (0, j))
    wspec = pl.BlockSpec((2, 128, 128), lambda j: (0, j, j))
    return pl.pallas_call(
        functools.partial(_lru_kernel, rows, wcols),
        grid=(nt,),
        in_specs=[pl.BlockSpec((t, 128), lambda j: (0, colblk + j)),
                  vec(SHORT_CONV), vec(1), wspec, vec(2), wspec, vec(2), vec(2), vec(2)],
        out_specs=[pl.BlockSpec((t, 128), lambda j: (0, j)), vec(2)],
        out_shape=[jax.ShapeDtypeStruct((t, LRU_W), F32), jax.ShapeDtypeStruct((2, LRU_W), F32)],
        scratch_shapes=[pltpu.VMEM((t + 4 * wcols, 128), F32), pltpu.VMEM((2, t, 128), F32),
                        pltpu.VMEM((2, t, 128), F32)],
        compiler_params=_cparams(("arbitrary",)),
        name="lru_scan",
    )(x, cw, cb, wa, ba, wi, bi, lam, h0)


def _rms_groups(x, g, width):
    parts = []
    for s in range(0, x.shape[-1], width):
        xs = x[:, s:s + width]
        parts.append(xs * lax.rsqrt(jnp.mean(xs * xs, axis=-1, keepdims=True) + EPS) * g[:, s:s + width])
    return parts


def _outproj_kernel(x_ref, g1_ref, dnf, dnb, dnz, dng, sdf, sdb, sdx, sdz, sdd, sdg,
                    lrh, lrg, mlf, mlb, mlo, mlg, w_ref, o_ref):
    def slabs(parts, base):
        per = MXU_K // parts[0].shape[-1]
        for s in range(0, len(parts), per):
            yield jnp.concatenate(parts[s:s + per], axis=-1), base + (s // per) * MXU_K

    pieces = []
    z = dnz[...]
    dn = [part * _silu(z[:, h * DN_DV:(h + 1) * DN_DV])
          for h, part in enumerate(_rms_groups(dnf[...] + dnb[...], dng[...], DN_DV))]
    pieces += list(slabs(dn, 0))
    y = (sdf[...] + sdb[...] + sdd[...] * sdx[...]) * _silu(sdz[...])
    pieces += list(slabs(_rms_groups(y, sdg[...], GROUP_W // SSD_GROUPS), GROUP_W))
    lr = lrh[...] * jax.nn.gelu(lrg[...])
    pieces += list(slabs([lr[:, s:s + MXU_K] for s in range(0, GROUP_W, MXU_K)], 2 * GROUP_W))
    o = mlo[...]
    ml = [part * jax.nn.sigmoid(o[:, h * ML_DV:(h + 1) * ML_DV])
          for h, part in enumerate(_rms_groups(mlf[...] + mlb[...], mlg[...], ML_DV))]
    pieces += list(slabs(ml, 3 * GROUP_W))
    acc = jnp.zeros(o_ref.shape, F32)
    for slab, row0 in pieces:
        acc += _mm(slab, w_ref[row0:row0 + MXU_K, :])
    o_ref[...] = x_ref[...] + g1_ref[...] * acc


def _outproj(x, g1, p, dn_f, dn_b, dn_g, sd_f, sd_b, sd_d, sd_g, lr_h, ml_f, ml_b, ml_g, w, l):
    t = x.shape[0]
    tm = min(t, 256)
    tok = pl.BlockSpec((tm, GROUP_W), lambda i: (i, 0))
    pcol = lambda col: pl.BlockSpec((tm, GROUP_W), lambda i: (i, col // GROUP_W))
    vec = lambda n: pl.BlockSpec((1, n), lambda i: (0, 0))
    return pl.pallas_call(
        _outproj_kernel,
        grid=(t // tm,),
        in_specs=[pl.BlockSpec((tm, D_MODEL), lambda i: (i, 0)), vec(D_MODEL),
                  tok, tok, pcol(COL_DN_Z), vec(GROUP_W),
                  tok, tok, pcol(COL_SSD_XBC), pcol(COL_SSD_Z), vec(GROUP_W), vec(GROUP_W),
                  tok, pcol(COL_LRU_G),
                  tok, tok, pcol(COL_ML_O), vec(GROUP_W),
                  pl.BlockSpec((None, D_MODEL, D_MODEL), lambda i: (l, 0, 0))],
        out_specs=pl.BlockSpec((tm, D_MODEL), lambda i: (i, 0)),
        out_shape=jax.ShapeDtypeStruct((t, D_MODEL), F32),
        compiler_params=_cparams(("arbitrary",)),
        name="outproj",
    )(x, g1, dn_f, dn_b, p, dn_g, sd_f, sd_b, p, p, sd_d, sd_g, lr_h, p, ml_f, ml_b, p, ml_g, w)


def _ffn_kernel(nt, nk, final, x_ref, xp_ref, xn_ref, ng_ref, sc_ref, sh_ref, g2_ref, fg_ref,
                wu_ref, wg_ref, cw_ref, wd_ref, o_ref, h_scr, g_scr, acc_scr):
    i = pl.program_id(0)
    k = pl.program_id(1)
    tm = x_ref.shape[0]

    @pl.when(k == 0)
    def _():
        _fill_halo_tile(h_scr, i, nt, x_ref, xp_ref, xn_ref, ng_ref[...], sc_ref[...], sh_ref[...])
        acc_scr[...] = jnp.zeros_like(acc_scr)

    u = jnp.dot(h_scr[HALO:HALO + tm, :], wu_ref[...], preferred_element_type=F32)
    g_scr[...] = jnp.dot(h_scr[...], wg_ref[...], preferred_element_type=F32)
    conv = (cw_ref[0:1, :] * g_scr[pl.ds(HALO - 1, tm), :]
            + cw_ref[1:2, :] * g_scr[pl.ds(HALO, tm), :]
            + cw_ref[2:3, :] * g_scr[pl.ds(HALO + 1, tm), :])
    acc_scr[...] += jnp.dot((_silu(conv) * u).astype(BF16), wd_ref[...], preferred_element_type=F32)

    @pl.when(k == nk - 1)
    def _():
        y = x_ref[...] + g2_ref[...] * acc_scr[...]
        if final:
            y = y * lax.rsqrt(jnp.mean(y * y, axis=-1, keepdims=True) + EPS) * fg_ref[...]
        o_ref[...] = y


def _ffn(x, ng, sc, sh, g2, fg, w_up, cw, w_down, l, final):
    t = x.shape[0]
    tm = min(t, 512)
    bk = 512
    nt, nk = t // tm, D_FF // bk
    vec = pl.BlockSpec((1, D_MODEL), lambda i, k: (0, 0))
    return pl.pallas_call(
        functools.partial(_ffn_kernel, nt, nk, final),
        grid=(nt, nk),
        in_specs=_halo_specs(t, tm) + [
                  vec, vec, vec, vec, vec,
                  pl.BlockSpec((None, D_MODEL, bk), lambda i, k: (l, 0, k)),
                  pl.BlockSpec((None, D_MODEL, bk), lambda i, k: (l, 0, nk + k)),
                  pl.BlockSpec((FFN_CONV, bk), lambda i, k: (0, k)),
                  pl.BlockSpec((None, bk, D_MODEL), lambda i, k: (l, k, 0))],
        out_specs=pl.BlockSpec((tm, D_MODEL), lambda i, k: (i, 0)),
        out_shape=jax.ShapeDtypeStruct((t, D_MODEL), F32),
        scratch_shapes=[pltpu.VMEM((tm + 2 * HALO, D_MODEL), BF16),
                        pltpu.VMEM((tm + 2 * HALO, bk), F32),
                        pltpu.VMEM((tm, D_MODEL), F32)],
        compiler_params=_cparams(("arbitrary", "arbitrary")),
        name="ffn",
    )(x, x, x, ng, sc, sh, g2, fg, w_up, w_up, cw, w_down)


PREP_ROWS = 256


def _perm_kernel(wt_ref, o_ref):
    dst = 0
    for name in _DST_ORDER:
        src, width = _SRC[name]
        if width >= 128:
            o_ref[:, dst:dst + width] = wt_ref[src:src + width, :].T.astype(BF16)
            dst += width
    assert dst == COL_GATES
    gate_rows = [wt_ref[_SRC[n][0]:_SRC[n][0] + _SRC[n][1], :]
                 for n in ("dn_a", "dn_b", "ssd_dt", "ml_i", "ml_f")]
    used = sum(r.shape[0] for r in gate_rows)
    gate_rows.append(jnp.zeros((128 - used, PREP_ROWS), F32))
    o_ref[:, COL_GATES:COL_GATES + 128] = jnp.concatenate(gate_rows, axis=0).T.astype(BF16)
    o_ref[:, COL_GATES + 128:N_PROJ] = jnp.zeros((PREP_ROWS, N_PROJ - COL_GATES - 128), BF16)


def _perm_w_in(w):
    assert (LANE_DN_A, LANE_DN_B, LANE_SSD_DT, LANE_ML_I, LANE_ML_F) == (0, 8, 16, 32, 40)
    depth, rows, _ = w.shape
    return pl.pallas_call(
        _perm_kernel,
        grid=(depth, rows // PREP_ROWS),
        in_specs=[pl.BlockSpec((None, D_IN, PREP_ROWS), lambda l, i: (l, 0, i))],
        out_specs=pl.BlockSpec((None, PREP_ROWS, N_PROJ), lambda l, i: (l, i, 0)),
        out_shape=jax.ShapeDtypeStruct((depth, rows, N_PROJ), BF16),
        compiler_params=_cparams(("arbitrary", "arbitrary")),
        name="w_in_layout",
    )(jnp.swapaxes(w, 1, 2))


def _lane_row(pairs):
    tile = jnp.zeros((8, 128), F32)
    for row, lane, vals in pairs:
        vals = vals.reshape(-1).astype(F32)
        tile = tile.at[row, lane:lane + vals.shape[0]].set(vals)
    return tile


def _block_diag(w):
    nb, bw = w.shape[1], w.shape[2]
    eye = jnp.eye(nb, dtype=w.dtype)
    full = jnp.einsum("dnjk,nm->dnjmk", w, eye)
    return full.reshape(2, nb * bw, nb * bw).astype(BF16)


def _ctx_to_cols(a, wcols):
    t, ch = a.shape
    return a.reshape(wcols, t // wcols, ch).transpose(1, 0, 2).reshape(t, ch)


def _ctx_from_cols(a, wcols):
    t, ch = a.shape
    return a.reshape(t // wcols, wcols, ch).transpose(1, 0, 2).reshape(t, ch)


CTX_COLS = 8


def kernel(x, c, ctx, c_ctx, ada_w, ada_b, norm1_g, norm2_g, w_in, dn_conv_w, dn_a_log, dn_dt_bias, dn_norm_g, ssd_conv_w, ssd_conv_b, ssd_a_log, ssd_dt_bias, ssd_d, ssd_norm_g, lru_conv_w, lru_conv_b, lru_w_a, lru_b_a, lru_w_i, lru_b_i, lru_lambda, ml_ig_b, ml_fg_b, ml_norm_g, w_out, ffn_w_up, ffn_conv_w, ffn_w_down, final_norm_g):
    assert x.shape[0] == 1 and c.shape[0] == 1
    lat, hctx = x[0], ctx[0]
    cc = jnp.zeros((8, D_MODEL), F32).at[0].set(c[0]).at[1].set(c_ctx)
    mods = _ada(cc, ada_w, ada_b)
    row = lambda v: v.reshape(1, -1).astype(F32)
    w_in_p = _perm_w_in(w_in)
    w_out_b = w_out.astype(BF16)
    w_up_b = ffn_w_up.astype(BF16)
    w_down_b = ffn_w_down.astype(BF16)

    for l in range(DEPTH):
        mod_l = mods[l, 0].reshape(N_MOD, 1, D_MODEL)
        mod_c = mods[l, 1].reshape(N_MOD, 1, D_MODEL)
        ng1, ng2 = row(norm1_g[l]), row(norm2_g[l])

        dn_par = _lane_row([(0, LANE_DN_A, dn_a_log[l]), (1, LANE_DN_A, dn_dt_bias[l])])
        ssd_par = _lane_row([(0, LANE_SSD_DT, ssd_dt_bias[l]), (1, LANE_SSD_DT, ssd_a_log[l])])
        ml_par = _lane_row([(0, LANE_ML_I, ml_ig_b[l]), (1, LANE_ML_F, ml_fg_b[l])])
        lru_args = (lru_conv_w[l], row(lru_conv_b[l]), _block_diag(lru_w_a[l]), lru_b_a[l],
                    _block_diag(lru_w_i[l]), lru_b_i[l], lru_lambda[l])
        dn_g = row(jnp.tile(dn_norm_g[l], DN_HEADS))
        sd_d = row(jnp.repeat(ssd_d[l], SSD_HEAD_DIM))
        sd_g, ml_g = row(ssd_norm_g[l]), row(ml_norm_g[l])

        conv_w = jnp.concatenate([ssd_conv_w[l], dn_conv_w[l]], axis=1)
        conv_b = jnp.concatenate([row(ssd_conv_b[l]), jnp.zeros((1, 3 * GROUP_W), F32)], axis=1)

        pc = _inproj(hctx, ng1, mod_c[1], mod_c[0], w_in_p, l, conv_w, conv_b)
        dn_cf, dn_cb, dn_s = _dn_scan(pc, dn_par, jnp.zeros((2 * DN_HEADS, DN_DK, DN_DV), F32))
        sd_cf, sd_cb, sd_s = _ssd_scan(
            pc, ssd_par, jnp.zeros((2 * SSD_GROUPS, SSD_STATE, GROUP_W // SSD_GROUPS), F32))
        ml_cf, ml_cb, ml_c, ml_m = _ml_scan(
            pc, ml_par, jnp.zeros((2 * ML_HEADS, ML_DK, 2 * ML_DV), F32), jnp.zeros((1, 128), F32))
        xc_cols = _ctx_to_cols(pc[:, COL_LRU_X:COL_LRU_X + LRU_W], CTX_COLS)
        lr_c, lr_s = _lru_scan(xc_cols, 0, CTX_COLS, *lru_args, jnp.zeros((2, LRU_W), F32))

        pl_ = _inproj(lat, ng1, mod_l[1], mod_l[0], w_in_p, l, conv_w, conv_b)
        dn_lf, dn_lb, _ = _dn_scan(pl_, dn_par, dn_s)
        sd_lf, sd_lb, _ = _ssd_scan(pl_, ssd_par, sd_s)
        ml_lf, ml_lb, _, _ = _ml_scan(pl_, ml_par, ml_c, ml_m)
        lr_l, _ = _lru_scan(pl_, COL_LRU_X // 128, GRID_W, *lru_args, lr_s)

        lat = _outproj(lat, mod_l[2], pl_, dn_lf, dn_lb, dn_g, sd_lf, sd_lb, sd_d, sd_g,
                       lr_l, ml_lf, ml_lb, ml_g, w_out_b, l)
        last = l == DEPTH - 1
        lat = _ffn(lat, ng2, mod_l[4], mod_l[3], mod_l[5], row(final_norm_g), w_up_b,
                   ffn_conv_w[l], w_down_b, l, last)

        if not last:
            lr_cn = _ctx_from_cols(lr_c, CTX_COLS)
            hctx = _outproj(hctx, mod_c[2], pc, dn_cf, dn_cb, dn_g, sd_cf, sd_cb, sd_d, sd_g,
                            lr_cn, ml_cf, ml_cb, ml_g, w_out_b, l)
            hctx = _ffn(hctx, ng2, mod_c[4], mod_c[3], mod_c[5], row(final_norm_g), w_up_b,
                        ffn_conv_w[l], w_down_b, l, False)

    return lat[None]
```
